```python
import jax
import jax.numpy as jnp
from jax import lax
import numpy as np

D_MODEL = 2048
BATCH = 2
SEQ = 4096
DEPTH = 2
DEC_BATCH = 8
DEC_SEQ = 4
PAST_LEN = 16384
PAGE_SIZE = 128

D_MIX = D_MODEL
N_GROUPS = 4
D_GROUP = D_MIX // N_GROUPS
HEAD_DIM = 64
N_HEADS_FOX = D_GROUP // HEAD_DIM
N_HEADS_SB = D_GROUP // HEAD_DIM
POOL_WINDOWS = (2, 4, 8, 16)
POOL_MAX = max(POOL_WINDOWS)
POOL_CH = D_GROUP // len(POOL_WINDOWS)
CONV_W = 4
LRU_BLOCKS = 8
LRU_BLOCK_DIM = D_GROUP // LRU_BLOCKS
LRU_C = 8.0
D_FF = 4 * D_MODEL
D_IN = 9 * D_GROUP + N_HEADS_FOX
QBLOCK = 128
FORGET_BIAS = 2.0
EPS = 1e-6

kernel_name = 'hybrid_pool_rglru_fox_stickbreak_step'


def rms_norm(x, g):
    xf = x.astype(jnp.float32)
    y = xf * lax.rsqrt(jnp.mean(xf * xf, axis=-1, keepdims=True) + EPS)
    return (y * g.astype(jnp.float32)).astype(x.dtype)


def gather_pages(pool, page_table):
    rows = pool[page_table]
    return rows.reshape((rows.shape[0], rows.shape[1] * rows.shape[2]) + rows.shape[3:])


def sweep_queries(fn, q_arrays, q_pos):
    t = q_pos.shape[0]
    if t <= QBLOCK:
        return fn(*q_arrays, q_pos)
    nb = t // QBLOCK
    blocks = tuple(a.reshape((a.shape[0], nb, QBLOCK) + a.shape[2:]).swapaxes(0, 1) for a in q_arrays)
    out = lax.map(lambda args: fn(*args), blocks + (q_pos.reshape(nb, QBLOCK),))
    out = out.swapaxes(0, 1)
    return out.reshape((out.shape[0], t) + out.shape[3:])


def pool_mixer(u, prev, pos0, pool_w, pool_scale):
    b, t, c = u.shape
    p = POOL_MAX - 1
    ext = jnp.concatenate([prev.astype(jnp.float32), u.astype(jnp.float32)], axis=1)
    cs = jnp.concatenate([jnp.zeros((b, 1, c), jnp.float32), jnp.cumsum(ext, axis=1)], axis=1)
    end = cs[:, p + 1:p + 1 + t]
    pos = pos0 + jnp.arange(t)
    means = []
    for g, w in enumerate(POOL_WINDOWS):
        lo, hi = g * POOL_CH, (g + 1) * POOL_CH
        start = cs[:, p + 1 - w:p + 1 - w + t, lo:hi]
        cnt = jnp.minimum(w, pos + 1).astype(jnp.float32)[None, :, None]
        means.append((end[..., lo:hi] - start) / cnt)
    d = jnp.concatenate(means, axis=-1) - ext[:, p:]
    d = d.reshape(b, t, len(POOL_WINDOWS), POOL_CH).astype(u.dtype)
    y = jnp.einsum('btgc,gcd->btgd', d, pool_w).reshape(b, t, c) * pool_scale
    return y, ext[:, -p:].astype(u.dtype)


def rglru_mixer(xb, gate, conv_prev, h0, pos0, conv_w, conv_b, wa, ba, wx, bx, lam):
    b, t, c = xb.shape
    ext = jnp.concatenate([conv_prev.astype(xb.dtype), xb], axis=1)
    xc = conv_b + sum(ext[:, k:k + t] * conv_w[k] for k in range(CONV_W))
    xblk = xc.reshape(b, t, LRU_BLOCKS, LRU_BLOCK_DIM)
    r = jax.nn.sigmoid(jnp.einsum('btnd,nde->btne', xblk, wa).reshape(b, t, c) + ba)
    i = jax.nn.sigmoid(jnp.einsum('btnd,nde->btne', xblk, wx).reshape(b, t, c) + bx)
    log_a = -LRU_C * r.astype(jnp.float32) * jax.nn.softplus(-lam.astype(jnp.float32))
    mult = jnp.sqrt(-jnp.expm1(2.0 * log_a))
    pos = pos0 + jnp.arange(t)
    mult = jnp.where((pos == 0)[None, :, None], 1.0, mult)
    a = jnp.exp(log_a)
    u = mult * (i * xc).astype(jnp.float32)

    def step(h, au):
        h = au[0] * h + au[1]
        return h, h

    h_last, hs = lax.scan(step, h0.astype(jnp.float32), (a.swapaxes(0, 1), u.swapaxes(0, 1)))
    y = jax.nn.gelu(gate) * hs.swapaxes(0, 1).astype(xb.dtype)
    return y, ext[:, -(CONV_W - 1):], h_last.astype(xb.dtype)


def fox_block(q, cq, qpos, k, v, ck_t, kpos):
    s = jnp.einsum('bqhd,bkhd->bhqk', q, k).astype(jnp.float32) * (HEAD_DIM ** -0.5)
    s = s + cq.astype(jnp.float32).transpose(0, 2, 1)[..., None] - ck_t[:, :, None, :]
    s = jnp.where((kpos[None, :] <= qpos[:, None])[None, None], s, -jnp.inf)
    p = jax.nn.softmax(s, axis=-1)
    return jnp.einsum('bhqk,bkhd->bqhd', p.astype(v.dtype), v)


def sb_block(q, qpos, k, v, kpos):
    z = jnp.einsum('bqhd,bkhd->bhqk', q, k).astype(jnp.float32) * (HEAD_DIM ** -0.5)
    mask = (kpos[None, :] < qpos[:, None])[None, None]
    log_1m = jnp.where(mask, jax.nn.log_sigmoid(-z), 0.0)
    rest = lax.cumsum(log_1m, axis=3, reverse=True) - log_1m
    wgt = jnp.where(mask, jnp.exp(jax.nn.log_sigmoid(z) + rest), 0.0)
    return jnp.einsum('bhqk,bkhd->bqhd', wgt.astype(v.dtype), v)


def token_mixers(h, pool_prev, conv_prev, lru_h0, past, w_in, pool_w, pool_scale, conv_w, conv_b,
                 lru_wa, lru_ba, lru_wx, lru_bx, lru_lambda, fox_bf, fox_q_g, fox_k_g, out_g, w_out):
    b, t, _ = h.shape
    G = D_GROUP
    proj = jnp.einsum('btd,de->bte', h, w_in)

    def heads(lo, n):
        return proj[..., lo:lo + n * HEAD_DIM].reshape(b, t, n, HEAD_DIM)

    u_pool = proj[..., 0:G]
    x_lru = proj[..., G:2 * G]
    g_lru = proj[..., 2 * G:3 * G]
    fq = rms_norm(heads(3 * G, N_HEADS_FOX), fox_q_g)
    fk = rms_norm(heads(4 * G, N_HEADS_FOX), fox_k_g)
    fv = heads(5 * G, N_HEADS_FOX)
    f_logit = proj[..., 6 * G:6 * G + N_HEADS_FOX]
    off = 6 * G + N_HEADS_FOX
    sq = heads(off, N_HEADS_SB)
    sk = heads(off + G, N_HEADS_SB)
    sv = heads(off + 2 * G, N_HEADS_SB)
    logf = jax.nn.log_sigmoid((f_logit + fox_bf).astype(jnp.float32)).astype(h.dtype)

    if past is None:
        p_len = 0
        fk_all, fv_all, logf_all, sk_all, sv_all = fk, fv, logf, sk, sv
    else:
        past_fox_kv, past_logf, past_sb_kv = past
        p_len = past_logf.shape[1]
        fk_all = jnp.concatenate([past_fox_kv[:, :, 0].astype(h.dtype), fk], axis=1)
        fv_all = jnp.concatenate([past_fox_kv[:, :, 1].astype(h.dtype), fv], axis=1)
        logf_all = jnp.concatenate([past_logf.astype(h.dtype), logf], axis=1)
        sk_all = jnp.concatenate([past_sb_kv[:, :, 0].astype(h.dtype), sk], axis=1)
        sv_all = jnp.concatenate([past_sb_kv[:, :, 1].astype(h.dtype), sv], axis=1)
    qpos = p_len + jnp.arange(t)
    kpos = jnp.arange(p_len + t)

    c_all = jnp.cumsum(logf_all.astype(jnp.float32), axis=1)
    cq = c_all[:, p_len:]
    ck_t = c_all.transpose(0, 2, 1)
    y_fox = sweep_queries(lambda qb, cqb, pb: fox_block(qb, cqb, pb, fk_all, fv_all, ck_t, kpos), (fq, cq), qpos)
    y_sb = sweep_queries(lambda qb, pb: sb_block(qb, pb, sk_all, sv_all, kpos), (sq,), qpos)

    y_pool, pool_new = pool_mixer(u_pool, pool_prev, p_len, pool_w, pool_scale)
    y_lru, conv_new, lru_new = rglru_mixer(x_lru, g_lru, conv_prev, lru_h0, p_len, conv_w, conv_b,
                                           lru_wa, lru_ba, lru_wx, lru_bx, lru_lambda)

    y_cat = jnp.stack([y_pool, y_lru, y_fox.reshape(b, t, G), y_sb.reshape(b, t, G)], axis=2)
    y_cat = rms_norm(y_cat, out_g.reshape(N_GROUPS, G)).reshape(b, t, D_MIX)
    y = jnp.einsum('btc,cd->btd', y_cat, w_out)
    new_state = (jnp.stack([fk, fv], axis=2), logf, jnp.stack([sk, sv], axis=2), pool_new, conv_new, lru_new)
    return y, new_state


def setup_inputs(seed: int = 0) -> dict:
    key = jax.random.key(seed)
    ks = jax.random.split(key, 32)
    f32 = jnp.float32
    n_pages = PAST_LEN // PAGE_SIZE
    n_used = DEC_BATCH * n_pages
    n_pool = n_used + n_used // 4

    def nrm(k, shape, s):
        return jax.random.normal(k, shape, f32) * s

    x_prompt = nrm(ks[0], (BATCH, SEQ, D_MODEL), 1.0)
    x_sample = nrm(ks[1], (DEC_BATCH, DEC_SEQ, D_MODEL), 1.0)
    cache_fox_kv = nrm(ks[2], (DEPTH, n_pool, PAGE_SIZE, 2, N_HEADS_FOX, HEAD_DIM), 1.0)
    cache_fox_logf = jax.nn.log_sigmoid(FORGET_BIAS + nrm(ks[3], (DEPTH, n_pool, PAGE_SIZE, N_HEADS_FOX), 1.0))
    cache_sb_kv = nrm(ks[4], (DEPTH, n_pool, PAGE_SIZE, 2, N_HEADS_SB, HEAD_DIM), 1.0)
    state_pool = nrm(ks[5], (DEPTH, DEC_BATCH, POOL_MAX - 1, D_GROUP), 1.0)
    state_conv = nrm(ks[6], (DEPTH, DEC_BATCH, CONV_W - 1, D_GROUP), 1.0)
    state_lru = nrm(ks[7], (DEPTH, DEC_BATCH, D_GROUP), 0.5)
    page_table = jax.random.permutation(ks[8], n_pool)[:n_used].reshape(DEC_BATCH, n_pages).astype(jnp.int32)

    norm1_g = 1.0 + nrm(ks[9], (DEPTH, D_MODEL), 0.02)
    w_in = nrm(ks[10], (DEPTH, D_MODEL, D_IN), D_MODEL ** -0.5)
    pool_w = nrm(ks[11], (DEPTH, len(POOL_WINDOWS), POOL_CH, POOL_CH), POOL_CH ** -0.5)
    pool_scale = 1.0 + nrm(ks[12], (DEPTH, D_GROUP), 0.1)
    conv_w = nrm(ks[13], (DEPTH, CONV_W, D_GROUP), CONV_W ** -0.5)
    conv_b = nrm(ks[14], (DEPTH, D_GROUP), 0.01)
    lru_wa = nrm(ks[15], (DEPTH, LRU_BLOCKS, LRU_BLOCK_DIM, LRU_BLOCK_DIM), LRU_BLOCK_DIM ** -0.5)
    lru_ba = nrm(ks[16], (DEPTH, D_GROUP), 0.01)
    lru_wx = nrm(ks[17], (DEPTH, LRU_BLOCKS, LRU_BLOCK_DIM, LRU_BLOCK_DIM), LRU_BLOCK_DIM ** -0.5)
    lru_bx = nrm(ks[18], (DEPTH, D_GROUP), 0.01)
    a_c = jax.random.uniform(ks[19], (DEPTH, D_GROUP), f32, 0.9, 0.999)
    s_l = a_c ** (1.0 / LRU_C)
    lru_lambda = jnp.log(s_l) - jnp.log1p(-s_l)
    fox_bf = FORGET_BIAS + nrm(ks[20], (DEPTH, N_HEADS_FOX), 0.1)
    fox_q_g = 1.0 + nrm(ks[21], (DEPTH, HEAD_DIM), 0.02)
    fox_k_g = 1.0 + nrm(ks[22], (DEPTH, HEAD_DIM), 0.02)
    out_g = 1.0 + nrm(ks[23], (DEPTH, D_MIX), 0.02)
    w_out = nrm(ks[24], (DEPTH, D_MIX, D_MODEL), D_MIX ** -0.5)
    norm2_g = 1.0 + nrm(ks[25], (DEPTH, D_MODEL), 0.02)
    w_up = nrm(ks[26], (DEPTH, D_MODEL, D_FF), D_MODEL ** -0.5)
    w_down = nrm(ks[27], (DEPTH, D_FF, D_MODEL), D_FF ** -0.5)
    return {'x_prompt': x_prompt, 'x_sample': x_sample, 'cache_fox_kv': cache_fox_kv,
            'cache_fox_logf': cache_fox_logf, 'cache_sb_kv': cache_sb_kv, 'state_pool': state_pool,
            'state_conv': state_conv, 'state_lru': state_lru, 'page_table': page_table,
            'norm1_g': norm1_g, 'w_in': w_in, 'pool_w': pool_w, 'pool_scale': pool_scale,
            'conv_w': conv_w, 'conv_b': conv_b, 'lru_wa': lru_wa, 'lru_ba': lru_ba, 'lru_wx': lru_wx,
            'lru_bx': lru_bx, 'lru_lambda': lru_lambda, 'fox_bf': fox_bf, 'fox_q_g': fox_q_g,
            'fox_k_g': fox_k_g, 'out_g': out_g, 'w_out': w_out, 'norm2_g': norm2_g,
            'w_up': w_up, 'w_down': w_down}


def reference(x_prompt, x_sample, cache_fox_kv, cache_fox_logf, cache_sb_kv, state_pool, state_conv,
              state_lru, page_table, norm1_g, w_in, pool_w, pool_scale, conv_w, conv_b, lru_wa, lru_ba,
              lru_wx, lru_bx, lru_lambda, fox_bf, fox_q_g, fox_k_g, out_g, w_out, norm2_g, w_up, w_down):

    def layer(x, l, pool_prev, conv_prev, lru_h0, past):
        y, st = token_mixers(rms_norm(x, norm1_g[l]), pool_prev, conv_prev, lru_h0, past,
                             w_in[l], pool_w[l], pool_scale[l], conv_w[l], conv_b[l],
                             lru_wa[l], lru_ba[l], lru_wx[l], lru_bx[l], lru_lambda[l],
                             fox_bf[l], fox_q_g[l], fox_k_g[l], out_g[l], w_out[l])
        x = x + y
        hid = jax.nn.relu(jnp.einsum('btd,df->btf', rms_norm(x, norm2_g[l]), w_up[l]))
        x = x + jnp.einsum('btf,fd->btd', hid * hid, w_down[l])
        return x, st

    xp = x_prompt
    bp = xp.shape[0]
    st_p = []
    for l in range(DEPTH):
        xp, st = layer(xp, l,
                       jnp.zeros((bp, POOL_MAX - 1, D_GROUP), xp.dtype),
                       jnp.zeros((bp, CONV_W - 1, D_GROUP), xp.dtype),
                       jnp.zeros((bp, D_GROUP), xp.dtype),
                       None)
        st_p.append(st)

    xs = x_sample
    st_s = []
    for l in range(DEPTH):
        past = (gather_pages(cache_fox_kv[l], page_table),
                gather_pages(cache_fox_logf[l], page_table),
                gather_pages(cache_sb_kv[l], page_table))
        xs, st = layer(xs, l, state_pool[l], state_conv[l], state_lru[l], past)
        st_s.append(st)

    def stk(sts, j):
        return jnp.stack([s[j] for s in sts], axis=0)

    return (xp, xs,
            stk(st_p, 0), stk(st_p, 1), stk(st_p, 2), stk(st_p, 3), stk(st_p, 4), stk(st_p, 5),
            stk(st_s, 0), stk(st_s, 1), stk(st_s, 2), stk(st_s, 3), stk(st_s, 4), stk(st_s, 5))
```

```python
import functools

import numpy as np
import jax
import jax.numpy as jnp
from jax import lax
from jax.experimental import pallas as pl
from jax.experimental.pallas import tpu as pltpu

F32 = jnp.float32
BF16 = jnp.bfloat16

D_GROUP = 512
HEAD_DIM = 64
N_HEADS = D_GROUP // HEAD_DIM
POOL_WINDOWS = (2, 4, 8, 16)
POOL_MAX = max(POOL_WINDOWS)
POOL_CH = D_GROUP // len(POOL_WINDOWS)
CONV_W = 4
LRU_C = 8.0
EPS = 1e-6
PAGE = 128
LANES = 128
HIST_P = 16
HIST_C = 8
NEG_BIG = -1e30
MIB = 1024 * 1024


def _cparams(sem, vmem_mib=48):
    return pltpu.CompilerParams(dimension_semantics=sem, vmem_limit_bytes=vmem_mib * MIB)


def _split_bf16(x, parts):
    out = []
    r = x
    for _ in range(parts - 1):
        h = r.astype(BF16)
        out.append(h)
        r = r - h.astype(F32)
    out.append(r.astype(BF16))
    return out


def _softplus(x):
    return jnp.maximum(x, 0.0) + jnp.log1p(jnp.exp(-jnp.abs(x)))


def _log_sigmoid(x):
    return -_softplus(-x)


def _dot_nt(a, b):
    return lax.dot_general(a, b, (((1,), (1,)), ((), ())), preferred_element_type=F32)


def _dot(a, b):
    return jnp.dot(a, b, preferred_element_type=F32)


def _inproj_kernel(x_ref, g_ref, w_ref, wf_ref, o_ref, of_ref, xn_ref):
    @pl.when(pl.program_id(1) == 0)
    def _():
        x = x_ref[...]
        ms = jnp.mean(x * x, axis=-1, keepdims=True)
        xn = (x * lax.rsqrt(ms + EPS) * g_ref[...]).astype(BF16)
        xn_ref[...] = xn
        of_ref[...] = _dot(xn, wf_ref[...])

    o_ref[...] = _dot(xn_ref[...], w_ref[...])


def _inproj(x, g, w, wf, tm):
    m, d = x.shape
    n = w.shape[1]
    tn = D_GROUP
    return pl.pallas_call(
        _inproj_kernel,
        out_shape=(jax.ShapeDtypeStruct((m, n), F32), jax.ShapeDtypeStruct((m, LANES), F32)),
        grid=(m // tm, n // tn),
        in_specs=[
            pl.BlockSpec((tm, d), lambda i, j: (i, 0)),
            pl.BlockSpec((1, d), lambda i, j: (0, 0)),
            pl.BlockSpec((d, tn), lambda i, j: (0, j)),
            pl.BlockSpec((d, LANES), lambda i, j: (0, 0)),
        ],
        out_specs=(
            pl.BlockSpec((tm, tn), lambda i, j: (i, j)),
            pl.BlockSpec((tm, LANES), lambda i, j: (i, 0)),
        ),
        scratch_shapes=[pltpu.VMEM((tm, d), BF16)],
        compiler_params=_cparams(("arbitrary", "arbitrary")),
        name="inproj",
    )(x, g, w, wf)


def _prep_kernel(fq_ref, fk_ref, fv_ref, sq_ref, sk_ref, sv_ref, fl_ref, gq_ref, gk_ref, bf_ref,
                 seg_ref, fkv_ref, skv_ref, lf_ref, c_ref, ct_ref,
                 qf_ref, kf_ref, vf_ref, qs_ref, ks_ref, vs_ref, carry_ref, *, seq, tm):
    i = pl.program_id(0)
    scale = HEAD_DIM ** -0.5

    def head_norm(x, g):
        hi, lo = _split_bf16(x * x, 2)
        ss = _dot(hi, seg_ref[...]) + _dot(lo, seg_ref[...])
        return x * lax.rsqrt(ss * (1.0 / HEAD_DIM) + EPS) * g

    fq = head_norm(fq_ref[...], gq_ref[...])
    fk = head_norm(fk_ref[...], gk_ref[...])
    fv = fv_ref[...]
    qf_ref[...] = (fq * scale).astype(BF16)
    kf_ref[...] = fk.astype(BF16)
    vf_ref[...] = fv.astype(BF16)
    fkv_ref[:, 0:D_GROUP] = fk
    fkv_ref[:, D_GROUP:2 * D_GROUP] = fv
    sk = sk_ref[...]
    sv = sv_ref[...]
    qs_ref[...] = (sq_ref[...] * scale).astype(BF16)
    ks_ref[...] = sk.astype(BF16)
    vs_ref[...] = sv.astype(BF16)
    skv_ref[:, 0:D_GROUP] = sk
    skv_ref[:, D_GROUP:2 * D_GROUP] = sv

    lf = _log_sigmoid(fl_ref[...] + bf_ref[...])
    lf_ref[...] = lf

    r = lax.broadcasted_iota(jnp.int32, (tm, tm), 0)
    c = lax.broadcasted_iota(jnp.int32, (tm, tm), 1)
    keep = c <= r
    if seq < tm:
        keep = jnp.logical_and(keep, (r // seq) == (c // seq))
    tri = jnp.where(keep, 1.0, 0.0).astype(BF16)
    cs = None
    for part in _split_bf16(lf, 3):
        t = _dot(tri, part)
        cs = t if cs is None else cs + t
    if seq >= tm:
        @pl.when((i * tm) % seq == 0)
        def _():
            carry_ref[...] = jnp.zeros_like(carry_ref)
        cs = cs + carry_ref[...]
        carry_ref[...] = cs[tm - 1:tm, :]
    c_ref[...] = cs
    ct_ref[...] = cs.T[0:N_HEADS, :]


def _prep(proj, flog, gq, gk, bf, seg, seq, tm):
    m = proj.shape[0]
    col = lambda s: pl.BlockSpec((tm, D_GROUP), lambda i, s=s: (i, s))
    row = lambda w: pl.BlockSpec((1, w), lambda i: (0, 0))
    bf_out = jax.ShapeDtypeStruct((m, D_GROUP), BF16)
    return pl.pallas_call(
        functools.partial(_prep_kernel, seq=seq, tm=tm),
        out_shape=(
            jax.ShapeDtypeStruct((m, 2 * D_GROUP), F32),
            jax.ShapeDtypeStruct((m, 2 * D_GROUP), F32),
            jax.ShapeDtypeStruct((m, LANES), F32),
            jax.ShapeDtypeStruct((m, LANES), F32),
            jax.ShapeDtypeStruct((N_HEADS, m), F32),
            bf_out, bf_out, bf_out, bf_out, bf_out, bf_out,
        ),
        grid=(m // tm,),
        in_specs=[col(3), col(4), col(5), col(6), col(7), col(8),
                  pl.BlockSpec((tm, LANES), lambda i: (i, 0)),
                  row(D_GROUP), row(D_GROUP), row(LANES),
                  pl.BlockSpec((D_GROUP, D_GROUP), lambda i: (0, 0))],
        out_specs=(
            pl.BlockSpec((tm, 2 * D_GROUP), lambda i: (i, 0)),
            pl.BlockSpec((tm, 2 * D_GROUP), lambda i: (i, 0)),
            pl.BlockSpec((tm, LANES), lambda i: (i, 0)),
            pl.BlockSpec((tm, LANES), lambda i: (i, 0)),
            pl.BlockSpec((N_HEADS, tm), lambda i: (0, i)),
        ) + tuple(pl.BlockSpec((tm, D_GROUP), lambda i: (i, 0)) for _ in range(6)),
        scratch_shapes=[pltpu.VMEM((1, LANES), F32)],
        compiler_params=_cparams(("arbitrary",)),
        name="prep",
    )(proj, proj, proj, proj, proj, proj, flog, gq, gk, bf, seg)


def _seqmix_kernel(u_ref, x_ref, gate_ref, pprev_ref, cprev_ref, h0_ref,
                   pw_ref, pscale_ref, cw_ref, cb_ref, wax_ref, ba_ref, bx_ref, lam_ref,
                   yp_ref, yl_ref, pnew_ref, cnew_ref, hnew_ref,
                   extp_ref, extc_ref, h_ref, *, tt, tv, pos0):
    ti = pl.program_id(1)

    @pl.when(ti == 0)
    def _():
        extp_ref[0:HIST_P, :] = pprev_ref[...]
        extc_ref[0:HIST_C, :] = cprev_ref[...]
        h_ref[...] = h0_ref[...]

    extp_ref[HIST_P:HIST_P + tt, :] = u_ref[...]
    extc_ref[HIST_C:HIST_C + tt, :] = x_ref[...]
    pos = pos0 + ti * tt + lax.broadcasted_iota(jnp.int32, (tt, 1), 0)

    for g, w in enumerate(POOL_WINDOWS):
        lanes = slice(g * POOL_CH, (g + 1) * POOL_CH)
        tok = extp_ref[HIST_P:HIST_P + tt, lanes]
        win = tok
        for j in range(1, w):
            win = win + extp_ref[HIST_P - j:HIST_P - j + tt, lanes]
        cnt = jnp.minimum(w, pos + 1).astype(F32)
        d = win / cnt - tok
        y = _dot(d.astype(BF16), pw_ref[g]) * pscale_ref[:, lanes]
        yp_ref[:, lanes] = y
    hist = extp_ref[tv:tv + HIST_P, :]
    extp_ref[0:HIST_P, :] = hist
    pnew_ref[...] = hist

    base = HIST_C - (CONV_W - 1)
    xc = extc_ref[base:base + tt, :] * cw_ref[0:1, :]
    for k in range(1, CONV_W):
        xc = xc + extc_ref[base + k:base + k + tt, :] * cw_ref[k:k + 1, :]
    xc = cb_ref[...] + xc
    chist = extc_ref[tv:tv + HIST_C, :]
    extc_ref[0:HIST_C, :] = chist
    cnew_ref[...] = chist

    ri = _dot(xc.astype(BF16), wax_ref[...])
    r = jax.nn.sigmoid(ri[:, 0:D_GROUP] + ba_ref[...])
    gi = jax.nn.sigmoid(ri[:, D_GROUP:2 * D_GROUP] + bx_ref[...])
    log_a = -LRU_C * r * _softplus(-lam_ref[...])
    a = jnp.exp(log_a)
    mult = jnp.sqrt(-jnp.tanh(log_a) * (a * a + 1.0))
    mult = jnp.where(pos == 0, 1.0, mult)
    b = mult * (gi * xc)

    rows = lax.broadcasted_iota(jnp.int32, (tt, 1), 0)
    d = 1
    while d < tt:
        ok = rows >= d
        a_s = jnp.where(ok, pltpu.roll(a, d, 0), 1.0)
        b_s = jnp.where(ok, pltpu.roll(b, d, 0), 0.0)
        b = a * b_s + b
        a = a * a_s
        d *= 2
    h = b + a * h_ref[...]
    hlast = h[tv - 1:tv, :]
    h_ref[...] = hlast
    hnew_ref[...] = hlast
    yl_ref[...] = jax.nn.gelu(gate_ref[...]) * h


def _seqmix(proj3, pprev, cprev, h0, pw, pscale, cw, cb, wax, ba, bx, lam, tt, tv, pos0):
    b, t, _ = proj3.shape
    nt = t // tt
    col = lambda s: pl.BlockSpec((None, tt, D_GROUP), lambda bi, ti, s=s: (bi, ti, s))
    per_b = lambda r: pl.BlockSpec((None, r, D_GROUP), lambda bi, ti: (bi, 0, 0))
    const2 = lambda shp: pl.BlockSpec(shp, lambda bi, ti: (0, 0))
    return pl.pallas_call(
        functools.partial(_seqmix_kernel, tt=tt, tv=tv, pos0=pos0),
        out_shape=(
            jax.ShapeDtypeStruct((b, t, D_GROUP), F32),
            jax.ShapeDtypeStruct((b, t, D_GROUP), F32),
            jax.ShapeDtypeStruct((b, HIST_P, D_GROUP), F32),
            jax.ShapeDtypeStruct((b, HIST_C, D_GROUP), F32),
            jax.ShapeDtypeStruct((b, 1, D_GROUP), F32),
        ),
        grid=(b, nt),
        in_specs=[col(0), col(1), col(2), per_b(HIST_P), per_b(HIST_C), per_b(1),
                  pl.BlockSpec((len(POOL_WINDOWS), POOL_CH, POOL_CH), lambda bi, ti: (0, 0, 0)),
                  const2((1, D_GROUP)), const2((HIST_C, D_GROUP)), const2((1, D_GROUP)),
                  const2((D_GROUP, 2 * D_GROUP)), const2((1, D_GROUP)), const2((1, D_GROUP)),
                  const2((1, D_GROUP))],
        out_specs=(
            pl.BlockSpec((None, tt, D_GROUP), lambda bi, ti: (bi, ti, 0)),
            pl.BlockSpec((None, tt, D_GROUP), lambda bi, ti: (bi, ti, 0)),
            per_b(HIST_P), per_b(HIST_C), per_b(1),
        ),
        scratch_shapes=[pltpu.VMEM((HIST_P + tt, D_GROUP), F32),
                        pltpu.VMEM((HIST_C + tt, D_GROUP), F32),
                        pltpu.VMEM((1, D_GROUP), F32)],
        compiler_params=_cparams(("arbitrary", "arbitrary")),
        name="seqmix",
    )(proj3, proj3, proj3, pprev, cprev, h0, pw, pscale, cw, cb, wax, ba, bx, lam)


def _pair_tables(n, reverse):
    qi, ki = [], []
    for q in range(n):
        ks = range(q, -1, -1) if reverse else range(q + 1)
        for k in ks:
            qi.append(q)
            ki.append(k)
    return jnp.asarray(np.array(qi, np.int32)), jnp.asarray(np.array(ki, np.int32))


def _fox_prompt_kernel(qi_ref, ki_ref, q_ref, k_ref, v_ref, c_ref, ct_ref, o_ref,
                       m_ref, l_ref, acc_ref, *, tq):
    s_id = pl.program_id(1)
    qi = qi_ref[s_id]
    ki = ki_ref[s_id]
    lane = lax.broadcasted_iota(jnp.int32, (1, LANES), 1)

    @pl.when(ki == 0)
    def _():
        m_ref[...] = jnp.full_like(m_ref, NEG_BIG)
        l_ref[...] = jnp.zeros_like(l_ref)
        acc_ref[...] = jnp.zeros_like(acc_ref)

    def step(masked):
        if masked:
            row = lax.broadcasted_iota(jnp.int32, (tq, tq), 0)
            colm = lax.broadcasted_iota(jnp.int32, (tq, tq), 1)
            causal = colm <= row
        for p in range(N_HEADS // 2):
            lanes = slice(p * LANES, (p + 1) * LANES)
            qp = q_ref[:, lanes]
            kp = k_ref[:, lanes]
            vp = v_ref[:, lanes]
            pv = None
            alphas = []
            for e in range(2):
                h = 2 * p + e
                half = (lane // HEAD_DIM) == e
                qm = jnp.where(half, qp, jnp.zeros_like(qp))
                vm = jnp.where(half, vp, jnp.zeros_like(vp))
                s = _dot_nt(qm, kp) - ct_ref[h:h + 1, :]
                if masked:
                    s = jnp.where(causal, s, NEG_BIG)
                cq = c_ref[:, h:h + 1]
                m_old = m_ref[h]
                m_new = jnp.maximum(m_old, jnp.max(s, axis=-1, keepdims=True) + cq)
                alpha = jnp.exp(m_old - m_new)
                pr = jnp.exp(s - (m_new - cq))
                l_ref[h] = alpha * l_ref[h] + jnp.sum(pr, axis=-1, keepdims=True)
                m_ref[h] = m_new
                t = _dot(pr.astype(BF16), vm)
                pv = t if pv is None else pv + t
                alphas.append(alpha)
            acc_ref[p] = acc_ref[p] * jnp.where(lane < HEAD_DIM, alphas[0], alphas[1]) + pv

    @pl.when(ki < qi)
    def _():
        step(False)

    @pl.when(ki == qi)
    def _():
        step(True)
        for p in range(N_HEADS // 2):
            den = jnp.where(lane < HEAD_DIM, l_ref[2 * p], l_ref[2 * p + 1])
            o_ref[:, p * LANES:(p + 1) * LANES] = acc_ref[p] / den


def _fox_prompt(q, k, v, c, ct, tq):
    b, t, _ = q.shape
    nq = t // tq
    qi, ki = _pair_tables(nq, reverse=False)
    qspec = pl.BlockSpec((None, tq, D_GROUP), lambda bi, s, qi, ki: (bi, qi[s], 0))
    kspec = pl.BlockSpec((None, tq, D_GROUP), lambda bi, s, qi, ki: (bi, ki[s], 0))
    grid_spec = pltpu.PrefetchScalarGridSpec(
        num_scalar_prefetch=2,
        grid=(b, int(qi.shape[0])),
        in_specs=[qspec, kspec, kspec,
                  pl.BlockSpec((None, tq, LANES), lambda bi, s, qi, ki: (bi, qi[s], 0)),
                  pl.BlockSpec((None, N_HEADS, tq), lambda bi, s, qi, ki: (bi, 0, ki[s]))],
        out_specs=pl.BlockSpec((None, tq, D_GROUP), lambda bi, s, qi, ki: (bi, qi[s], 0)),
        scratch_shapes=[pltpu.VMEM((N_HEADS, tq, 1), F32),
                        pltpu.VMEM((N_HEADS, tq, 1), F32),
                        pltpu.VMEM((N_HEADS // 2, tq, LANES), F32)],
    )
    return pl.pallas_call(
        functools.partial(_fox_prompt_kernel, tq=tq),
        out_shape=jax.ShapeDtypeStruct((b, t, D_GROUP), F32),
        grid_spec=grid_spec,
        compiler_params=_cparams(("arbitrary", "arbitrary")),
        name="fox_prompt",
    )(qi, ki, q, k, v, c, ct)


def _sb_prompt_kernel(qi_ref, ki_ref, q_ref, k_ref, v_ref, tri_ref, o_ref, carry_ref, acc_ref, *, tq):
    s_id = pl.program_id(1)
    qi = qi_ref[s_id]
    ki = ki_ref[s_id]
    lane = lax.broadcasted_iota(jnp.int32, (1, LANES), 1)

    def step(masked):
        if masked:
            row = lax.broadcasted_iota(jnp.int32, (tq, tq), 0)
            colm = lax.broadcasted_iota(jnp.int32, (tq, tq), 1)
            strict = colm < row
        for p in range(N_HEADS // 2):
            lanes = slice(p * LANES, (p + 1) * LANES)
            qp = q_ref[:, lanes]
            kp = k_ref[:, lanes]
            vp = v_ref[:, lanes]
            pv = None
            for e in range(2):
                h = 2 * p + e
                half = (lane // HEAD_DIM) == e
                qm = jnp.where(half, qp, jnp.zeros_like(qp))
                vm = jnp.where(half, vp, jnp.zeros_like(vp))
                z = _dot_nt(qm, kp)
                l1m = -_softplus(z)
                if masked:
                    l1m = jnp.where(strict, l1m, 0.0)
                hi, lo = _split_bf16(l1m, 2)
                rest = _dot(hi, tri_ref[...]) + _dot(lo, tri_ref[...])
                if masked:
                    carry = jnp.zeros((tq, 1), F32)
                else:
                    carry = carry_ref[h]
                wgt = jnp.exp(z + l1m + rest + carry)
                if masked:
                    wgt = jnp.where(strict, wgt, 0.0)
                carry_ref[h] = carry + rest[:, 0:1] + l1m[:, 0:1]
                t = _dot(wgt.astype(BF16), vm)
                pv = t if pv is None else pv + t
            if masked:
                acc_ref[p] = pv
            else:
                acc_ref[p] = acc_ref[p] + pv

    @pl.when(ki == qi)
    def _():
        step(True)

    @pl.when(ki < qi)
    def _():
        step(False)

    @pl.when(ki == 0)
    def _():
        for p in range(N_HEADS // 2):
            o_ref[:, p * LANES:(p + 1) * LANES] = acc_ref[p]


def _sb_prompt(q, k, v, tri, tq):
    b, t, _ = q.shape
    nq = t // tq
    qi, ki = _pair_tables(nq, reverse=True)
    qspec = pl.BlockSpec((None, tq, D_GROUP), lambda bi, s, qi, ki: (bi, qi[s], 0))
    kspec = pl.BlockSpec((None, tq, D_GROUP), lambda bi, s, qi, ki: (bi, ki[s], 0))
    grid_spec = pltpu.PrefetchScalarGridSpec(
        num_scalar_prefetch=2,
        grid=(b, int(qi.shape[0])),
        in_specs=[qspec, kspec, kspec, pl.BlockSpec((tq, tq), lambda bi, s, qi, ki: (0, 0))],
        out_specs=pl.BlockSpec((None, tq, D_GROUP), lambda bi, s, qi, ki: (bi, qi[s], 0)),
        scratch_shapes=[pltpu.VMEM((N_HEADS, tq, 1), F32),
                        pltpu.VMEM((N_HEADS // 2, tq, LANES), F32)],
    )
    return pl.pallas_call(
        functools.partial(_sb_prompt_kernel, tq=tq),
        out_shape=jax.ShapeDtypeStruct((b, t, D_GROUP), F32),
        grid_spec=grid_spec,
        compiler_params=_cparams(("arbitrary", "arbitrary")),
        name="sb_prompt",
    )(qi, ki, q, k, v, tri)


def _sample_rows(t_new):
    return t_new * N_HEADS


def _extract_heads(acc, t_new):
    hrow = lax.broadcasted_iota(jnp.int32, (N_HEADS, D_GROUP), 0)
    hcol = lax.broadcasted_iota(jnp.int32, (N_HEADS, D_GROUP), 1) // HEAD_DIM
    own = hrow == hcol
    outs = []
    for t in range(t_new):
        blk = acc[t * N_HEADS:(t + 1) * N_HEADS, :]
        outs.append(jnp.sum(jnp.where(own, blk, 0.0), axis=0, keepdims=True))
    return outs


def _fox_sample_kernel(pt_ref, wq_ref, new_ref, cache_ref, gt_ref, lft_ref, grow_ref, tri_ref, o_ref,
                       m_ref, l_ref, acc_ref, dcar_ref, *, t_new, n_pages):
    p = pl.program_id(1)
    rows = _sample_rows(t_new)

    def process(kv_ref, bias8, mask):
        k = kv_ref[:, 0:D_GROUP].astype(BF16)
        v = kv_ref[:, D_GROUP:2 * D_GROUP].astype(BF16)
        s = _dot_nt(wq_ref[...], k)
        s = s + jnp.concatenate([bias8] * t_new, axis=0) + grow_ref[...]
        if mask is not None:
            s = jnp.where(mask, s, NEG_BIG)
        m_old = m_ref[...]
        m_new = jnp.maximum(m_old, jnp.max(s, axis=-1, keepdims=True))
        alpha = jnp.exp(m_old - m_new)
        pr = jnp.exp(s - m_new)
        l_ref[...] = alpha * l_ref[...] + jnp.sum(pr, axis=-1, keepdims=True)
        acc_ref[...] = alpha * acc_ref[...] + _dot(pr.astype(BF16), v)
        m_ref[...] = m_new

    @pl.when(p == 0)
    def _():
        m_ref[...] = jnp.full_like(m_ref, NEG_BIG)
        l_ref[...] = jnp.zeros_like(l_ref)
        acc_ref[...] = jnp.zeros_like(acc_ref)
        dcar_ref[...] = jnp.zeros_like(dcar_ref)
        tok = lax.broadcasted_iota(jnp.int32, (rows, PAGE), 0) // N_HEADS
        key = lax.broadcasted_iota(jnp.int32, (rows, PAGE), 1)
        process(new_ref, -gt_ref[...], key <= tok)

    @pl.when(p > 0)
    def _():
        lf = lft_ref[...]
        parts = [x.astype(F32) for x in _split_bf16(lf, 3)]
        stacked = jnp.concatenate(parts + [jnp.zeros_like(lf)], axis=0).astype(BF16)
        r = _dot(stacked, tri_ref[...])
        later = r[0:N_HEADS] + r[N_HEADS:2 * N_HEADS] + r[2 * N_HEADS:3 * N_HEADS]
        process(cache_ref, later + dcar_ref[...], None)
        dcar_ref[...] = dcar_ref[...] + later[:, 0:1] + lf[:, 0:1]

    @pl.when(p == n_pages)
    def _():
        out = acc_ref[...] / l_ref[...]
        for t, rowv in enumerate(_extract_heads(out, t_new)):
            o_ref[t:t + 1, :] = rowv


def _sb_sample_kernel(pt_ref, wq_ref, new_ref, cache_ref, tri_ref, o_ref, acc_ref, car_ref,
                      *, t_new, n_pages):
    p = pl.program_id(1)
    rows = _sample_rows(t_new)

    def process(kv_ref, mask):
        k = kv_ref[:, 0:D_GROUP].astype(BF16)
        v = kv_ref[:, D_GROUP:2 * D_GROUP].astype(BF16)
        z = _dot_nt(wq_ref[...], k)
        l1m = -_softplus(z)
        if mask is not None:
            l1m = jnp.where(mask, l1m, 0.0)
        hi, lo = _split_bf16(l1m, 2)
        stacked = jnp.concatenate([hi.astype(F32), lo.astype(F32)], axis=0).astype(BF16)
        r = _dot(stacked, tri_ref[...])
        rest = r[0:rows] + r[rows:2 * rows]
        wgt = jnp.exp(z + l1m + rest + car_ref[...])
        if mask is not None:
            wgt = jnp.where(mask, wgt, 0.0)
        acc_ref[...] = acc_ref[...] + _dot(wgt.astype(BF16), v)
        car_ref[...] = car_ref[...] + rest[:, 0:1] + l1m[:, 0:1]

    @pl.when(p == 0)
    def _():
        acc_ref[...] = jnp.zeros_like(acc_ref)
        car_ref[...] = jnp.zeros_like(car_ref)
        tok = lax.broadcasted_iota(jnp.int32, (rows, PAGE), 0) // N_HEADS
        key = lax.broadcasted_iota(jnp.int32, (rows, PAGE), 1)
        process(new_ref, key < tok)

    @pl.when(p > 0)
    def _():
        process(cache_ref, None)

    @pl.when(p == n_pages)
    def _():
        for t, rowv in enumerate(_extract_heads(acc_ref[...], t_new)):
            o_ref[t:t + 1, :] = rowv


def _page_index(n_pages):
    def idx(bi, p, pt):
        return pt[bi * n_pages + (n_pages - jnp.maximum(p, 1))]
    return idx


def _fox_sample(pt, wq, new_kv, cache, layer, gt, lft_cache, grow, tri, t_new):
    b = wq.shape[0]
    n_pages = pt.shape[0] // b
    rows = _sample_rows(t_new)
    page = _page_index(n_pages)
    grid_spec = pltpu.PrefetchScalarGridSpec(
        num_scalar_prefetch=1,
        grid=(b, n_pages + 1),
        in_specs=[
            pl.BlockSpec((None, rows, D_GROUP), lambda bi, p, pt: (bi, 0, 0)),
            pl.BlockSpec((None, PAGE, 2 * D_GROUP), lambda bi, p, pt: (bi, 0, 0)),
            pl.BlockSpec((None, None, PAGE, 2 * D_GROUP), lambda bi, p, pt: (layer, page(bi, p, pt), 0, 0)),
            pl.BlockSpec((None, N_HEADS, PAGE), lambda bi, p, pt: (bi, 0, 0)),
            pl.BlockSpec((None, None, N_HEADS, PAGE), lambda bi, p, pt: (layer, page(bi, p, pt), 0, 0)),
            pl.BlockSpec((None, rows, 1), lambda bi, p, pt: (bi, 0, 0)),
            pl.BlockSpec((PAGE, PAGE), lambda bi, p, pt: (0, 0)),
        ],
        out_specs=pl.BlockSpec((None, t_new, D_GROUP), lambda bi, p, pt: (bi, 0, 0)),
        scratch_shapes=[pltpu.VMEM((rows, 1), F32), pltpu.VMEM((rows, 1), F32),
                        pltpu.VMEM((rows, D_GROUP), F32), pltpu.VMEM((N_HEADS, 1), F32)],
    )
    return pl.pallas_call(
        functools.partial(_fox_sample_kernel, t_new=t_new, n_pages=n_pages),
        out_shape=jax.ShapeDtypeStruct((b, t_new, D_GROUP), F32),
        grid_spec=grid_spec,
        compiler_params=_cparams(("arbitrary", "arbitrary")),
        name="fox_sample",
    )(pt, wq, new_kv, cache, gt, lft_cache, grow, tri)


def _sb_sample(pt, wq, new_kv, cache, layer, tri, t_new):
    b = wq.shape[0]
    n_pages = pt.shape[0] // b
    rows = _sample_rows(t_new)
    page = _page_index(n_pages)
    grid_spec = pltpu.PrefetchScalarGridSpec(
        num_scalar_prefetch=1,
        grid=(b, n_pages + 1),
        in_specs=[
            pl.BlockSpec((None, rows, D_GROUP), lambda bi, p, pt: (bi, 0, 0)),
            pl.BlockSpec((None, PAGE, 2 * D_GROUP), lambda bi, p, pt: (bi, 0, 0)),
            pl.BlockSpec((None, None, PAGE, 2 * D_GROUP), lambda bi, p, pt: (layer, page(bi, p, pt), 0, 0)),
            pl.BlockSpec((PAGE, PAGE), lambda bi, p, pt: (0, 0)),
        ],
        out_specs=pl.BlockSpec((None, t_new, D_GROUP), lambda bi, p, pt: (bi, 0, 0)),
        scratch_shapes=[pltpu.VMEM((rows, D_GROUP), F32), pltpu.VMEM((rows, 1), F32)],
    )
    return pl.pallas_call(
        functools.partial(_sb_sample_kernel, t_new=t_new, n_pages=n_pages),
        out_shape=jax.ShapeDtypeStruct((b, t_new, D_GROUP), F32),
        grid_spec=grid_spec,
        compiler_params=_cparams(("arbitrary", "arbitrary")),
        name="sb_sample",
    )(pt, wq, new_kv, cache, tri)


def _outproj_kernel(x_ref, y0_ref, y1_ref, y2_ref, y3_ref, g_ref, w_ref, o_ref):
    acc = x_ref[...]
    for gi, y_ref in enumerate((y0_ref, y1_ref, y2_ref, y3_ref)):
        y = y_ref[...]
        ms = jnp.mean(y * y, axis=-1, keepdims=True)
        yn = (y * lax.rsqrt(ms + EPS) * g_ref[gi:gi + 1, :]).astype(BF16)
        acc = acc + _dot(yn, w_ref[gi * D_GROUP:(gi + 1) * D_GROUP, :])
    o_ref[...] = acc


def _outproj(x, ys, g, w, tm):
    m, d = x.shape
    yspec = pl.BlockSpec((tm, D_GROUP), lambda i: (i, 0))
    return pl.pallas_call(
        _outproj_kernel,
        out_shape=jax.ShapeDtypeStruct((m, d), F32),
        grid=(m // tm,),
        in_specs=[pl.BlockSpec((tm, d), lambda i: (i, 0)), yspec, yspec, yspec, yspec,
                  pl.BlockSpec((4, D_GROUP), lambda i: (0, 0)),
                  pl.BlockSpec((4 * D_GROUP, d), lambda i: (0, 0))],
        out_specs=pl.BlockSpec((tm, d), lambda i: (i, 0)),
        compiler_params=_cparams(("arbitrary",)),
        name="outproj",
    )(x, *ys, g, w)


def _mlp_kernel(x_ref, g_ref, wu_ref, wd_ref, o_ref, xn_ref):
    @pl.when(pl.program_id(1) == 0)
    def _():
        x = x_ref[...]
        ms = jnp.mean(x * x, axis=-1, keepdims=True)
        xn_ref[...] = (x * lax.rsqrt(ms + EPS) * g_ref[...]).astype(BF16)
        o_ref[...] = x

    hid = jnp.maximum(_dot(xn_ref[...], wu_ref[...]), 0.0)
    o_ref[...] += _dot((hid * hid).astype(BF16), wd_ref[...])


def _mlp(x, g, wu, wd, tm, tf):
    m, d = x.shape
    f = wu.shape[1]
    return pl.pallas_call(
        _mlp_kernel,
        out_shape=jax.ShapeDtypeStruct((m, d), F32),
        grid=(m // tm, f // tf),
        in_specs=[pl.BlockSpec((tm, d), lambda i, j: (i, 0)),
                  pl.BlockSpec((1, d), lambda i, j: (0, 0)),
                  pl.BlockSpec((d, tf), lambda i, j: (0, j)),
                  pl.BlockSpec((tf, d), lambda i, j: (j, 0))],
        out_specs=pl.BlockSpec((tm, d), lambda i, j: (i, 0)),
        scratch_shapes=[pltpu.VMEM((tm, d), BF16)],
        compiler_params=_cparams(("arbitrary", "arbitrary")),
        name="mlp",
    )(x, g, wu, wd)


def _block_diag(w):
    n, d, _ = w.shape
    eye = jnp.eye(n, dtype=w.dtype)
    return (eye[:, None, :, None] * w[:, :, None, :]).reshape(n * d, n * d)


def _layer_weights(l, norm1_g, w_in, pool_w, pool_scale, conv_w, conv_b, lru_wa, lru_ba, lru_wx, lru_bx,
                   lru_lambda, fox_bf, fox_q_g, fox_k_g, out_g, w_out, norm2_g, w_up, w_down):
    g = D_GROUP
    wi = w_in[l]
    nf = 6 * g
    w_main = jnp.concatenate([wi[:, :nf], wi[:, nf + N_HEADS:]], axis=1).astype(BF16)
    w_f = jnp.pad(wi[:, nf:nf + N_HEADS], ((0, 0), (0, LANES - N_HEADS))).astype(BF16)
    return dict(
        norm1_g=norm1_g[l][None, :], w_main=w_main, w_f=w_f,
        pool_w=pool_w[l].astype(BF16), pool_scale=pool_scale[l][None, :],
        conv_w=jnp.pad(conv_w[l], ((0, HIST_C - CONV_W), (0, 0))), conv_b=conv_b[l][None, :],
        wax=jnp.concatenate([_block_diag(lru_wa[l]), _block_diag(lru_wx[l])], axis=1).astype(BF16),
        ba=lru_ba[l][None, :], bx=lru_bx[l][None, :], lam=lru_lambda[l][None, :],
        bf=jnp.pad(fox_bf[l], (0, LANES - N_HEADS))[None, :],
        gq=jnp.tile(fox_q_g[l], N_HEADS)[None, :], gk=jnp.tile(fox_k_g[l], N_HEADS)[None, :],
        out_g=out_g[l].reshape(4, g), w_out=w_out[l].astype(BF16),
        norm2_g=norm2_g[l][None, :], w_up=w_up[l].astype(BF16), w_down=w_down[l].astype(BF16),
    )


def _tile(m, pref):
    t = min(m, pref)
    while m % t:
        t //= 2
    return t


def _expand_queries(q, b, t_new):
    q4 = q.reshape(b, t_new, 1, D_GROUP)
    own = (jnp.arange(D_GROUP)[None, :] // HEAD_DIM) == jnp.arange(N_HEADS)[:, None]
    return jnp.where(own[None, None], q4, jnp.zeros_like(q4)).reshape(b, t_new * N_HEADS, D_GROUP)


def kernel(x_prompt, x_sample, cache_fox_kv, cache_fox_logf, cache_sb_kv, state_pool, state_conv, state_lru,
           page_table, norm1_g, w_in, pool_w, pool_scale, conv_w, conv_b, lru_wa, lru_ba, lru_wx, lru_bx,
           lru_lambda, fox_bf, fox_q_g, fox_k_g, out_g, w_out, norm2_g, w_up, w_down):
    depth = w_in.shape[0]
    bp, seq, d_model = x_prompt.shape
    bs, t_new, _ = x_sample.shape
    n_pool = cache_fox_kv.shape[1]
    past_len = page_table.shape[1] * PAGE
    g = D_GROUP

    seg = jnp.asarray(np.kron(np.eye(N_HEADS), np.ones((HEAD_DIM, HEAD_DIM))), BF16)
    tq = _tile(seq, 256)
    tri_q = jnp.asarray(np.tril(np.ones((tq, tq)), -1), BF16)
    tri_p = jnp.asarray(np.tril(np.ones((PAGE, PAGE)), -1), BF16)
    pt_flat = page_table.reshape(-1).astype(jnp.int32)
    fox_cache = cache_fox_kv.reshape(depth, n_pool, PAGE, 2 * g)
    sb_cache = cache_sb_kv.reshape(depth, n_pool, PAGE, 2 * g)
    lft_cache = jnp.swapaxes(cache_fox_logf, 2, 3)
    t_pad = 8

    xp = x_prompt.reshape(bp * seq, d_model)
    xs = x_sample.reshape(bs * t_new, d_model)
    st_p, st_s = [], []
    for l in range(depth):
        w = _layer_weights(l, norm1_g, w_in, pool_w, pool_scale, conv_w, conv_b, lru_wa, lru_ba, lru_wx,
                           lru_bx, lru_lambda, fox_bf, fox_q_g, fox_k_g, out_g, w_out, norm2_g, w_up, w_down)

        def dense_tail(x, ys, tm):
            x1 = _outproj(x, ys, w["out_g"], w["w_out"], tm)
            return _mlp(x1, w["norm2_g"], w["w_up"], w["w_down"], tm, _tile(w["w_up"].shape[1], 512))

        m = bp * seq
        tm = _tile(m, 512)
        proj, flog = _inproj(xp, w["norm1_g"], w["w_main"], w["w_f"], tm)
        tmp = _tile(seq, 256)
        fkv, skv, lf, c, ct, qf, kf, vf, qs, ks, vs = _prep(proj, flog, w["gq"], w["gk"], w["bf"], seg, seq, tmp)
        r3 = lambda a: a.reshape(bp, seq, a.shape[-1])
        ct3 = jnp.swapaxes(ct.reshape(N_HEADS, bp, seq), 0, 1)
        y_fox = _fox_prompt(r3(qf), r3(kf), r3(vf), r3(c), ct3, tq)
        y_sb = _sb_prompt(r3(qs), r3(ks), r3(vs), tri_q, tq)
        tt = _tile(seq, 256)
        y_pool, y_lru, pnew, cnew, hnew = _seqmix(
            r3(proj), jnp.zeros((bp, HIST_P, g), F32), jnp.zeros((bp, HIST_C, g), F32),
            jnp.zeros((bp, 1, g), F32), w["pool_w"], w["pool_scale"], w["conv_w"], w["conv_b"], w["wax"],
            w["ba"], w["bx"], w["lam"], tt, tt, 0)
        flat = lambda a: a.reshape(m, g)
        xp = dense_tail(xp, (flat(y_pool), flat(y_lru), flat(y_fox), flat(y_sb)), tm)
        st_p.append((fkv.reshape(bp, seq, 2, N_HEADS, HEAD_DIM), lf[:, :N_HEADS].reshape(bp, seq, N_HEADS),
                     skv.reshape(bp, seq, 2, N_HEADS, HEAD_DIM), pnew[:, 1:], cnew[:, HIST_C - CONV_W + 1:],
                     hnew[:, 0]))

        ms = bs * t_new
        proj, flog = _inproj(xs, w["norm1_g"], w["w_main"], w["w_f"], ms)
        fkv, skv, lf, c, ct, qf, kf, vf, qs, ks, vs = _prep(proj, flog, w["gq"], w["gk"], w["bf"], seg, t_new, ms)
        pad_page = lambda a: jnp.pad(a.reshape(bs, t_new, 2 * g), ((0, 0), (0, PAGE - t_new), (0, 0)))
        gt = jnp.pad(jnp.swapaxes(ct.reshape(N_HEADS, bs, t_new), 0, 1), ((0, 0), (0, 0), (0, PAGE - t_new)))
        grow = c[:, :N_HEADS].reshape(bs, t_new * N_HEADS, 1)
        y_fox = _fox_sample(pt_flat, _expand_queries(qf, bs, t_new), pad_page(fkv), fox_cache, l, gt, lft_cache,
                            grow, tri_p, t_new)
        y_sb = _sb_sample(pt_flat, _expand_queries(qs, bs, t_new), pad_page(skv), sb_cache, l, tri_p, t_new)
        proj3 = jnp.pad(proj.reshape(bs, t_new, -1), ((0, 0), (0, t_pad - t_new), (0, 0)))
        y_pool, y_lru, pnew, cnew, hnew = _seqmix(
            proj3, jnp.pad(state_pool[l], ((0, 0), (HIST_P - POOL_MAX + 1, 0), (0, 0))),
            jnp.pad(state_conv[l], ((0, 0), (HIST_C - CONV_W + 1, 0), (0, 0))), state_lru[l][:, None, :],
            w["pool_w"], w["pool_scale"], w["conv_w"], w["conv_b"], w["wax"], w["ba"], w["bx"], w["lam"],
            t_pad, t_new, past_len)
        cut = lambda a: a[:, :t_new].reshape(ms, g)
        xs = dense_tail(xs, (cut(y_pool), cut(y_lru), y_fox.reshape(ms, g), y_sb.reshape(ms, g)), ms)
        st_s.append((fkv.reshape(bs, t_new, 2, N_HEADS, HEAD_DIM), lf[:, :N_HEADS].reshape(bs, t_new, N_HEADS),
                     skv.reshape(bs, t_new, 2, N_HEADS, HEAD_DIM), pnew[:, 1:], cnew[:, HIST_C - CONV_W + 1:],
                     hnew[:, 0]))

    stk = lambda sts, j: jnp.stack([s[j] for s in sts], axis=0)
    return ((xp.reshape(bp, seq, d_model), xs.reshape(bs, t_new, d_model))
            + tuple(stk(st_p, j) for j in range(6)) + tuple(stk(st_s, j) for j in range(6)))
```

```python
import functools

import numpy as np
import jax
import jax.numpy as jnp
from jax import lax
from jax.experimental import pallas as pl
from jax.experimental.pallas import tpu as pltpu

F32 = jnp.float32
BF16 = jnp.bfloat16

D_GROUP = 512
HEAD_DIM = 64
N_HEADS = D_GROUP // HEAD_DIM
POOL_WINDOWS = (2, 4, 8, 16)
POOL_MAX = max(POOL_WINDOWS)
POOL_CH = D_GROUP // len(POOL_WINDOWS)
CONV_W = 4
LRU_C = 8.0
EPS = 1e-6
PAGE = 128
LANES = 128
D_AUG = N_HEADS * LANES
N_SPLIT = 3
HIST_P = 16
HIST_C = 8
NEG_BIG = -1e30
PAGES_PER_STEP = 4
MIB = 1024 * 1024


def _cparams(sem, vmem_mib=48):
    return pltpu.CompilerParams(dimension_semantics=sem, vmem_limit_bytes=vmem_mib * MIB)


def _split_bf16(x, parts):
    out = []
    r = x
    for _ in range(parts - 1):
        h = r.astype(BF16)
        out.append(h)
        r = r - h.astype(F32)
    out.append(r.astype(BF16))
    return out


def _softplus(x):
    return jnp.maximum(x, 0.0) + jnp.log1p(jnp.exp(-jnp.abs(x)))


def _log_sigmoid(x):
    return -_softplus(-x)


def _neg_softplus(z):
    return -(jnp.maximum(z, 0.0) + jnp.log(1.0 + jnp.exp(-jnp.abs(z))))


def _dot_nt(a, b):
    return lax.dot_general(a, b, (((1,), (1,)), ((), ())), preferred_element_type=F32)


def _dot(a, b):
    return jnp.dot(a, b, preferred_element_type=F32)


def _inproj_kernel(x_ref, g_ref, w_ref, wf_ref, o_ref, of_ref, xn_ref):
    @pl.when(pl.program_id(1) == 0)
    def _():
        x = x_ref[...]
        ms = jnp.mean(x * x, axis=-1, keepdims=True)
        xn = (x * lax.rsqrt(ms + EPS) * g_ref[...]).astype(BF16)
        xn_ref[...] = xn
        of_ref[...] = _dot(xn, wf_ref[...])

    o_ref[...] = _dot(xn_ref[...], w_ref[...])


def _inproj(x, g, w, wf, tm):
    m, d = x.shape
    n = w.shape[1]
    tn = D_GROUP
    return pl.pallas_call(
        _inproj_kernel,
        out_shape=(jax.ShapeDtypeStruct((m, n), F32), jax.ShapeDtypeStruct((m, LANES), F32)),
        grid=(m // tm, n // tn),
        in_specs=[
            pl.BlockSpec((tm, d), lambda i, j: (i, 0)),
            pl.BlockSpec((1, d), lambda i, j: (0, 0)),
            pl.BlockSpec((d, tn), lambda i, j: (0, j)),
            pl.BlockSpec((d, LANES), lambda i, j: (0, 0)),
        ],
        out_specs=(
            pl.BlockSpec((tm, tn), lambda i, j: (i, j)),
            pl.BlockSpec((tm, LANES), lambda i, j: (i, 0)),
        ),
        scratch_shapes=[pltpu.VMEM((tm, d), BF16)],
        compiler_params=_cparams(("arbitrary", "arbitrary")),
        name="inproj",
    )(x, g, w, wf)


def _head_norm(x, g, seg_ref):
    hi, lo = _split_bf16(x * x, 2)
    ss = _dot(hi, seg_ref[...]) + _dot(lo, seg_ref[...])
    return x * lax.rsqrt(ss * (1.0 / HEAD_DIM) + EPS) * g


def _running_sum(lf, seq, tm):
    r = lax.broadcasted_iota(jnp.int32, (tm, tm), 0)
    c = lax.broadcasted_iota(jnp.int32, (tm, tm), 1)
    keep = c <= r
    if seq < tm:
        keep = jnp.logical_and(keep, (r // seq) == (c // seq))
    tri = jnp.where(keep, 1.0, 0.0).astype(BF16)
    cs = None
    for part in _split_bf16(lf, N_SPLIT):
        t = _dot(tri, part)
        cs = t if cs is None else cs + t
    return cs


def _prep_prompt_kernel(fq_ref, fk_ref, fv_ref, sq_ref, sk_ref, sv_ref, fl_ref, gq_ref, gk_ref, bf_ref,
                        seg_ref, place_ref, placec_ref, neg_ref,
                        fkvt_ref, skvt_ref, lft_ref, ct_ref,
                        fqt_ref, fka_ref, fvt_ref, sqt_ref, ska_ref, svt_ref, carry_ref, *, tm):
    scale = HEAD_DIM ** -0.5
    fq = _head_norm(fq_ref[...], gq_ref[...], seg_ref)
    fk = _head_norm(fk_ref[...], gk_ref[...], seg_ref)
    fv = fv_ref[...]
    sk = sk_ref[...]
    sv = sv_ref[...]

    fvt = fv.T
    svt = sv.T
    fkvt_ref[0:D_GROUP, :] = fk.T
    fkvt_ref[D_GROUP:2 * D_GROUP, :] = fvt
    skvt_ref[0:D_GROUP, :] = sk.T
    skvt_ref[D_GROUP:2 * D_GROUP, :] = svt
    fvt_ref[...] = fvt.astype(BF16)
    svt_ref[...] = svt.astype(BF16)

    lf = _log_sigmoid(fl_ref[...] + bf_ref[...])
    cs = _running_sum(lf, tm, tm)

    @pl.when(pl.program_id(1) == 0)
    def _():
        carry_ref[...] = jnp.zeros_like(carry_ref)
    cs = cs + carry_ref[...]
    carry_ref[...] = cs[tm - 1:tm, :]
    lft_ref[...] = lf.T[0:N_HEADS, :]
    ct_ref[...] = cs.T[0:N_HEADS, :]

    ka = _dot(fk.astype(BF16), place_ref[...])
    for i, part in enumerate(_split_bf16(cs, N_SPLIT)):
        ka = ka + _dot(part, placec_ref[i])
    fka_ref[...] = ka.astype(BF16)
    qa = _dot((fq * scale).astype(BF16), place_ref[...]) + neg_ref[...]
    fqt_ref[...] = qa.T.astype(BF16)
    ska_ref[...] = _dot(sk.astype(BF16), place_ref[...]).astype(BF16)
    sqt_ref[...] = _dot((sq_ref[...] * scale).astype(BF16), place_ref[...]).T.astype(BF16)


def _prep_prompt(proj3, flog3, gq, gk, bf, seg, place, placec, neg, tm):
    b, t, _ = proj3.shape
    col = lambda s: pl.BlockSpec((None, tm, D_GROUP), lambda bi, ti, s=s: (bi, ti, s))
    const = lambda shp: pl.BlockSpec(shp, lambda bi, ti: (0,) * len(shp))
    tr = lambda rows: pl.BlockSpec((None, rows, tm), lambda bi, ti: (bi, 0, ti))
    nt = lambda cols: pl.BlockSpec((None, tm, cols), lambda bi, ti: (bi, ti, 0))
    sds = jax.ShapeDtypeStruct
    return pl.pallas_call(
        functools.partial(_prep_prompt_kernel, tm=tm),
        out_shape=(
            sds((b, 2 * D_GROUP, t), F32), sds((b, 2 * D_GROUP, t), F32),
            sds((b, N_HEADS, t), F32), sds((b, N_HEADS, t), F32),
            sds((b, D_AUG, t), BF16), sds((b, t, D_AUG), BF16), sds((b, D_GROUP, t), BF16),
            sds((b, D_AUG, t), BF16), sds((b, t, D_AUG), BF16), sds((b, D_GROUP, t), BF16),
        ),
        grid=(b, t // tm),
        in_specs=[col(3), col(4), col(5), col(6), col(7), col(8),
                  pl.BlockSpec((None, tm, LANES), lambda bi, ti: (bi, ti, 0)),
                  const((1, D_GROUP)), const((1, D_GROUP)), const((1, LANES)),
                  const((D_GROUP, D_GROUP)), const((D_GROUP, D_AUG)), const((N_SPLIT, LANES, D_AUG)),
                  const((1, D_AUG))],
        out_specs=(tr(2 * D_GROUP), tr(2 * D_GROUP), tr(N_HEADS), tr(N_HEADS),
                   tr(D_AUG), nt(D_AUG), tr(D_GROUP), tr(D_AUG), nt(D_AUG), tr(D_GROUP)),
        scratch_shapes=[pltpu.VMEM((1, LANES), F32)],
        compiler_params=_cparams(("arbitrary", "arbitrary")),
        name="prep_prompt",
    )(proj3, proj3, proj3, proj3, proj3, proj3, flog3, gq, gk, bf, seg, place, placec, neg)


def _prep_sample_kernel(fq_ref, fk_ref, fv_ref, sq_ref, sk_ref, sv_ref, fl_ref, gq_ref, gk_ref, bf_ref,
                        seg_ref, fkv_ref, skv_ref, lf_ref, c_ref, ct_ref, qf_ref, qs_ref, *, seq, tm):
    scale = HEAD_DIM ** -0.5
    fq = _head_norm(fq_ref[...], gq_ref[...], seg_ref)
    fk = _head_norm(fk_ref[...], gk_ref[...], seg_ref)
    qf_ref[...] = (fq * scale).astype(BF16)
    qs_ref[...] = (sq_ref[...] * scale).astype(BF16)
    fkv_ref[:, 0:D_GROUP] = fk
    fkv_ref[:, D_GROUP:2 * D_GROUP] = fv_ref[...]
    skv_ref[:, 0:D_GROUP] = sk_ref[...]
    skv_ref[:, D_GROUP:2 * D_GROUP] = sv_ref[...]
    lf = _log_sigmoid(fl_ref[...] + bf_ref[...])
    lf_ref[...] = lf
    cs = _running_sum(lf, seq, tm)
    c_ref[...] = cs
    ct_ref[...] = cs.T[0:N_HEADS, :]


def _prep_sample(proj, flog, gq, gk, bf, seg, seq):
    m = proj.shape[0]
    col = lambda s: pl.BlockSpec((m, D_GROUP), lambda i, s=s: (0, s))
    full = lambda shp: pl.BlockSpec(shp, lambda i: (0,) * len(shp))
    sds = jax.ShapeDtypeStruct
    return pl.pallas_call(
        functools.partial(_prep_sample_kernel, seq=seq, tm=m),
        out_shape=(sds((m, 2 * D_GROUP), F32), sds((m, 2 * D_GROUP), F32), sds((m, LANES), F32),
                   sds((m, LANES), F32), sds((N_HEADS, m), F32), sds((m, D_GROUP), BF16),
                   sds((m, D_GROUP), BF16)),
        grid=(1,),
        in_specs=[col(3), col(4), col(5), col(6), col(7), col(8), full((m, LANES)),
                  full((1, D_GROUP)), full((1, D_GROUP)), full((1, LANES)), full((D_GROUP, D_GROUP))],
        out_specs=(full((m, 2 * D_GROUP)), full((m, 2 * D_GROUP)), full((m, LANES)), full((m, LANES)),
                   full((N_HEADS, m)), full((m, D_GROUP)), full((m, D_GROUP))),
        compiler_params=_cparams(("arbitrary",)),
        name="prep_sample",
    )(proj, proj, proj, proj, proj, proj, flog, gq, gk, bf, seg)


def _seqmix_kernel(u_ref, x_ref, gate_ref, pprev_ref, cprev_ref, h0_ref,
                   pw_ref, pscale_ref, cw_ref, cb_ref, wax_ref, ba_ref, bx_ref, lam_ref,
                   yp_ref, yl_ref, pnew_ref, cnew_ref, hnew_ref,
                   extp_ref, extc_ref, h_ref, *, tt, tv, pos0):
    ti = pl.program_id(1)

    @pl.when(ti == 0)
    def _():
        extp_ref[0:HIST_P, :] = pprev_ref[...]
        extc_ref[0:HIST_C, :] = cprev_ref[...]
        h_ref[...] = h0_ref[...]

    extp_ref[HIST_P:HIST_P + tt, :] = u_ref[...]
    extc_ref[HIST_C:HIST_C + tt, :] = x_ref[...]
    pos = pos0 + ti * tt + lax.broadcasted_iota(jnp.int32, (tt, 1), 0)

    for g, w in enumerate(POOL_WINDOWS):
        lanes = slice(g * POOL_CH, (g + 1) * POOL_CH)
        tok = extp_ref[HIST_P:HIST_P + tt, lanes]
        win = tok
        for j in range(1, w):
            win = win + extp_ref[HIST_P - j:HIST_P - j + tt, lanes]
        cnt = jnp.minimum(w, pos + 1).astype(F32)
        d = win / cnt - tok
        y = _dot(d.astype(BF16), pw_ref[g]) * pscale_ref[:, lanes]
        yp_ref[:, lanes] = y
    hist = extp_ref[tv:tv + HIST_P, :]
    extp_ref[0:HIST_P, :] = hist
    pnew_ref[...] = hist

    base = HIST_C - (CONV_W - 1)
    xc = extc_ref[base:base + tt, :] * cw_ref[0:1, :]
    for k in range(1, CONV_W):
        xc = xc + extc_ref[base + k:base + k + tt, :] * cw_ref[k:k + 1, :]
    xc = cb_ref[...] + xc
    chist = extc_ref[tv:tv + HIST_C, :]
    extc_ref[0:HIST_C, :] = chist
    cnew_ref[...] = chist

    ri = _dot(xc.astype(BF16), wax_ref[...])
    r = jax.nn.sigmoid(ri[:, 0:D_GROUP] + ba_ref[...])
    gi = jax.nn.sigmoid(ri[:, D_GROUP:2 * D_GROUP] + bx_ref[...])
    log_a = -LRU_C * r * _softplus(-lam_ref[...])
    a = jnp.exp(log_a)
    mult = jnp.sqrt(-jnp.tanh(log_a) * (a * a + 1.0))
    mult = jnp.where(pos == 0, 1.0, mult)
    b = mult * (gi * xc)

    rows = lax.broadcasted_iota(jnp.int32, (tt, 1), 0)
    d = 1
    while d < tt:
        ok = rows >= d
        a_s = jnp.where(ok, pltpu.roll(a, d, 0), 1.0)
        b_s = jnp.where(ok, pltpu.roll(b, d, 0), 0.0)
        b = a * b_s + b
        a = a * a_s
        d *= 2
    h = b + a * h_ref[...]
    hlast = h[tv - 1:tv, :]
    h_ref[...] = hlast
    hnew_ref[...] = hlast
    yl_ref[...] = jax.nn.gelu(gate_ref[...]) * h


def _seqmix(proj3, pprev, cprev, h0, pw, pscale, cw, cb, wax, ba, bx, lam, tt, tv, pos0):
    b, t, _ = proj3.shape
    nt = t // tt
    col = lambda s: pl.BlockSpec((None, tt, D_GROUP), lambda bi, ti, s=s: (bi, ti, s))
    per_b = lambda r: pl.BlockSpec((None, r, D_GROUP), lambda bi, ti: (bi, 0, 0))
    const2 = lambda shp: pl.BlockSpec(shp, lambda bi, ti: (0, 0))
    return pl.pallas_call(
        functools.partial(_seqmix_kernel, tt=tt, tv=tv, pos0=pos0),
        out_shape=(
            jax.ShapeDtypeStruct((b, t, D_GROUP), F32),
            jax.ShapeDtypeStruct((b, t, D_GROUP), F32),
            jax.ShapeDtypeStruct((b, HIST_P, D_GROUP), F32),
            jax.ShapeDtypeStruct((b, HIST_C, D_GROUP), F32),
            jax.ShapeDtypeStruct((b, 1, D_GROUP), F32),
        ),
        grid=(b, nt),
        in_specs=[col(0), col(1), col(2), per_b(HIST_P), per_b(HIST_C), per_b(1),
                  pl.BlockSpec((len(POOL_WINDOWS), POOL_CH, POOL_CH), lambda bi, ti: (0, 0, 0)),
                  const2((1, D_GROUP)), const2((HIST_C, D_GROUP)), const2((1, D_GROUP)),
                  const2((D_GROUP, 2 * D_GROUP)), const2((1, D_GROUP)), const2((1, D_GROUP)),
                  const2((1, D_GROUP))],
        out_specs=(
            pl.BlockSpec((None, tt, D_GROUP), lambda bi, ti: (bi, ti, 0)),
            pl.BlockSpec((None, tt, D_GROUP), lambda bi, ti: (bi, ti, 0)),
            per_b(HIST_P), per_b(HIST_C), per_b(1),
        ),
        scratch_shapes=[pltpu.VMEM((HIST_P + tt, D_GROUP), F32),
                        pltpu.VMEM((HIST_C + tt, D_GROUP), F32),
                        pltpu.VMEM((1, D_GROUP), F32)],
        compiler_params=_cparams(("arbitrary", "arbitrary")),
        name="seqmix",
    )(proj3, proj3, proj3, pprev, cprev, h0, pw, pscale, cw, cb, wax, ba, bx, lam)


def _pair_tables(n, reverse):
    qi, ki = [], []
    for q in range(n):
        ks = range(q, -1, -1) if reverse else range(q + 1)
        for k in ks:
            qi.append(q)
            ki.append(k)
    return jnp.asarray(np.array(qi, np.int32)), jnp.asarray(np.array(ki, np.int32))


def _fox_prompt_kernel(qi_ref, ki_ref, qt_ref, k_ref, vt_ref, cq_ref, o_ref, m_ref, l_ref, acc_ref, *, tq):
    s_id = pl.program_id(1)
    qi = qi_ref[s_id]
    ki = ki_ref[s_id]

    @pl.when(ki == 0)
    def _():
        m_ref[...] = jnp.full_like(m_ref, NEG_BIG)
        l_ref[...] = jnp.zeros_like(l_ref)
        acc_ref[...] = jnp.zeros_like(acc_ref)

    def step(masked):
        if masked:
            key = lax.broadcasted_iota(jnp.int32, (tq, tq), 0)
            qry = lax.broadcasted_iota(jnp.int32, (tq, tq), 1)
            causal = key <= qry

        def scores(h):
            slot = slice(h * LANES, (h + 1) * LANES)
            return _dot(k_ref[:, slot], qt_ref[slot, :])

        def softmax_update(h, st):
            if masked:
                st = jnp.where(causal, st, NEG_BIG)
            cq = cq_ref[h:h + 1, :]
            m_old = m_ref[h:h + 1, :]
            m_new = jnp.maximum(m_old, jnp.max(st, axis=0, keepdims=True) + cq)
            alpha = jnp.exp(m_old - m_new)
            pt = jnp.exp(st - (m_new - cq))
            l_ref[h:h + 1, :] = alpha * l_ref[h:h + 1, :] + jnp.sum(pt, axis=0, keepdims=True)
            m_ref[h:h + 1, :] = m_new
            return alpha, pt.astype(BF16)

        st = {0: scores(0), 1: scores(1)}
        for h in range(N_HEADS):
            rows = slice(h * HEAD_DIM, (h + 1) * HEAD_DIM)
            if h + 2 < N_HEADS:
                st[h + 2] = scores(h + 2)
            alpha, pt = softmax_update(h, st.pop(h))
            acc_ref[rows, :] = alpha * acc_ref[rows, :] + _dot(vt_ref[rows, :], pt)

    @pl.when(ki < qi)
    def _():
        step(False)

    @pl.when(ki == qi)
    def _():
        step(True)
        for h in range(N_HEADS):
            rows = slice(h * HEAD_DIM, (h + 1) * HEAD_DIM)
            acc_ref[rows, :] = acc_ref[rows, :] / l_ref[h:h + 1, :]
        o_ref[...] = acc_ref[...].T


def _sb_prompt_kernel(qi_ref, ki_ref, qt_ref, k_ref, vt_ref, tri_ref, o_ref, carry_ref, acc_ref, *, tq):
    s_id = pl.program_id(1)
    qi = qi_ref[s_id]
    ki = ki_ref[s_id]

    def step(masked):
        if masked:
            key = lax.broadcasted_iota(jnp.int32, (tq, tq), 0)
            qry = lax.broadcasted_iota(jnp.int32, (tq, tq), 1)
            strict = key < qry

        def logits(h):
            slot = slice(h * LANES, (h + 1) * LANES)
            return _dot(k_ref[:, slot], qt_ref[slot, :])

        def log_one_minus_beta(zt):
            l1m = _neg_softplus(zt)
            if masked:
                l1m = jnp.where(strict, l1m, 0.0)
            return (l1m,) + tuple(_split_bf16(l1m, 2))

        def later_keys_sum(hi, lo):
            return _dot(tri_ref[...], hi) + _dot(tri_ref[...], lo)

        def weights(h, zt, l1m, rest):
            if masked:
                carry = jnp.zeros((1, tq), F32)
            else:
                carry = carry_ref[h:h + 1, :]
            wgt = jnp.exp(zt + l1m + rest + carry)
            if masked:
                wgt = jnp.where(strict, wgt, 0.0)
            carry_ref[h:h + 1, :] = carry + rest[0:1, :] + l1m[0:1, :]
            return wgt.astype(BF16)

        zt = {0: logits(0), 1: logits(1)}
        lg = {0: log_one_minus_beta(zt[0])}
        for h in range(N_HEADS):
            rows = slice(h * HEAD_DIM, (h + 1) * HEAD_DIM)
            l1m, hi, lo = lg.pop(h)
            rest = later_keys_sum(hi, lo)
            if h + 2 < N_HEADS:
                zt[h + 2] = logits(h + 2)
            if h + 1 < N_HEADS:
                lg[h + 1] = log_one_minus_beta(zt[h + 1])
            pv = _dot(vt_ref[rows, :], weights(h, zt.pop(h), l1m, rest))
            if masked:
                acc_ref[rows, :] = pv
            else:
                acc_ref[rows, :] = acc_ref[rows, :] + pv

    @pl.when(ki == qi)
    def _():
        step(True)

    @pl.when(ki < qi)
    def _():
        step(False)

    @pl.when(ki == 0)
    def _():
        o_ref[...] = acc_ref[...].T


def _prompt_attention(kind, qt, k, vt, extra, tq):
    b, t, _ = k.shape
    nq = t // tq
    qi, ki = _pair_tables(nq, reverse=(kind == "sb"))
    in_specs = [pl.BlockSpec((None, D_AUG, tq), lambda bi, s, qi, ki: (bi, 0, qi[s])),
                pl.BlockSpec((None, tq, D_AUG), lambda bi, s, qi, ki: (bi, ki[s], 0)),
                pl.BlockSpec((None, D_GROUP, tq), lambda bi, s, qi, ki: (bi, 0, ki[s]))]
    if kind == "fox":
        in_specs.append(pl.BlockSpec((None, N_HEADS, tq), lambda bi, s, qi, ki: (bi, 0, qi[s])))
        body = _fox_prompt_kernel
        scratch = [pltpu.VMEM((N_HEADS, tq), F32), pltpu.VMEM((N_HEADS, tq), F32),
                   pltpu.VMEM((D_GROUP, tq), F32)]
    else:
        in_specs.append(pl.BlockSpec((tq, tq), lambda bi, s, qi, ki: (0, 0)))
        body = _sb_prompt_kernel
        scratch = [pltpu.VMEM((N_HEADS, tq), F32), pltpu.VMEM((D_GROUP, tq), F32)]
    grid_spec = pltpu.PrefetchScalarGridSpec(
        num_scalar_prefetch=2,
        grid=(b, int(qi.shape[0])),
        in_specs=in_specs,
        out_specs=pl.BlockSpec((None, tq, D_GROUP), lambda bi, s, qi, ki: (bi, qi[s], 0)),
        scratch_shapes=scratch,
    )
    return pl.pallas_call(
        functools.partial(body, tq=tq),
        out_shape=jax.ShapeDtypeStruct((b, t, D_GROUP), F32),
        grid_spec=grid_spec,
        compiler_params=_cparams(("arbitrary", "arbitrary")),
        name=kind + "_prompt",
    )(qi, ki, qt, k, vt, extra)


def _extract_heads(acc, t_new):
    hrow = lax.broadcasted_iota(jnp.int32, (N_HEADS, D_GROUP), 0)
    hcol = lax.broadcasted_iota(jnp.int32, (N_HEADS, D_GROUP), 1) // HEAD_DIM
    own = hrow == hcol
    outs = []
    for t in range(t_new):
        blk = acc[t * N_HEADS:(t + 1) * N_HEADS, :]
        outs.append(jnp.sum(jnp.where(own, blk, 0.0), axis=0, keepdims=True))
    return outs


def _fox_sample_kernel(pt_ref, wq_ref, new_ref, gt_ref, grow_ref, tri_ref, *rest, t_new, n_steps):
    g = PAGES_PER_STEP
    cache_refs = rest[0:g]
    lft_refs = rest[g:2 * g]
    o_ref, m_ref, l_ref, acc_ref, dcar_ref = rest[2 * g:]
    p = pl.program_id(1)
    rows = t_new * N_HEADS

    def process(kv_refs, biases, mask):
        s = []
        for kv_ref, bias in zip(kv_refs, biases):
            kt = kv_ref[0:D_GROUP, :].astype(BF16)
            s.append(_dot(wq_ref[...], kt) + jnp.concatenate([bias] * t_new, axis=0) + grow_ref[...])
        s = s[0] if len(s) == 1 else jnp.concatenate(s, axis=1)
        if mask is not None:
            s = jnp.where(mask, s, NEG_BIG)
        m_old = m_ref[...]
        m_new = jnp.maximum(m_old, jnp.max(s, axis=-1, keepdims=True))
        alpha = jnp.exp(m_old - m_new)
        pr = jnp.exp(s - m_new)
        l_ref[...] = alpha * l_ref[...] + jnp.sum(pr, axis=-1, keepdims=True)
        pv = None
        for i, kv_ref in enumerate(kv_refs):
            vt = kv_ref[D_GROUP:2 * D_GROUP, :].astype(BF16)
            t = _dot_nt(pr[:, i * PAGE:(i + 1) * PAGE].astype(BF16), vt)
            pv = t if pv is None else pv + t
        acc_ref[...] = alpha * acc_ref[...] + pv
        m_ref[...] = m_new

    @pl.when(p == 0)
    def _():
        m_ref[...] = jnp.full_like(m_ref, NEG_BIG)
        l_ref[...] = jnp.zeros_like(l_ref)
        acc_ref[...] = jnp.zeros_like(acc_ref)
        dcar_ref[...] = jnp.zeros_like(dcar_ref)
        tok = lax.broadcasted_iota(jnp.int32, (rows, PAGE), 0) // N_HEADS
        key = lax.broadcasted_iota(jnp.int32, (rows, PAGE), 1)
        process([new_ref], [-gt_ref[...]], key <= tok)

    @pl.when(p > 0)
    def _():
        lfs = [r[...] for r in lft_refs]
        parts = []
        for lf in lfs:
            parts += [x.astype(F32) for x in _split_bf16(lf, N_SPLIT)]
        r = _dot(jnp.concatenate(parts, axis=0).astype(BF16), tri_ref[...])
        run = dcar_ref[...]
        biases = []
        for i, lf in enumerate(lfs):
            base = i * N_SPLIT * N_HEADS
            later = r[base:base + N_HEADS]
            for j in range(1, N_SPLIT):
                later = later + r[base + j * N_HEADS:base + (j + 1) * N_HEADS]
            biases.append(later + run)
            run = run + later[:, 0:1] + lf[:, 0:1]
        dcar_ref[...] = run
        process(cache_refs, biases, None)

    @pl.when(p == n_steps)
    def _():
        out = acc_ref[...] / l_ref[...]
        for t, rowv in enumerate(_extract_heads(out, t_new)):
            o_ref[t:t + 1, :] = rowv


def _sb_sample_kernel(pt_ref, wq_ref, new_ref, tri_ref, *rest, t_new, n_steps):
    g = PAGES_PER_STEP
    cache_refs = rest[0:g]
    o_ref, acc_ref, car_ref = rest[g:]
    p = pl.program_id(1)
    rows = t_new * N_HEADS

    def process(kv_refs, mask):
        zs, l1ms, parts = [], [], []
        for kv_ref in kv_refs:
            kt = kv_ref[0:D_GROUP, :].astype(BF16)
            z = _dot(wq_ref[...], kt)
            l1m = _neg_softplus(z)
            if mask is not None:
                l1m = jnp.where(mask, l1m, 0.0)
            zs.append(z)
            l1ms.append(l1m)
            parts += [x.astype(F32) for x in _split_bf16(l1m, 2)]
        r = _dot(jnp.concatenate(parts, axis=0).astype(BF16), tri_ref[...])
        run = car_ref[...]
        pv = None
        for i, kv_ref in enumerate(kv_refs):
            rest_i = r[2 * i * rows:(2 * i + 1) * rows] + r[(2 * i + 1) * rows:(2 * i + 2) * rows]
            wgt = jnp.exp(zs[i] + l1ms[i] + rest_i + run)
            if mask is not None:
                wgt = jnp.where(mask, wgt, 0.0)
            run = run + rest_i[:, 0:1] + l1ms[i][:, 0:1]
            vt = kv_ref[D_GROUP:2 * D_GROUP, :].astype(BF16)
            t = _dot_nt(wgt.astype(BF16), vt)
            pv = t if pv is None else pv + t
        car_ref[...] = run
        acc_ref[...] = acc_ref[...] + pv

    @pl.when(p == 0)
    def _():
        acc_ref[...] = jnp.zeros_like(acc_ref)
        car_ref[...] = jnp.zeros_like(car_ref)
        tok = lax.broadcasted_iota(jnp.int32, (rows, PAGE), 0) // N_HEADS
        key = lax.broadcasted_iota(jnp.int32, (rows, PAGE), 1)
        process([new_ref], key < tok)

    @pl.when(p > 0)
    def _():
        process(cache_refs, None)

    @pl.when(p == n_steps)
    def _():
        for t, rowv in enumerate(_extract_heads(acc_ref[...], t_new)):
            o_ref[t:t + 1, :] = rowv


def _sample_attention(kind, pt, wq, new_kvt, cache, layer, tri, t_new, fox_extra=None):
    b = wq.shape[0]
    n_pages = pt.shape[0] // b
    g = PAGES_PER_STEP
    n_steps = n_pages // g
    rows = t_new * N_HEADS

    def page_spec(block, i):
        def idx(bi, p, pt):
            newest_first = (jnp.maximum(p, 1) - 1) * g + i
            return (layer, pt[bi * n_pages + (n_pages - 1 - newest_first)], 0, 0)
        return pl.BlockSpec((None, None) + block, idx)

    per_b = lambda r, c: pl.BlockSpec((None, r, c), lambda bi, p, pt: (bi, 0, 0))
    tri_spec = pl.BlockSpec((PAGE, PAGE), lambda bi, p, pt: (0, 0))
    cache_specs = [page_spec((2 * D_GROUP, PAGE), i) for i in range(g)]
    if kind == "fox":
        gt, lft_cache, grow = fox_extra
        in_specs = ([per_b(rows, D_GROUP), per_b(2 * D_GROUP, PAGE), per_b(N_HEADS, PAGE), per_b(rows, 1),
                     tri_spec] + cache_specs + [page_spec((N_HEADS, PAGE), i) for i in range(g)])
        args = (pt, wq, new_kvt, gt, grow, tri) + (cache,) * g + (lft_cache,) * g
        body = _fox_sample_kernel
        scratch = [pltpu.VMEM((rows, 1), F32), pltpu.VMEM((rows, 1), F32),
                   pltpu.VMEM((rows, D_GROUP), F32), pltpu.VMEM((N_HEADS, 1), F32)]
    else:
        in_specs = [per_b(rows, D_GROUP), per_b(2 * D_GROUP, PAGE), tri_spec] + cache_specs
        args = (pt, wq, new_kvt, tri) + (cache,) * g
        body = _sb_sample_kernel
        scratch = [pltpu.VMEM((rows, D_GROUP), F32), pltpu.VMEM((rows, 1), F32)]
    grid_spec = pltpu.PrefetchScalarGridSpec(
        num_scalar_prefetch=1,
        grid=(b, n_steps + 1),
        in_specs=in_specs,
        out_specs=pl.BlockSpec((None, t_new, D_GROUP), lambda bi, p, pt: (bi, 0, 0)),
        scratch_shapes=scratch,
    )
    return pl.pallas_call(
        functools.partial(body, t_new=t_new, n_steps=n_steps),
        out_shape=jax.ShapeDtypeStruct((b, t_new, D_GROUP), F32),
        grid_spec=grid_spec,
        compiler_params=_cparams(("arbitrary", "arbitrary")),
        name=kind + "_sample",
    )(*args)


def _outproj_kernel(x_ref, y0_ref, y1_ref, y2_ref, y3_ref, g_ref, w_ref, o_ref):
    acc = x_ref[...]
    for gi, y_ref in enumerate((y0_ref, y1_ref, y2_ref, y3_ref)):
        y = y_ref[...]
        ms = jnp.mean(y * y, axis=-1, keepdims=True)
        yn = (y * lax.rsqrt(ms + EPS) * g_ref[gi:gi + 1, :]).astype(BF16)
        acc = acc + _dot(yn, w_ref[gi * D_GROUP:(gi + 1) * D_GROUP, :])
    o_ref[...] = acc


def _outproj(x, ys, g, w, tm):
    m, d = x.shape
    yspec = pl.BlockSpec((tm, D_GROUP), lambda i: (i, 0))
    return pl.pallas_call(
        _outproj_kernel,
        out_shape=jax.ShapeDtypeStruct((m, d), F32),
        grid=(m // tm,),
        in_specs=[pl.BlockSpec((tm, d), lambda i: (i, 0)), yspec, yspec, yspec, yspec,
                  pl.BlockSpec((4, D_GROUP), lambda i: (0, 0)),
                  pl.BlockSpec((4 * D_GROUP, d), lambda i: (0, 0))],
        out_specs=pl.BlockSpec((tm, d), lambda i: (i, 0)),
        compiler_params=_cparams(("arbitrary",)),
        name="outproj",
    )(x, *ys, g, w)


def _mlp_kernel(x_ref, g_ref, wu_ref, wd_ref, o_ref, xn_ref):
    @pl.when(pl.program_id(1) == 0)
    def _():
        x = x_ref[...]
        ms = jnp.mean(x * x, axis=-1, keepdims=True)
        xn_ref[...] = (x * lax.rsqrt(ms + EPS) * g_ref[...]).astype(BF16)
        o_ref[...] = x

    hid = jnp.maximum(_dot(xn_ref[...], wu_ref[...]), 0.0)
    o_ref[...] += _dot((hid * hid).astype(BF16), wd_ref[...])


def _mlp(x, g, wu, wd, tm, tf):
    m, d = x.shape
    f = wu.shape[1]
    return pl.pallas_call(
        _mlp_kernel,
        out_shape=jax.ShapeDtypeStruct((m, d), F32),
        grid=(m // tm, f // tf),
        in_specs=[pl.BlockSpec((tm, d), lambda i, j: (i, 0)),
                  pl.BlockSpec((1, d), lambda i, j: (0, 0)),
                  pl.BlockSpec((d, tf), lambda i, j: (0, j)),
                  pl.BlockSpec((tf, d), lambda i, j: (j, 0))],
        out_specs=pl.BlockSpec((tm, d), lambda i, j: (i, 0)),
        scratch_shapes=[pltpu.VMEM((tm, d), BF16)],
        compiler_params=_cparams(("arbitrary", "arbitrary")),
        name="mlp",
    )(x, g, wu, wd)


def _block_diag(w):
    n, d, _ = w.shape
    eye = jnp.eye(n, dtype=w.dtype)
    return (eye[:, None, :, None] * w[:, :, None, :]).reshape(n * d, n * d)


def _layer_weights(l, norm1_g, w_in, pool_w, pool_scale, conv_w, conv_b, lru_wa, lru_ba, lru_wx, lru_bx,
                   lru_lambda, fox_bf, fox_q_g, fox_k_g, out_g, w_out, norm2_g, w_up, w_down):
    g = D_GROUP
    wi = w_in[l]
    nf = 6 * g
    w_main = jnp.concatenate([wi[:, :nf], wi[:, nf + N_HEADS:]], axis=1).astype(BF16)
    w_f = jnp.pad(wi[:, nf:nf + N_HEADS], ((0, 0), (0, LANES - N_HEADS))).astype(BF16)
    return dict(
        norm1_g=norm1_g[l][None, :], w_main=w_main, w_f=w_f,
        pool_w=pool_w[l].astype(BF16), pool_scale=pool_scale[l][None, :],
        conv_w=jnp.pad(conv_w[l], ((0, HIST_C - CONV_W), (0, 0))), conv_b=conv_b[l][None, :],
        wax=jnp.concatenate([_block_diag(lru_wa[l]), _block_diag(lru_wx[l])], axis=1).astype(BF16),
        ba=lru_ba[l][None, :], bx=lru_bx[l][None, :], lam=lru_lambda[l][None, :],
        bf=jnp.pad(fox_bf[l], (0, LANES - N_HEADS))[None, :],
        gq=jnp.tile(fox_q_g[l], N_HEADS)[None, :], gk=jnp.tile(fox_k_g[l], N_HEADS)[None, :],
        out_g=out_g[l].reshape(4, g), w_out=w_out[l].astype(BF16),
        norm2_g=norm2_g[l][None, :], w_up=w_up[l].astype(BF16), w_down=w_down[l].astype(BF16),
    )


def _tile(m, pref):
    t = min(m, pref)
    while m % t:
        t //= 2
    return t


def _slot_constants():
    place = np.zeros((D_GROUP, D_AUG), np.float32)
    for c in range(D_GROUP):
        place[c, (c // HEAD_DIM) * LANES + c % HEAD_DIM] = 1.0
    placec = np.zeros((N_SPLIT, LANES, D_AUG), np.float32)
    neg = np.zeros((1, D_AUG), np.float32)
    for i in range(N_SPLIT):
        for h in range(N_HEADS):
            placec[i, h, h * LANES + HEAD_DIM + i] = 1.0
            neg[0, h * LANES + HEAD_DIM + i] = -1.0
    return jnp.asarray(place, BF16), jnp.asarray(placec, BF16), jnp.asarray(neg, F32)


def _expand_queries(q, b, t_new):
    q4 = q.reshape(b, t_new, 1, D_GROUP)
    own = (jnp.arange(D_GROUP)[None, :] // HEAD_DIM) == jnp.arange(N_HEADS)[:, None]
    return jnp.where(own[None, None], q4, jnp.zeros_like(q4)).reshape(b, t_new * N_HEADS, D_GROUP)


def _pages_view(cache):
    d, n = cache.shape[0], cache.shape[1]
    return jnp.transpose(cache, (0, 1, 3, 4, 5, 2)).reshape(d, n, 2 * D_GROUP, PAGE)


def kernel(x_prompt, x_sample, cache_fox_kv, cache_fox_logf, cache_sb_kv, state_pool, state_conv, state_lru,
           page_table, norm1_g, w_in, pool_w, pool_scale, conv_w, conv_b, lru_wa, lru_ba, lru_wx, lru_bx,
           lru_lambda, fox_bf, fox_q_g, fox_k_g, out_g, w_out, norm2_g, w_up, w_down):
    depth = w_in.shape[0]
    bp, seq, d_model = x_prompt.shape
    bs, t_new, _ = x_sample.shape
    past_len = page_table.shape[1] * PAGE
    g = D_GROUP

    seg = jnp.asarray(np.kron(np.eye(N_HEADS), np.ones((HEAD_DIM, HEAD_DIM))), BF16)
    place, placec, neg = _slot_constants()
    tq = _tile(seq, 256)
    tri_q = jnp.asarray(np.triu(np.ones((tq, tq)), 1), BF16)
    tri_p = jnp.asarray(np.tril(np.ones((PAGE, PAGE)), -1), BF16)
    pt_flat = page_table.reshape(-1).astype(jnp.int32)
    fox_cache = _pages_view(cache_fox_kv)
    sb_cache = _pages_view(cache_sb_kv)
    lft_cache = jnp.swapaxes(cache_fox_logf, 2, 3)
    t_pad = 8

    xp = x_prompt.reshape(bp * seq, d_model)
    xs = x_sample.reshape(bs * t_new, d_model)
    st_p, st_s = [], []
    for l in range(depth):
        w = _layer_weights(l, norm1_g, w_in, pool_w, pool_scale, conv_w, conv_b, lru_wa, lru_ba, lru_wx,
                           lru_bx, lru_lambda, fox_bf, fox_q_g, fox_k_g, out_g, w_out, norm2_g, w_up, w_down)

        def dense_tail(x, ys, tm):
            x1 = _outproj(x, ys, w["out_g"], w["w_out"], tm)
            return _mlp(x1, w["norm2_g"], w["w_up"], w["w_down"], tm, _tile(w["w_up"].shape[1], 512))

        m = bp * seq
        tm = _tile(m, 512)
        proj, flog = _inproj(xp, w["norm1_g"], w["w_main"], w["w_f"], tm)
        r3 = lambda a: a.reshape(bp, seq, a.shape[-1])
        fkvt, skvt, lft, ct, fqt, fka, fvt, sqt, ska, svt = _prep_prompt(
            r3(proj), r3(flog), w["gq"], w["gk"], w["bf"], seg, place, placec, neg, _tile(seq, 256))
        y_fox = _prompt_attention("fox", fqt, fka, fvt, ct, tq)
        y_sb = _prompt_attention("sb", sqt, ska, svt, tri_q, tq)
        tt = _tile(seq, 256)
        y_pool, y_lru, pnew, cnew, hnew = _seqmix(
            r3(proj), jnp.zeros((bp, HIST_P, g), F32), jnp.zeros((bp, HIST_C, g), F32),
            jnp.zeros((bp, 1, g), F32), w["pool_w"], w["pool_scale"], w["conv_w"], w["conv_b"], w["wax"],
            w["ba"], w["bx"], w["lam"], tt, tt, 0)
        flat = lambda a: a.reshape(m, g)
        xp = dense_tail(xp, (flat(y_pool), flat(y_lru), flat(y_fox), flat(y_sb)), tm)
        kv_state = lambda a: jnp.transpose(a.reshape(bp, 2, N_HEADS, HEAD_DIM, seq), (0, 4, 1, 2, 3))
        st_p.append((kv_state(fkvt), jnp.swapaxes(lft, 1, 2), kv_state(skvt), pnew[:, 1:],
                     cnew[:, HIST_C - CONV_W + 1:], hnew[:, 0]))

        ms = bs * t_new
        proj, flog = _inproj(xs, w["norm1_g"], w["w_main"], w["w_f"], ms)
        fkv, skv, lf, c, ct, qf, qs = _prep_sample(proj, flog, w["gq"], w["gk"], w["bf"], seg, t_new)
        new_page = lambda a: jnp.pad(jnp.swapaxes(a.reshape(bs, t_new, 2 * g), 1, 2),
                                     ((0, 0), (0, 0), (0, PAGE - t_new)))
        gt = jnp.pad(jnp.swapaxes(ct.reshape(N_HEADS, bs, t_new), 0, 1), ((0, 0), (0, 0), (0, PAGE - t_new)))
        grow = c[:, :N_HEADS].reshape(bs, t_new * N_HEADS, 1)
        y_fox = _sample_attention("fox", pt_flat, _expand_queries(qf, bs, t_new), new_page(fkv), fox_cache, l,
                                  tri_p, t_new, (gt, lft_cache, grow))
        y_sb = _sample_attention("sb", pt_flat, _expand_queries(qs, bs, t_new), new_page(skv), sb_cache, l,
                                 tri_p, t_new)
        proj3 = jnp.pad(proj.reshape(bs, t_new, -1), ((0, 0), (0, t_pad - t_new), (0, 0)))
        y_pool, y_lru, pnew, cnew, hnew = _seqmix(
            proj3, jnp.pad(state_pool[l], ((0, 0), (HIST_P - POOL_MAX + 1, 0), (0, 0))),
            jnp.pad(state_conv[l], ((0, 0), (HIST_C - CONV_W + 1, 0), (0, 0))), state_lru[l][:, None, :],
            w["pool_w"], w["pool_scale"], w["conv_w"], w["conv_b"], w["wax"], w["ba"], w["bx"], w["lam"],
            t_pad, t_new, past_len)
        cut = lambda a: a[:, :t_new].reshape(ms, g)
        xs = dense_tail(xs, (cut(y_pool), cut(y_lru), y_fox.reshape(ms, g), y_sb.reshape(ms, g)), ms)
        st_s.append((fkv.reshape(bs, t_new, 2, N_HEADS, HEAD_DIM), lf[:, :N_HEADS].reshape(bs, t_new, N_HEADS),
                     skv.reshape(bs, t_new, 2, N_HEADS, HEAD_DIM), pnew[:, 1:], cnew[:, HIST_C - CONV_W + 1:],
                     hnew[:, 0]))

    stk = lambda sts, j: jnp.stack([s[j] for s in sts], axis=0)
    return ((xp.reshape(bp, seq, d_model), xs.reshape(bs, t_new, d_model))
            + tuple(stk(st_p, j) for j in range(6)) + tuple(stk(st_s, j) for j in range(6)))
```

```python
import functools

import numpy as np
import jax
import jax.numpy as jnp
from jax import lax
from jax.experimental import pallas as pl
from jax.experimental.pallas import tpu as pltpu

F32 = jnp.float32
BF16 = jnp.bfloat16

D_GROUP = 512
HEAD_DIM = 64
N_HEADS = D_GROUP // HEAD_DIM
POOL_WINDOWS = (2, 4, 8, 16)
POOL_MAX = max(POOL_WINDOWS)
POOL_CH = D_GROUP // len(POOL_WINDOWS)
CONV_W = 4
LRU_C = 8.0
EPS = 1e-6
PAGE = 128
LANES = 128
D_AUG = N_HEADS * LANES
N_SPLIT = 3
HIST_P = 16
HIST_C = 8
NEG_BIG = -1e30
PAGES_PER_STEP = 8
MIB = 1024 * 1024


def _cparams(sem, vmem_mib=48):
    return pltpu.CompilerParams(dimension_semantics=sem, vmem_limit_bytes=vmem_mib * MIB)


def _split_bf16(x, parts):
    out = []
    r = x
    for _ in range(parts - 1):
        h = r.astype(BF16)
        out.append(h)
        r = r - h.astype(F32)
    out.append(r.astype(BF16))
    return out


def _softplus(x):
    return jnp.maximum(x, 0.0) + jnp.log1p(jnp.exp(-jnp.abs(x)))


def _log_sigmoid(x):
    return -_softplus(-x)


def _neg_softplus(z):
    return -(jnp.maximum(z, 0.0) + jnp.log(1.0 + jnp.exp(-jnp.abs(z))))


def _dot_nt(a, b):
    return lax.dot_general(a, b, (((1,), (1,)), ((), ())), preferred_element_type=F32)


def _dot(a, b):
    return jnp.dot(a, b, preferred_element_type=F32)


def _inproj_kernel(x_ref, g_ref, w_ref, wf_ref, o_ref, of_ref, xn_ref):
    @pl.when(pl.program_id(1) == 0)
    def _():
        x = x_ref[...]
        ms = jnp.mean(x * x, axis=-1, keepdims=True)
        xn = (x * lax.rsqrt(ms + EPS) * g_ref[...]).astype(BF16)
        xn_ref[...] = xn
        of_ref[...] = _dot(xn, wf_ref[...])

    o_ref[...] = _dot(xn_ref[...], w_ref[...])


def _inproj(x, g, w, wf, tm):
    m, d = x.shape
    n = w.shape[1]
    tn = D_GROUP
    return pl.pallas_call(
        _inproj_kernel,
        out_shape=(jax.ShapeDtypeStruct((m, n), F32), jax.ShapeDtypeStruct((m, LANES), F32)),
        grid=(m // tm, n // tn),
        in_specs=[
            pl.BlockSpec((tm, d), lambda i, j: (i, 0)),
            pl.BlockSpec((1, d), lambda i, j: (0, 0)),
            pl.BlockSpec((d, tn), lambda i, j: (0, j)),
            pl.BlockSpec((d, LANES), lambda i, j: (0, 0)),
        ],
        out_specs=(
            pl.BlockSpec((tm, tn), lambda i, j: (i, j)),
            pl.BlockSpec((tm, LANES), lambda i, j: (i, 0)),
        ),
        scratch_shapes=[pltpu.VMEM((tm, d), BF16)],
        compiler_params=_cparams(("arbitrary", "arbitrary")),
        name="inproj",
    )(x, g, w, wf)


def _head_norm(x, g, seg_ref):
    hi, lo = _split_bf16(x * x, 2)
    ss = _dot(hi, seg_ref[...]) + _dot(lo, seg_ref[...])
    return x * lax.rsqrt(ss * (1.0 / HEAD_DIM) + EPS) * g


def _running_sum(lf, seq, tm):
    r = lax.broadcasted_iota(jnp.int32, (tm, tm), 0)
    c = lax.broadcasted_iota(jnp.int32, (tm, tm), 1)
    keep = c <= r
    if seq < tm:
        keep = jnp.logical_and(keep, (r // seq) == (c // seq))
    tri = jnp.where(keep, 1.0, 0.0).astype(BF16)
    cs = None
    for part in _split_bf16(lf, N_SPLIT):
        t = _dot(tri, part)
        cs = t if cs is None else cs + t
    return cs


def _prep_prompt_kernel(fq_ref, fk_ref, fv_ref, sq_ref, sk_ref, sv_ref, fl_ref, gq_ref, gk_ref, bf_ref,
                        seg_ref, place_ref, placec_ref, neg_ref,
                        fkvt_ref, skvt_ref, lft_ref, ct_ref,
                        fqt_ref, fka_ref, fvt_ref, sqt_ref, ska_ref, svt_ref, carry_ref, *, tm):
    scale = HEAD_DIM ** -0.5
    fq = _head_norm(fq_ref[...], gq_ref[...], seg_ref)
    fk = _head_norm(fk_ref[...], gk_ref[...], seg_ref)
    fv = fv_ref[...]
    sk = sk_ref[...]
    sv = sv_ref[...]

    fvt = fv.T
    svt = sv.T
    fkvt_ref[0:D_GROUP, :] = fk.T
    fkvt_ref[D_GROUP:2 * D_GROUP, :] = fvt
    skvt_ref[0:D_GROUP, :] = sk.T
    skvt_ref[D_GROUP:2 * D_GROUP, :] = svt
    fvt_ref[...] = fvt.astype(BF16)
    svt_ref[...] = svt.astype(BF16)

    lf = _log_sigmoid(fl_ref[...] + bf_ref[...])
    cs = _running_sum(lf, tm, tm)

    @pl.when(pl.program_id(1) == 0)
    def _():
        carry_ref[...] = jnp.zeros_like(carry_ref)
    cs = cs + carry_ref[...]
    carry_ref[...] = cs[tm - 1:tm, :]
    lft_ref[...] = lf.T[0:N_HEADS, :]
    ct_ref[...] = cs.T[0:N_HEADS, :]

    ka = _dot(fk.astype(BF16), place_ref[...])
    for i, part in enumerate(_split_bf16(cs, N_SPLIT)):
        ka = ka + _dot(part, placec_ref[i])
    fka_ref[...] = ka.astype(BF16)
    qa = _dot((fq * scale).astype(BF16), place_ref[...]) + neg_ref[...]
    fqt_ref[...] = qa.T.astype(BF16)
    ska_ref[...] = _dot(sk.astype(BF16), place_ref[...]).astype(BF16)
    sqt_ref[...] = _dot((sq_ref[...] * scale).astype(BF16), place_ref[...]).T.astype(BF16)


def _prep_prompt(proj3, flog3, gq, gk, bf, seg, place, placec, neg, tm):
    b, t, _ = proj3.shape
    col = lambda s: pl.BlockSpec((None, tm, D_GROUP), lambda bi, ti, s=s: (bi, ti, s))
    const = lambda shp: pl.BlockSpec(shp, lambda bi, ti: (0,) * len(shp))
    tr = lambda rows: pl.BlockSpec((None, rows, tm), lambda bi, ti: (bi, 0, ti))
    nt = lambda cols: pl.BlockSpec((None, tm, cols), lambda bi, ti: (bi, ti, 0))
    sds = jax.ShapeDtypeStruct
    return pl.pallas_call(
        functools.partial(_prep_prompt_kernel, tm=tm),
        out_shape=(
            sds((b, 2 * D_GROUP, t), F32), sds((b, 2 * D_GROUP, t), F32),
            sds((b, N_HEADS, t), F32), sds((b, N_HEADS, t), F32),
            sds((b, D_AUG, t), BF16), sds((b, t, D_AUG), BF16), sds((b, D_GROUP, t), BF16),
            sds((b, D_AUG, t), BF16), sds((b, t, D_AUG), BF16), sds((b, D_GROUP, t), BF16),
        ),
        grid=(b, t // tm),
        in_specs=[col(3), col(4), col(5), col(6), col(7), col(8),
                  pl.BlockSpec((None, tm, LANES), lambda bi, ti: (bi, ti, 0)),
                  const((1, D_GROUP)), const((1, D_GROUP)), const((1, LANES)),
                  const((D_GROUP, D_GROUP)), const((D_GROUP, D_AUG)), const((N_SPLIT, LANES, D_AUG)),
                  const((1, D_AUG))],
        out_specs=(tr(2 * D_GROUP), tr(2 * D_GROUP), tr(N_HEADS), tr(N_HEADS),
                   tr(D_AUG), nt(D_AUG), tr(D_GROUP), tr(D_AUG), nt(D_AUG), tr(D_GROUP)),
        scratch_shapes=[pltpu.VMEM((1, LANES), F32)],
        compiler_params=_cparams(("arbitrary", "arbitrary")),
        name="prep_prompt",
    )(proj3, proj3, proj3, proj3, proj3, proj3, flog3, gq, gk, bf, seg, place, placec, neg)


def _prep_sample_kernel(fq_ref, fk_ref, fv_ref, sq_ref, sk_ref, sv_ref, fl_ref, gq_ref, gk_ref, bf_ref,
                        seg_ref, fkv_ref, skv_ref, lf_ref, c_ref, ct_ref, qf_ref, qs_ref, *, seq, tm):
    scale = HEAD_DIM ** -0.5
    fq = _head_norm(fq_ref[...], gq_ref[...], seg_ref)
    fk = _head_norm(fk_ref[...], gk_ref[...], seg_ref)
    qf_ref[...] = (fq * scale).astype(BF16)
    qs_ref[...] = (sq_ref[...] * scale).astype(BF16)
    fkv_ref[:, 0:D_GROUP] = fk
    fkv_ref[:, D_GROUP:2 * D_GROUP] = fv_ref[...]
    skv_ref[:, 0:D_GROUP] = sk_ref[...]
    skv_ref[:, D_GROUP:2 * D_GROUP] = sv_ref[...]
    lf = _log_sigmoid(fl_ref[...] + bf_ref[...])
    lf_ref[...] = lf
    cs = _running_sum(lf, seq, tm)
    c_ref[...] = cs
    ct_ref[...] = cs.T[0:N_HEADS, :]


def _prep_sample(proj, flog, gq, gk, bf, seg, seq):
    m = proj.shape[0]
    col = lambda s: pl.BlockSpec((m, D_GROUP), lambda i, s=s: (0, s))
    full = lambda shp: pl.BlockSpec(shp, lambda i: (0,) * len(shp))
    sds = jax.ShapeDtypeStruct
    return pl.pallas_call(
        functools.partial(_prep_sample_kernel, seq=seq, tm=m),
        out_shape=(sds((m, 2 * D_GROUP), F32), sds((m, 2 * D_GROUP), F32), sds((m, LANES), F32),
                   sds((m, LANES), F32), sds((N_HEADS, m), F32), sds((m, D_GROUP), BF16),
                   sds((m, D_GROUP), BF16)),
        grid=(1,),
        in_specs=[col(3), col(4), col(5), col(6), col(7), col(8), full((m, LANES)),
                  full((1, D_GROUP)), full((1, D_GROUP)), full((1, LANES)), full((D_GROUP, D_GROUP))],
        out_specs=(full((m, 2 * D_GROUP)), full((m, 2 * D_GROUP)), full((m, LANES)), full((m, LANES)),
                   full((N_HEADS, m)), full((m, D_GROUP)), full((m, D_GROUP))),
        compiler_params=_cparams(("arbitrary",)),
        name="prep_sample",
    )(proj, proj, proj, proj, proj, proj, flog, gq, gk, bf, seg)


def _seqmix_kernel(u_ref, x_ref, gate_ref, pprev_ref, cprev_ref, h0_ref,
                   pw_ref, pscale_ref, cw_ref, cb_ref, wax_ref, ba_ref, bx_ref, lam_ref,
                   yp_ref, yl_ref, pnew_ref, cnew_ref, hnew_ref,
                   extp_ref, extc_ref, h_ref, *, tt, tv, pos0):
    ti = pl.program_id(1)

    @pl.when(ti == 0)
    def _():
        extp_ref[0:HIST_P, :] = pprev_ref[...]
        extc_ref[0:HIST_C, :] = cprev_ref[...]
        h_ref[...] = h0_ref[...]

    extp_ref[HIST_P:HIST_P + tt, :] = u_ref[...]
    extc_ref[HIST_C:HIST_C + tt, :] = x_ref[...]
    pos = pos0 + ti * tt + lax.broadcasted_iota(jnp.int32, (tt, 1), 0)

    for g, w in enumerate(POOL_WINDOWS):
        lanes = slice(g * POOL_CH, (g + 1) * POOL_CH)
        tok = extp_ref[HIST_P:HIST_P + tt, lanes]
        win = tok
        for j in range(1, w):
            win = win + extp_ref[HIST_P - j:HIST_P - j + tt, lanes]
        cnt = jnp.minimum(w, pos + 1).astype(F32)
        d = win / cnt - tok
        y = _dot(d.astype(BF16), pw_ref[g]) * pscale_ref[:, lanes]
        yp_ref[:, lanes] = y
    hist = extp_ref[tv:tv + HIST_P, :]
    extp_ref[0:HIST_P, :] = hist
    pnew_ref[...] = hist

    base = HIST_C - (CONV_W - 1)
    xc = extc_ref[base:base + tt, :] * cw_ref[0:1, :]
    for k in range(1, CONV_W):
        xc = xc + extc_ref[base + k:base + k + tt, :] * cw_ref[k:k + 1, :]
    xc = cb_ref[...] + xc
    chist = extc_ref[tv:tv + HIST_C, :]
    extc_ref[0:HIST_C, :] = chist
    cnew_ref[...] = chist

    ri = _dot(xc.astype(BF16), wax_ref[...])
    r = jax.nn.sigmoid(ri[:, 0:D_GROUP] + ba_ref[...])
    gi = jax.nn.sigmoid(ri[:, D_GROUP:2 * D_GROUP] + bx_ref[...])
    log_a = -LRU_C * r * _softplus(-lam_ref[...])
    a = jnp.exp(log_a)
    mult = jnp.sqrt(-jnp.tanh(log_a) * (a * a + 1.0))
    mult = jnp.where(pos == 0, 1.0, mult)
    b = mult * (gi * xc)

    rows = lax.broadcasted_iota(jnp.int32, (tt, 1), 0)
    d = 1
    while d < tt:
        ok = rows >= d
        a_s = jnp.where(ok, pltpu.roll(a, d, 0), 1.0)
        b_s = jnp.where(ok, pltpu.roll(b, d, 0), 0.0)
        b = a * b_s + b
        a = a * a_s
        d *= 2
    h = b + a * h_ref[...]
    hlast = h[tv - 1:tv, :]
    h_ref[...] = hlast
    hnew_ref[...] = hlast
    yl_ref[...] = jax.nn.gelu(gate_ref[...]) * h


def _seqmix(proj3, pprev, cprev, h0, pw, pscale, cw, cb, wax, ba, bx, lam, tt, tv, pos0):
    b, t, _ = proj3.shape
    nt = t // tt
    col = lambda s: pl.BlockSpec((None, tt, D_GROUP), lambda bi, ti, s=s: (bi, ti, s))
    per_b = lambda r: pl.BlockSpec((None, r, D_GROUP), lambda bi, ti: (bi, 0, 0))
    const2 = lambda shp: pl.BlockSpec(shp, lambda bi, ti: (0, 0))
    return pl.pallas_call(
        functools.partial(_seqmix_kernel, tt=tt, tv=tv, pos0=pos0),
        out_shape=(
            jax.ShapeDtypeStruct((b, t, D_GROUP), F32),
            jax.ShapeDtypeStruct((b, t, D_GROUP), F32),
            jax.ShapeDtypeStruct((b, HIST_P, D_GROUP), F32),
            jax.ShapeDtypeStruct((b, HIST_C, D_GROUP), F32),
            jax.ShapeDtypeStruct((b, 1, D_GROUP), F32),
        ),
        grid=(b, nt),
        in_specs=[col(0), col(1), col(2), per_b(HIST_P), per_b(HIST_C), per_b(1),
                  pl.BlockSpec((len(POOL_WINDOWS), POOL_CH, POOL_CH), lambda bi, ti: (0, 0, 0)),
                  const2((1, D_GROUP)), const2((HIST_C, D_GROUP)), const2((1, D_GROUP)),
                  const2((D_GROUP, 2 * D_GROUP)), const2((1, D_GROUP)), const2((1, D_GROUP)),
                  const2((1, D_GROUP))],
        out_specs=(
            pl.BlockSpec((None, tt, D_GROUP), lambda bi, ti: (bi, ti, 0)),
            pl.BlockSpec((None, tt, D_GROUP), lambda bi, ti: (bi, ti, 0)),
            per_b(HIST_P), per_b(HIST_C), per_b(1),
        ),
        scratch_shapes=[pltpu.VMEM((HIST_P + tt, D_GROUP), F32),
                        pltpu.VMEM((HIST_C + tt, D_GROUP), F32),
                        pltpu.VMEM((1, D_GROUP), F32)],
        compiler_params=_cparams(("arbitrary", "arbitrary")),
        name="seqmix",
    )(proj3, proj3, proj3, pprev, cprev, h0, pw, pscale, cw, cb, wax, ba, bx, lam)


def _pair_tables(n, ratio, reverse):
    qi, ki = [], []
    for q in range(n):
        ks = range((q + 1) * ratio)
        for k in (reversed(ks) if reverse else ks):
            qi.append(q)
            ki.append(k)
    return jnp.asarray(np.array(qi, np.int32)), jnp.asarray(np.array(ki, np.int32))


def _key_minus_query(tk, tq):
    return (lax.broadcasted_iota(jnp.int32, (tk, tq), 0) - lax.broadcasted_iota(jnp.int32, (tk, tq), 1))


def _fox_prompt_kernel(qi_ref, ki_ref, qt_ref, k_ref, vt_ref, cq_ref, o_ref, m_ref, l_ref, acc_ref, *, tq, tk):
    s_id = pl.program_id(1)
    qi = qi_ref[s_id]
    ki = ki_ref[s_id]
    ratio = tq // tk

    @pl.when(ki == 0)
    def _():
        m_ref[...] = jnp.full_like(m_ref, NEG_BIG)
        l_ref[...] = jnp.zeros_like(l_ref)
        acc_ref[...] = jnp.zeros_like(acc_ref)

    def step(masked):
        if masked:
            causal = _key_minus_query(tk, tq) <= qi * tq - ki * tk

        def scores(h):
            slot = slice(h * LANES, (h + 1) * LANES)
            return _dot(k_ref[:, slot], qt_ref[slot, :])

        def softmax_update(h, st):
            if masked:
                st = jnp.where(causal, st, NEG_BIG)
            cq = cq_ref[h:h + 1, :]
            m_old = m_ref[h:h + 1, :]
            m_new = jnp.maximum(m_old, jnp.max(st, axis=0, keepdims=True) + cq)
            alpha = jnp.exp(m_old - m_new)
            pt = jnp.exp(st - (m_new - cq))
            l_ref[h:h + 1, :] = alpha * l_ref[h:h + 1, :] + jnp.sum(pt, axis=0, keepdims=True)
            m_ref[h:h + 1, :] = m_new
            return alpha, pt.astype(BF16)

        st = {0: scores(0), 1: scores(1)}
        for h in range(N_HEADS):
            rows = slice(h * HEAD_DIM, (h + 1) * HEAD_DIM)
            if h + 2 < N_HEADS:
                st[h + 2] = scores(h + 2)
            alpha, pt = softmax_update(h, st.pop(h))
            acc_ref[rows, :] = alpha * acc_ref[rows, :] + _dot(vt_ref[rows, :], pt)

    @pl.when(ki < qi * ratio)
    def _():
        step(False)

    @pl.when(ki >= qi * ratio)
    def _():
        step(True)

    @pl.when(ki == (qi + 1) * ratio - 1)
    def _():
        for h in range(N_HEADS):
            rows = slice(h * HEAD_DIM, (h + 1) * HEAD_DIM)
            acc_ref[rows, :] = acc_ref[rows, :] / l_ref[h:h + 1, :]
        o_ref[...] = acc_ref[...].T


def _sb_prompt_kernel(qi_ref, ki_ref, qt_ref, k_ref, vt_ref, tri_ref, o_ref, carry_ref, acc_ref, *, tq, tk):
    s_id = pl.program_id(1)
    qi = qi_ref[s_id]
    ki = ki_ref[s_id]
    ratio = tq // tk

    @pl.when(ki == (qi + 1) * ratio - 1)
    def _():
        carry_ref[...] = jnp.zeros_like(carry_ref)
        acc_ref[...] = jnp.zeros_like(acc_ref)

    def step(masked):
        if masked:
            strict = _key_minus_query(tk, tq) < qi * tq - ki * tk

        def logits(h):
            slot = slice(h * LANES, (h + 1) * LANES)
            return _dot(k_ref[:, slot], qt_ref[slot, :])

        def log_one_minus_beta(zt):
            l1m = _neg_softplus(zt)
            if masked:
                l1m = jnp.where(strict, l1m, 0.0)
            return (l1m,) + tuple(_split_bf16(l1m, 2))

        def later_keys_sum(hi, lo):
            return _dot(tri_ref[...], hi) + _dot(tri_ref[...], lo)

        def weights(h, zt, l1m, rest):
            carry = carry_ref[h:h + 1, :]
            wgt = jnp.exp(zt + l1m + rest + carry)
            if masked:
                wgt = jnp.where(strict, wgt, 0.0)
            carry_ref[h:h + 1, :] = carry + rest[0:1, :] + l1m[0:1, :]
            return wgt.astype(BF16)

        zt = {0: logits(0), 1: logits(1)}
        lg = {0: log_one_minus_beta(zt[0])}
        rest = {}
        for h in range(N_HEADS + 1):
            if h < N_HEADS:
                rest[h] = later_keys_sum(*lg[h][1:])
            if h + 2 < N_HEADS:
                zt[h + 2] = logits(h + 2)
            if h >= 1:
                g = h - 1
                rows = slice(g * HEAD_DIM, (g + 1) * HEAD_DIM)
                wgt = weights(g, zt.pop(g), lg.pop(g)[0], rest.pop(g))
                acc_ref[rows, :] = acc_ref[rows, :] + _dot(vt_ref[rows, :], wgt)
            if h + 1 < N_HEADS:
                lg[h + 1] = log_one_minus_beta(zt[h + 1])

    @pl.when(ki >= qi * ratio)
    def _():
        step(True)

    @pl.when(ki < qi * ratio)
    def _():
        step(False)

    @pl.when(ki == 0)
    def _():
        o_ref[...] = acc_ref[...].T


def _prompt_attention(kind, qt, k, vt, extra, tq, tk):
    b, t, _ = k.shape
    nq = t // tq
    qi, ki = _pair_tables(nq, tq // tk, reverse=(kind == "sb"))
    in_specs = [pl.BlockSpec((None, D_AUG, tq), lambda bi, s, qi, ki: (bi, 0, qi[s])),
                pl.BlockSpec((None, tk, D_AUG), lambda bi, s, qi, ki: (bi, ki[s], 0)),
                pl.BlockSpec((None, D_GROUP, tk), lambda bi, s, qi, ki: (bi, 0, ki[s]))]
    if kind == "fox":
        in_specs.append(pl.BlockSpec((None, N_HEADS, tq), lambda bi, s, qi, ki: (bi, 0, qi[s])))
        body = _fox_prompt_kernel
        scratch = [pltpu.VMEM((N_HEADS, tq), F32), pltpu.VMEM((N_HEADS, tq), F32),
                   pltpu.VMEM((D_GROUP, tq), F32)]
    else:
        in_specs.append(pl.BlockSpec((tk, tk), lambda bi, s, qi, ki: (0, 0)))
        body = _sb_prompt_kernel
        scratch = [pltpu.VMEM((N_HEADS, tq), F32), pltpu.VMEM((D_GROUP, tq), F32)]
    grid_spec = pltpu.PrefetchScalarGridSpec(
        num_scalar_prefetch=2,
        grid=(b, int(qi.shape[0])),
        in_specs=in_specs,
        out_specs=pl.BlockSpec((None, tq, D_GROUP), lambda bi, s, qi, ki: (bi, qi[s], 0)),
        scratch_shapes=scratch,
    )
    return pl.pallas_call(
        functools.partial(body, tq=tq, tk=tk),
        out_shape=jax.ShapeDtypeStruct((b, t, D_GROUP), F32),
        grid_spec=grid_spec,
        compiler_params=_cparams(("arbitrary", "arbitrary")),
        name=kind + "_prompt",
    )(qi, ki, qt, k, vt, extra)


def _extract_heads(acc, t_new):
    hrow = lax.broadcasted_iota(jnp.int32, (N_HEADS, D_GROUP), 0)
    hcol = lax.broadcasted_iota(jnp.int32, (N_HEADS, D_GROUP), 1) // HEAD_DIM
    own = hrow == hcol
    outs = []
    for t in range(t_new):
        blk = acc[t * N_HEADS:(t + 1) * N_HEADS, :]
        outs.append(jnp.sum(jnp.where(own, blk, 0.0), axis=0, keepdims=True))
    return outs


def _fox_sample_kernel(pt_ref, wq_ref, new_ref, gt_ref, grow_ref, tri_ref, *rest, t_new, n_steps):
    g = PAGES_PER_STEP
    cache_refs = rest[0:g]
    lft_refs = rest[g:2 * g]
    o_ref, m_ref, l_ref, acc_ref, dcar_ref = rest[2 * g:]
    p = pl.program_id(1)
    rows = t_new * N_HEADS

    def process(kv_refs, biases, mask):
        s = []
        for kv_ref, bias in zip(kv_refs, biases):
            kt = kv_ref[0:D_GROUP, :].astype(BF16)
            s.append(_dot(wq_ref[...], kt) + jnp.concatenate([bias] * t_new, axis=0) + grow_ref[...])
        s = s[0] if len(s) == 1 else jnp.concatenate(s, axis=1)
        if mask is not None:
            s = jnp.where(mask, s, NEG_BIG)
        m_old = m_ref[...]
        m_new = jnp.maximum(m_old, jnp.max(s, axis=-1, keepdims=True))
        alpha = jnp.exp(m_old - m_new)
        pr = jnp.exp(s - m_new)
        l_ref[...] = alpha * l_ref[...] + jnp.sum(pr, axis=-1, keepdims=True)
        pv = None
        for i, kv_ref in enumerate(kv_refs):
            vt = kv_ref[D_GROUP:2 * D_GROUP, :].astype(BF16)
            t = _dot_nt(pr[:, i * PAGE:(i + 1) * PAGE].astype(BF16), vt)
            pv = t if pv is None else pv + t
        acc_ref[...] = alpha * acc_ref[...] + pv
        m_ref[...] = m_new

    @pl.when(p == 0)
    def _():
        m_ref[...] = jnp.full_like(m_ref, NEG_BIG)
        l_ref[...] = jnp.zeros_like(l_ref)
        acc_ref[...] = jnp.zeros_like(acc_ref)
        dcar_ref[...] = jnp.zeros_like(dcar_ref)
        tok = lax.broadcasted_iota(jnp.int32, (rows, PAGE), 0) // N_HEADS
        key = lax.broadcasted_iota(jnp.int32, (rows, PAGE), 1)
        process([new_ref], [-gt_ref[...]], key <= tok)

    @pl.when(p > 0)
    def _():
        lfs = [r[...] for r in lft_refs]
        parts = []
        for lf in lfs:
            parts += [x.astype(F32) for x in _split_bf16(lf, N_SPLIT)]
        r = _dot(jnp.concatenate(parts, axis=0).astype(BF16), tri_ref[...])
        run = dcar_ref[...]
        biases = []
        for i, lf in enumerate(lfs):
            base = i * N_SPLIT * N_HEADS
            later = r[base:base + N_HEADS]
            for j in range(1, N_SPLIT):
                later = later + r[base + j * N_HEADS:base + (j + 1) * N_HEADS]
            biases.append(later + run)
            run = run + later[:, 0:1] + lf[:, 0:1]
        dcar_ref[...] = run
        process(cache_refs, biases, None)

    @pl.when(p == n_steps)
    def _():
        out = acc_ref[...] / l_ref[...]
        for t, rowv in enumerate(_extract_heads(out, t_new)):
            o_ref[t:t + 1, :] = rowv


def _sb_sample_kernel(pt_ref, wq_ref, new_ref, tri_ref, *rest, t_new, n_steps):
    g = PAGES_PER_STEP
    cache_refs = rest[0:g]
    o_ref, acc_ref, car_ref = rest[g:]
    p = pl.program_id(1)
    rows = t_new * N_HEADS

    def process(kv_refs, mask):
        zs, l1ms, parts = [], [], []
        for kv_ref in kv_refs:
            kt = kv_ref[0:D_GROUP, :].astype(BF16)
            z = _dot(wq_ref[...], kt)
            l1m = _neg_softplus(z)
            if mask is not None:
                l1m = jnp.where(mask, l1m, 0.0)
            zs.append(z)
            l1ms.append(l1m)
            parts += [x.astype(F32) for x in _split_bf16(l1m, 2)]
        r = _dot(jnp.concatenate(parts, axis=0).astype(BF16), tri_ref[...])
        run = car_ref[...]
        pv = None
        for i, kv_ref in enumerate(kv_refs):
            rest_i = r[2 * i * rows:(2 * i + 1) * rows] + r[(2 * i + 1) * rows:(2 * i + 2) * rows]
            wgt = jnp.exp(zs[i] + l1ms[i] + rest_i + run)
            if mask is not None:
                wgt = jnp.where(mask, wgt, 0.0)
            run = run + rest_i[:, 0:1] + l1ms[i][:, 0:1]
            vt = kv_ref[D_GROUP:2 * D_GROUP, :].astype(BF16)
            t = _dot_nt(wgt.astype(BF16), vt)
            pv = t if pv is None else pv + t
        car_ref[...] = run
        acc_ref[...] = acc_ref[...] + pv

    @pl.when(p == 0)
    def _():
        acc_ref[...] = jnp.zeros_like(acc_ref)
        car_ref[...] = jnp.zeros_like(car_ref)
        tok = lax.broadcasted_iota(jnp.int32, (rows, PAGE), 0) // N_HEADS
        key = lax.broadcasted_iota(jnp.int32, (rows, PAGE), 1)
        process([new_ref], key < tok)

    @pl.when(p > 0)
    def _():
        process(cache_refs, None)

    @pl.when(p == n_steps)
    def _():
        for t, rowv in enumerate(_extract_heads(acc_ref[...], t_new)):
            o_ref[t:t + 1, :] = rowv


def _sample_attention(kind, pt, wq, new_kvt, cache, layer, tri, t_new, fox_extra=None):
    b = wq.shape[0]
    n_pages = pt.shape[0] // b
    g = PAGES_PER_STEP
    n_steps = n_pages // g
    rows = t_new * N_HEADS

    def page_spec(block, i):
        def idx(bi, p, pt):
            newest_first = (jnp.maximum(p, 1) - 1) * g + i
            return (layer, pt[bi * n_pages + (n_pages - 1 - newest_first)], 0, 0)
        return pl.BlockSpec((None, None) + block, idx)

    per_b = lambda r, c: pl.BlockSpec((None, r, c), lambda bi, p, pt: (bi, 0, 0))
    tri_spec = pl.BlockSpec((PAGE, PAGE), lambda bi, p, pt: (0, 0))
    cache_specs = [page_spec((2 * D_GROUP, PAGE), i) for i in range(g)]
    if kind == "fox":
        gt, lft_cache, grow = fox_extra
        in_specs = ([per_b(rows, D_GROUP), per_b(2 * D_GROUP, PAGE), per_b(N_HEADS, PAGE), per_b(rows, 1),
                     tri_spec] + cache_specs + [page_spec((N_HEADS, PAGE), i) for i in range(g)])
        args = (pt, wq, new_kvt, gt, grow, tri) + (cache,) * g + (lft_cache,) * g
        body = _fox_sample_kernel
        scratch = [pltpu.VMEM((rows, 1), F32), pltpu.VMEM((rows, 1), F32),
                   pltpu.VMEM((rows, D_GROUP), F32), pltpu.VMEM((N_HEADS, 1), F32)]
    else:
        in_specs = [per_b(rows, D_GROUP), per_b(2 * D_GROUP, PAGE), tri_spec] + cache_specs
        args = (pt, wq, new_kvt, tri) + (cache,) * g
        body = _sb_sample_kernel
        scratch = [pltpu.VMEM((rows, D_GROUP), F32), pltpu.VMEM((rows, 1), F32)]
    grid_spec = pltpu.PrefetchScalarGridSpec(
        num_scalar_prefetch=1,
        grid=(b, n_steps + 1),
        in_specs=in_specs,
        out_specs=pl.BlockSpec((None, t_new, D_GROUP), lambda bi, p, pt: (bi, 0, 0)),
        scratch_shapes=scratch,
    )
    return pl.pallas_call(
        functools.partial(body, t_new=t_new, n_steps=n_steps),
        out_shape=jax.ShapeDtypeStruct((b, t_new, D_GROUP), F32),
        grid_spec=grid_spec,
        compiler_params=_cparams(("arbitrary", "arbitrary")),
        name=kind + "_sample",
    )(*args)


def _outproj_kernel(x_ref, y0_ref, y1_ref, y2_ref, y3_ref, g_ref, w_ref, o_ref):
    acc = x_ref[...]
    for gi, y_ref in enumerate((y0_ref, y1_ref, y2_ref, y3_ref)):
        y = y_ref[...]
        ms = jnp.mean(y * y, axis=-1, keepdims=True)
        yn = (y * lax.rsqrt(ms + EPS) * g_ref[gi:gi + 1, :]).astype(BF16)
        acc = acc + _dot(yn, w_ref[gi * D_GROUP:(gi + 1) * D_GROUP, :])
    o_ref[...] = acc


def _outproj(x, ys, g, w, tm):
    m, d = x.shape
    yspec = pl.BlockSpec((tm, D_GROUP), lambda i: (i, 0))
    return pl.pallas_call(
        _outproj_kernel,
        out_shape=jax.ShapeDtypeStruct((m, d), F32),
        grid=(m // tm,),
        in_specs=[pl.BlockSpec((tm, d), lambda i: (i, 0)), yspec, yspec, yspec, yspec,
                  pl.BlockSpec((4, D_GROUP), lambda i: (0, 0)),
                  pl.BlockSpec((4 * D_GROUP, d), lambda i: (0, 0))],
        out_specs=pl.BlockSpec((tm, d), lambda i: (i, 0)),
        compiler_params=_cparams(("arbitrary",)),
        name="outproj",
    )(x, *ys, g, w)


def _mlp_kernel(x_ref, g_ref, wu_ref, wd_ref, o_ref, xn_ref):
    @pl.when(pl.program_id(1) == 0)
    def _():
        x = x_ref[...]
        ms = jnp.mean(x * x, axis=-1, keepdims=True)
        xn_ref[...] = (x * lax.rsqrt(ms + EPS) * g_ref[...]).astype(BF16)
        o_ref[...] = x

    hid = jnp.maximum(_dot(xn_ref[...], wu_ref[...]), 0.0)
    o_ref[...] += _dot((hid * hid).astype(BF16), wd_ref[...])


def _mlp(x, g, wu, wd, tm, tf):
    m, d = x.shape
    f = wu.shape[1]
    return pl.pallas_call(
        _mlp_kernel,
        out_shape=jax.ShapeDtypeStruct((m, d), F32),
        grid=(m // tm, f // tf),
        in_specs=[pl.BlockSpec((tm, d), lambda i, j: (i, 0)),
                  pl.BlockSpec((1, d), lambda i, j: (0, 0)),
                  pl.BlockSpec((d, tf), lambda i, j: (0, j)),
                  pl.BlockSpec((tf, d), lambda i, j: (j, 0))],
        out_specs=pl.BlockSpec((tm, d), lambda i, j: (i, 0)),
        scratch_shapes=[pltpu.VMEM((tm, d), BF16)],
        compiler_params=_cparams(("arbitrary", "arbitrary"), 56),
        name="mlp",
    )(x, g, wu, wd)


def _block_diag(w):
    n, d, _ = w.shape
    eye = jnp.eye(n, dtype=w.dtype)
    return (eye[:, None, :, None] * w[:, :, None, :]).reshape(n * d, n * d)


def _layer_weights(l, norm1_g, w_in, pool_w, pool_scale, conv_w, conv_b, lru_wa, lru_ba, lru_wx, lru_bx,
                   lru_lambda, fox_bf, fox_q_g, fox_k_g, out_g, w_out, norm2_g, w_up, w_down):
    g = D_GROUP
    wi = w_in[l]
    nf = 6 * g
    w_main = jnp.concatenate([wi[:, :nf], wi[:, nf + N_HEADS:]], axis=1).astype(BF16)
    w_f = jnp.pad(wi[:, nf:nf + N_HEADS], ((0, 0), (0, LANES - N_HEADS))).astype(BF16)
    return dict(
        norm1_g=norm1_g[l][None, :], w_main=w_main, w_f=w_f,
        pool_w=pool_w[l].astype(BF16), pool_scale=pool_scale[l][None, :],
        conv_w=jnp.pad(conv_w[l], ((0, HIST_C - CONV_W), (0, 0))), conv_b=conv_b[l][None, :],
        wax=jnp.concatenate([_block_diag(lru_wa[l]), _block_diag(lru_wx[l])], axis=1).astype(BF16),
        ba=lru_ba[l][None, :], bx=lru_bx[l][None, :], lam=lru_lambda[l][None, :],
        bf=jnp.pad(fox_bf[l], (0, LANES - N_HEADS))[None, :],
        gq=jnp.tile(fox_q_g[l], N_HEADS)[None, :], gk=jnp.tile(fox_k_g[l], N_HEADS)[None, :],
        out_g=out_g[l].reshape(4, g), w_out=w_out[l].astype(BF16),
        norm2_g=norm2_g[l][None, :], w_up=w_up[l].astype(BF16), w_down=w_down[l].astype(BF16),
    )


def _tile(m, pref):
    t = min(m, pref)
    while m % t:
        t //= 2
    return t


def _slot_constants():
    place = np.zeros((D_GROUP, D_AUG), np.float32)
    for c in range(D_GROUP):
        place[c, (c // HEAD_DIM) * LANES + c % HEAD_DIM] = 1.0
    placec = np.zeros((N_SPLIT, LANES, D_AUG), np.float32)
    neg = np.zeros((1, D_AUG), np.float32)
    for i in range(N_SPLIT):
        for h in range(N_HEADS):
            placec[i, h, h * LANES + HEAD_DIM + i] = 1.0
            neg[0, h * LANES + HEAD_DIM + i] = -1.0
    return jnp.asarray(place, BF16), jnp.asarray(placec, BF16), jnp.asarray(neg, F32)


def _expand_queries(q, b, t_new):
    q4 = q.reshape(b, t_new, 1, D_GROUP)
    own = (jnp.arange(D_GROUP)[None, :] // HEAD_DIM) == jnp.arange(N_HEADS)[:, None]
    return jnp.where(own[None, None], q4, jnp.zeros_like(q4)).reshape(b, t_new * N_HEADS, D_GROUP)


def _pages_view(cache):
    d, n = cache.shape[0], cache.shape[1]
    return jnp.transpose(cache, (0, 1, 3, 4, 5, 2)).reshape(d, n, 2 * D_GROUP, PAGE)


def kernel(x_prompt, x_sample, cache_fox_kv, cache_fox_logf, cache_sb_kv, state_pool, state_conv, state_lru,
           page_table, norm1_g, w_in, pool_w, pool_scale, conv_w, conv_b, lru_wa, lru_ba, lru_wx, lru_bx,
           lru_lambda, fox_bf, fox_q_g, fox_k_g, out_g, w_out, norm2_g, w_up, w_down):
    depth = w_in.shape[0]
    bp, seq, d_model = x_prompt.shape
    bs, t_new, _ = x_sample.shape
    past_len = page_table.shape[1] * PAGE
    g = D_GROUP

    seg = jnp.asarray(np.kron(np.eye(N_HEADS), np.ones((HEAD_DIM, HEAD_DIM))), BF16)
    place, placec, neg = _slot_constants()
    tk = _tile(seq, 256)
    tq = _tile(seq, 512)
    tri_q = jnp.asarray(np.triu(np.ones((tk, tk)), 1), BF16)
    tri_p = jnp.asarray(np.tril(np.ones((PAGE, PAGE)), -1), BF16)
    pt_flat = page_table.reshape(-1).astype(jnp.int32)
    fox_cache = _pages_view(cache_fox_kv)
    sb_cache = _pages_view(cache_sb_kv)
    lft_cache = jnp.swapaxes(cache_fox_logf, 2, 3)
    t_pad = 8

    xp = x_prompt.reshape(bp * seq, d_model)
    xs = x_sample.reshape(bs * t_new, d_model)
    st_p, st_s = [], []
    for l in range(depth):
        w = _layer_weights(l, norm1_g, w_in, pool_w, pool_scale, conv_w, conv_b, lru_wa, lru_ba, lru_wx,
                           lru_bx, lru_lambda, fox_bf, fox_q_g, fox_k_g, out_g, w_out, norm2_g, w_up, w_down)

        def dense_tail(x, ys):
            rows = x.shape[0]
            x1 = _outproj(x, ys, w["out_g"], w["w_out"], _tile(rows, 512))
            return _mlp(x1, w["norm2_g"], w["w_up"], w["w_down"], _tile(rows, 1024),
                        _tile(w["w_up"].shape[1], 512))

        m = bp * seq
        proj, flog = _inproj(xp, w["norm1_g"], w["w_main"], w["w_f"], _tile(m, 1024))
        r3 = lambda a: a.reshape(bp, seq, a.shape[-1])
        fkvt, skvt, lft, ct, fqt, fka, fvt, sqt, ska, svt = _prep_prompt(
            r3(proj), r3(flog), w["gq"], w["gk"], w["bf"], seg, place, placec, neg, _tile(seq, 256))
        y_fox = _prompt_attention("fox", fqt, fka, fvt, ct, tq, tk)
        y_sb = _prompt_attention("sb", sqt, ska, svt, tri_q, tq, tk)
        tt = _tile(seq, 256)
        y_pool, y_lru, pnew, cnew, hnew = _seqmix(
            r3(proj), jnp.zeros((bp, HIST_P, g), F32), jnp.zeros((bp, HIST_C, g), F32),
            jnp.zeros((bp, 1, g), F32), w["pool_w"], w["pool_scale"], w["conv_w"], w["conv_b"], w["wax"],
            w["ba"], w["bx"], w["lam"], tt, tt, 0)
        flat = lambda a: a.reshape(m, g)
        xp = dense_tail(xp, (flat(y_pool), flat(y_lru), flat(y_fox), flat(y_sb)))
        kv_state = lambda a: jnp.transpose(a.reshape(bp, 2, N_HEADS, HEAD_DIM, seq), (0, 4, 1, 2, 3))
        st_p.append((kv_state(fkvt), jnp.swapaxes(lft, 1, 2), kv_state(skvt), pnew[:, 1:],
                     cnew[:, HIST_C - CONV_W + 1:], hnew[:, 0]))

        ms = bs * t_new
        proj, flog = _inproj(xs, w["norm1_g"], w["w_main"], w["w_f"], ms)
        fkv, skv, lf, c, ct, qf, qs = _prep_sample(proj, flog, w["gq"], w["gk"], w["bf"], seg, t_new)
        new_page = lambda a: jnp.pad(jnp.swapaxes(a.reshape(bs, t_new, 2 * g), 1, 2),
                                     ((0, 0), (0, 0), (0, PAGE - t_new)))
        gt = jnp.pad(jnp.swapaxes(ct.reshape(N_HEADS, bs, t_new), 0, 1), ((0, 0), (0, 0), (0, PAGE - t_new)))
        grow = c[:, :N_HEADS].reshape(bs, t_new * N_HEADS, 1)
        y_fox = _sample_attention("fox", pt_flat, _expand_queries(qf, bs, t_new), new_page(fkv), fox_cache, l,
                                  tri_p, t_new, (gt, lft_cache, grow))
        y_sb = _sample_attention("sb", pt_flat, _expand_queries(qs, bs, t_new), new_page(skv), sb_cache, l,
                                 tri_p, t_new)
        proj3 = jnp.pad(proj.reshape(bs, t_new, -1), ((0, 0), (0, t_pad - t_new), (0, 0)))
        y_pool, y_lru, pnew, cnew, hnew = _seqmix(
            proj3, jnp.pad(state_pool[l], ((0, 0), (HIST_P - POOL_MAX + 1, 0), (0, 0))),
            jnp.pad(state_conv[l], ((0, 0), (HIST_C - CONV_W + 1, 0), (0, 0))), state_lru[l][:, None, :],
            w["pool_w"], w["pool_scale"], w["conv_w"], w["conv_b"], w["wax"], w["ba"], w["bx"], w["lam"],
            t_pad, t_new, past_len)
        cut = lambda a: a[:, :t_new].reshape(ms, g)
        xs = dense_tail(xs, (cut(y_pool), cut(y_lru), y_fox.reshape(ms, g), y_sb.reshape(ms, g)))
        st_s.append((fkv.reshape(bs, t_new, 2, N_HEADS, HEAD_DIM), lf[:, :N_HEADS].reshape(bs, t_new, N_HEADS),
                     skv.reshape(bs, t_new, 2, N_HEADS, HEAD_DIM), pnew[:, 1:], cnew[:, HIST_C - CONV_W + 1:],
                     hnew[:, 0]))

    stk = lambda sts, j: jnp.stack([s[j] for s in sts], axis=0)
    return ((xp.reshape(bp, seq, d_model), xs.reshape(bs, t_new, d_model))
            + tuple(stk(st_p, j) for j in range(6)) + tuple(stk(st_s, j) for j in range(6)))
```

```python
import functools

import numpy as np
import jax
import jax.numpy as jnp
from jax import lax
from jax.experimental import pallas as pl
from jax.experimental.pallas import tpu as pltpu

F32 = jnp.float32
BF16 = jnp.bfloat16

D_GROUP = 512
HEAD_DIM = 64
N_HEADS = D_GROUP // HEAD_DIM
POOL_WINDOWS = (2, 4, 8, 16)
POOL_MAX = max(POOL_WINDOWS)
POOL_CH = D_GROUP // len(POOL_WINDOWS)
CONV_W = 4
LRU_C = 8.0
EPS = 1e-6
PAGE = 128
LANES = 128
D_AUG = N_HEADS * LANES
N_SPLIT = 3
HIST_P = 16
HIST_C = 8
NEG_LOG2E = -1.4426950408889634
NEG_BIG = -1e30
MIB = 1024 * 1024


def _cparams(sem, vmem_mib=48):
    return pltpu.CompilerParams(dimension_semantics=sem, vmem_limit_bytes=vmem_mib * MIB)


def _split_bf16(x, parts):
    out = []
    r = x
    for _ in range(parts - 1):
        h = r.astype(BF16)
        out.append(h)
        r = r - h.astype(F32)
    out.append(r.astype(BF16))
    return out


def _softplus(x):
    return jnp.maximum(x, 0.0) + jnp.log1p(jnp.exp(-jnp.abs(x)))


def _log_sigmoid(x):
    return -_softplus(-x)


def _softplus_exp2(z):
    return jnp.maximum(z, 0.0) + jnp.log(1.0 + jnp.exp2(jnp.abs(z) * NEG_LOG2E))


def _neg_softplus(z):
    return -(jnp.maximum(z, 0.0) + jnp.log(1.0 + jnp.exp(-jnp.abs(z))))


def _dot_nt(a, b):
    return lax.dot_general(a, b, (((1,), (1,)), ((), ())), preferred_element_type=F32)


def _dot(a, b):
    return jnp.dot(a, b, preferred_element_type=F32)


def _inproj_kernel(x_ref, g_ref, w_ref, wf_ref, o_ref, of_ref, xn_ref):
    @pl.when(pl.program_id(1) == 0)
    def _():
        x = x_ref[...]
        ms = jnp.mean(x * x, axis=-1, keepdims=True)
        xn = (x * lax.rsqrt(ms + EPS) * g_ref[...]).astype(BF16)
        xn_ref[...] = xn
        of_ref[...] = _dot(xn, wf_ref[...])

    o_ref[...] = _dot(xn_ref[...], w_ref[...])


def _inproj(x, g, w, wf, tm):
    m, d = x.shape
    n = w.shape[1]
    tn = D_GROUP
    return pl.pallas_call(
        _inproj_kernel,
        out_shape=(jax.ShapeDtypeStruct((m, n), F32), jax.ShapeDtypeStruct((m, LANES), F32)),
        grid=(m // tm, n // tn),
        in_specs=[
            pl.BlockSpec((tm, d), lambda i, j: (i, 0)),
            pl.BlockSpec((1, d), lambda i, j: (0, 0)),
            pl.BlockSpec((d, tn), lambda i, j: (0, j)),
            pl.BlockSpec((d, LANES), lambda i, j: (0, 0)),
        ],
        out_specs=(
            pl.BlockSpec((tm, tn), lambda i, j: (i, j)),
            pl.BlockSpec((tm, LANES), lambda i, j: (i, 0)),
        ),
        scratch_shapes=[pltpu.VMEM((tm, d), BF16)],
        compiler_params=_cparams(("arbitrary", "arbitrary")),
        name="inproj",
    )(x, g, w, wf)


def _head_norm(x, g, seg_ref):
    hi, lo = _split_bf16(x * x, 2)
    ss = _dot(hi, seg_ref[...]) + _dot(lo, seg_ref[...])
    return x * lax.rsqrt(ss * (1.0 / HEAD_DIM) + EPS) * g


def _running_sum(lf, seq, tm):
    r = lax.broadcasted_iota(jnp.int32, (tm, tm), 0)
    c = lax.broadcasted_iota(jnp.int32, (tm, tm), 1)
    keep = c <= r
    if seq < tm:
        keep = jnp.logical_and(keep, (r // seq) == (c // seq))
    tri = jnp.where(keep, 1.0, 0.0).astype(BF16)
    cs = None
    for part in _split_bf16(lf, N_SPLIT):
        t = _dot(tri, part)
        cs = t if cs is None else cs + t
    return cs


def _prep_prompt_kernel(fq_ref, fk_ref, fv_ref, sq_ref, sk_ref, sv_ref, fl_ref, gq_ref, gk_ref, bf_ref,
                        seg_ref, place_ref, placec_ref, neg_ref,
                        fkvt_ref, skvt_ref, lft_ref, ct_ref,
                        fqt_ref, fka_ref, fvt_ref, sqt_ref, ska_ref, svt_ref, carry_ref, *, tm):
    scale = HEAD_DIM ** -0.5
    fq = _head_norm(fq_ref[...], gq_ref[...], seg_ref)
    fk = _head_norm(fk_ref[...], gk_ref[...], seg_ref)
    fv = fv_ref[...]
    sk = sk_ref[...]
    sv = sv_ref[...]

    fvt = fv.T
    svt = sv.T
    fkvt_ref[0:D_GROUP, :] = fk.T
    fkvt_ref[D_GROUP:2 * D_GROUP, :] = fvt
    skvt_ref[0:D_GROUP, :] = sk.T
    skvt_ref[D_GROUP:2 * D_GROUP, :] = svt
    fvt_ref[...] = fvt.astype(BF16)
    svt_ref[...] = svt.astype(BF16)

    lf = _log_sigmoid(fl_ref[...] + bf_ref[...])
    cs = _running_sum(lf, tm, tm)

    @pl.when(pl.program_id(1) == 0)
    def _():
        carry_ref[...] = jnp.zeros_like(carry_ref)
    cs = cs + carry_ref[...]
    carry_ref[...] = cs[tm - 1:tm, :]
    lft_ref[...] = lf.T[0:N_HEADS, :]
    ct_ref[...] = cs.T[0:N_HEADS, :]

    ka = _dot(fk.astype(BF16), place_ref[...])
    for i, part in enumerate(_split_bf16(cs, N_SPLIT)):
        ka = ka + _dot(part, placec_ref[i])
    fka_ref[...] = ka.astype(BF16)
    qa = _dot((fq * scale).astype(BF16), place_ref[...]) + neg_ref[...]
    fqt_ref[...] = qa.T.astype(BF16)
    ska_ref[...] = _dot(sk.astype(BF16), place_ref[...]).astype(BF16)
    sqt_ref[...] = _dot((sq_ref[...] * scale).astype(BF16), place_ref[...]).T.astype(BF16)


def _prep_prompt(proj3, flog3, gq, gk, bf, seg, place, placec, neg, tm):
    b, t, _ = proj3.shape
    col = lambda s: pl.BlockSpec((None, tm, D_GROUP), lambda bi, ti, s=s: (bi, ti, s))
    const = lambda shp: pl.BlockSpec(shp, lambda bi, ti: (0,) * len(shp))
    tr = lambda rows: pl.BlockSpec((None, rows, tm), lambda bi, ti: (bi, 0, ti))
    nt = lambda cols: pl.BlockSpec((None, tm, cols), lambda bi, ti: (bi, ti, 0))
    sds = jax.ShapeDtypeStruct
    return pl.pallas_call(
        functools.partial(_prep_prompt_kernel, tm=tm),
        out_shape=(
            sds((b, 2 * D_GROUP, t), F32), sds((b, 2 * D_GROUP, t), F32),
            sds((b, N_HEADS, t), F32), sds((b, N_HEADS, t), F32),
            sds((b, D_AUG, t), BF16), sds((b, t, D_AUG), BF16), sds((b, D_GROUP, t), BF16),
            sds((b, D_AUG, t), BF16), sds((b, t, D_AUG), BF16), sds((b, D_GROUP, t), BF16),
        ),
        grid=(b, t // tm),
        in_specs=[col(3), col(4), col(5), col(6), col(7), col(8),
                  pl.BlockSpec((None, tm, LANES), lambda bi, ti: (bi, ti, 0)),
                  const((1, D_GROUP)), const((1, D_GROUP)), const((1, LANES)),
                  const((D_GROUP, D_GROUP)), const((D_GROUP, D_AUG)), const((N_SPLIT, LANES, D_AUG)),
                  const((1, D_AUG))],
        out_specs=(tr(2 * D_GROUP), tr(2 * D_GROUP), tr(N_HEADS), tr(N_HEADS),
                   tr(D_AUG), nt(D_AUG), tr(D_GROUP), tr(D_AUG), nt(D_AUG), tr(D_GROUP)),
        scratch_shapes=[pltpu.VMEM((1, LANES), F32)],
        compiler_params=_cparams(("arbitrary", "arbitrary")),
        name="prep_prompt",
    )(proj3, proj3, proj3, proj3, proj3, proj3, flog3, gq, gk, bf, seg, place, placec, neg)


def _prep_sample_kernel(fq_ref, fk_ref, fv_ref, sq_ref, sk_ref, sv_ref, fl_ref, gq_ref, gk_ref, bf_ref,
                        seg_ref, fkv_ref, skv_ref, lf_ref, c_ref, ct_ref, qf_ref, qs_ref, *, seq, tm):
    scale = HEAD_DIM ** -0.5
    fq = _head_norm(fq_ref[...], gq_ref[...], seg_ref)
    fk = _head_norm(fk_ref[...], gk_ref[...], seg_ref)
    qf_ref[...] = (fq * scale).astype(BF16)
    qs_ref[...] = (sq_ref[...] * scale).astype(BF16)
    fkv_ref[:, 0:D_GROUP] = fk
    fkv_ref[:, D_GROUP:2 * D_GROUP] = fv_ref[...]
    skv_ref[:, 0:D_GROUP] = sk_ref[...]
    skv_ref[:, D_GROUP:2 * D_GROUP] = sv_ref[...]
    lf = _log_sigmoid(fl_ref[...] + bf_ref[...])
    lf_ref[...] = lf
    cs = _running_sum(lf, seq, tm)
    c_ref[...] = cs
    ct_ref[...] = cs.T[0:N_HEADS, :]


def _prep_sample(proj, flog, gq, gk, bf, seg, seq):
    m = proj.shape[0]
    col = lambda s: pl.BlockSpec((m, D_GROUP), lambda i, s=s: (0, s))
    full = lambda shp: pl.BlockSpec(shp, lambda i: (0,) * len(shp))
    sds = jax.ShapeDtypeStruct
    return pl.pallas_call(
        functools.partial(_prep_sample_kernel, seq=seq, tm=m),
        out_shape=(sds((m, 2 * D_GROUP), F32), sds((m, 2 * D_GROUP), F32), sds((m, LANES), F32),
                   sds((m, LANES), F32), sds((N_HEADS, m), F32), sds((m, D_GROUP), BF16),
                   sds((m, D_GROUP), BF16)),
        grid=(1,),
        in_specs=[col(3), col(4), col(5), col(6), col(7), col(8), full((m, LANES)),
                  full((1, D_GROUP)), full((1, D_GROUP)), full((1, LANES)), full((D_GROUP, D_GROUP))],
        out_specs=(full((m, 2 * D_GROUP)), full((m, 2 * D_GROUP)), full((m, LANES)), full((m, LANES)),
                   full((N_HEADS, m)), full((m, D_GROUP)), full((m, D_GROUP))),
        compiler_params=_cparams(("arbitrary",)),
        name="prep_sample",
    )(proj, proj, proj, proj, proj, proj, flog, gq, gk, bf, seg)


def _seqmix_kernel(u_ref, x_ref, gate_ref, pprev_ref, cprev_ref, h0_ref,
                   pw_ref, pscale_ref, cw_ref, cb_ref, wax_ref, ba_ref, bx_ref, lam_ref,
                   yp_ref, yl_ref, pnew_ref, cnew_ref, hnew_ref,
                   extp_ref, extc_ref, h_ref, *, tt, tv, pos0):
    ti = pl.program_id(1)

    @pl.when(ti == 0)
    def _():
        extp_ref[0:HIST_P, :] = pprev_ref[...]
        extc_ref[0:HIST_C, :] = cprev_ref[...]
        h_ref[...] = h0_ref[...]

    extp_ref[HIST_P:HIST_P + tt, :] = u_ref[...]
    extc_ref[HIST_C:HIST_C + tt, :] = x_ref[...]
    pos = pos0 + ti * tt + lax.broadcasted_iota(jnp.int32, (tt, 1), 0)

    for g, w in enumerate(POOL_WINDOWS):
        lanes = slice(g * POOL_CH, (g + 1) * POOL_CH)
        tok = extp_ref[HIST_P:HIST_P + tt, lanes]
        win = tok
        for j in range(1, w):
            win = win + extp_ref[HIST_P - j:HIST_P - j + tt, lanes]
        cnt = jnp.minimum(w, pos + 1).astype(F32)
        d = win / cnt - tok
        y = _dot(d.astype(BF16), pw_ref[g]) * pscale_ref[:, lanes]
        yp_ref[:, lanes] = y
    hist = extp_ref[tv:tv + HIST_P, :]
    extp_ref[0:HIST_P, :] = hist
    pnew_ref[...] = hist

    base = HIST_C - (CONV_W - 1)
    xc = extc_ref[base:base + tt, :] * cw_ref[0:1, :]
    for k in range(1, CONV_W):
        xc = xc + extc_ref[base + k:base + k + tt, :] * cw_ref[k:k + 1, :]
    xc = cb_ref[...] + xc
    chist = extc_ref[tv:tv + HIST_C, :]
    extc_ref[0:HIST_C, :] = chist
    cnew_ref[...] = chist

    ri = _dot(xc.astype(BF16), wax_ref[...])
    r = jax.nn.sigmoid(ri[:, 0:D_GROUP] + ba_ref[...])
    gi = jax.nn.sigmoid(ri[:, D_GROUP:2 * D_GROUP] + bx_ref[...])
    log_a = -LRU_C * r * _softplus(-lam_ref[...])
    a = jnp.exp(log_a)
    mult = jnp.sqrt(-jnp.tanh(log_a) * (a * a + 1.0))
    mult = jnp.where(pos == 0, 1.0, mult)
    b = mult * (gi * xc)

    rows = lax.broadcasted_iota(jnp.int32, (tt, 1), 0)
    d = 1
    while d < tt:
        ok = rows >= d
        a_s = jnp.where(ok, pltpu.roll(a, d, 0), 1.0)
        b_s = jnp.where(ok, pltpu.roll(b, d, 0), 0.0)
        b = a * b_s + b
        a = a * a_s
        d *= 2
    h = b + a * h_ref[...]
    hlast = h[tv - 1:tv, :]
    h_ref[...] = hlast
    hnew_ref[...] = hlast
    yl_ref[...] = jax.nn.gelu(gate_ref[...]) * h


def _seqmix(proj3, pprev, cprev, h0, pw, pscale, cw, cb, wax, ba, bx, lam, tt, tv, pos0):
    b, t, _ = proj3.shape
    nt = t // tt
    col = lambda s: pl.BlockSpec((None, tt, D_GROUP), lambda bi, ti, s=s: (bi, ti, s))
    per_b = lambda r: pl.BlockSpec((None, r, D_GROUP), lambda bi, ti: (bi, 0, 0))
    const2 = lambda shp: pl.BlockSpec(shp, lambda bi, ti: (0, 0))
    return pl.pallas_call(
        functools.partial(_seqmix_kernel, tt=tt, tv=tv, pos0=pos0),
        out_shape=(
            jax.ShapeDtypeStruct((b, t, D_GROUP), F32),
            jax.ShapeDtypeStruct((b, t, D_GROUP), F32),
            jax.ShapeDtypeStruct((b, HIST_P, D_GROUP), F32),
            jax.ShapeDtypeStruct((b, HIST_C, D_GROUP), F32),
            jax.ShapeDtypeStruct((b, 1, D_GROUP), F32),
        ),
        grid=(b, nt),
        in_specs=[col(0), col(1), col(2), per_b(HIST_P), per_b(HIST_C), per_b(1),
                  pl.BlockSpec((len(POOL_WINDOWS), POOL_CH, POOL_CH), lambda bi, ti: (0, 0, 0)),
                  const2((1, D_GROUP)), const2((HIST_C, D_GROUP)), const2((1, D_GROUP)),
                  const2((D_GROUP, 2 * D_GROUP)), const2((1, D_GROUP)), const2((1, D_GROUP)),
                  const2((1, D_GROUP))],
        out_specs=(
            pl.BlockSpec((None, tt, D_GROUP), lambda bi, ti: (bi, ti, 0)),
            pl.BlockSpec((None, tt, D_GROUP), lambda bi, ti: (bi, ti, 0)),
            per_b(HIST_P), per_b(HIST_C), per_b(1),
        ),
        scratch_shapes=[pltpu.VMEM((HIST_P + tt, D_GROUP), F32),
                        pltpu.VMEM((HIST_C + tt, D_GROUP), F32),
                        pltpu.VMEM((1, D_GROUP), F32)],
        compiler_params=_cparams(("arbitrary", "arbitrary")),
        name="seqmix",
    )(proj3, proj3, proj3, pprev, cprev, h0, pw, pscale, cw, cb, wax, ba, bx, lam)


def _pair_tables(n, ratio, reverse):
    qi, ki = [], []
    for q in range(n):
        ks = range((q + 1) * ratio)
        for k in (reversed(ks) if reverse else ks):
            qi.append(q)
            ki.append(k)
    return jnp.asarray(np.array(qi, np.int32)), jnp.asarray(np.array(ki, np.int32))


def _key_minus_query(tk, tq):
    return (lax.broadcasted_iota(jnp.int32, (tk, tq), 0) - lax.broadcasted_iota(jnp.int32, (tk, tq), 1))


def _fox_prompt_kernel(qi_ref, ki_ref, qt_ref, k_ref, vt_ref, cq_ref, o_ref, m_ref, l_ref, acc_ref, *, tq, tk):
    s_id = pl.program_id(1)
    qi = qi_ref[s_id]
    ki = ki_ref[s_id]
    ratio = tq // tk

    @pl.when(ki == 0)
    def _():
        m_ref[...] = jnp.full_like(m_ref, NEG_BIG)
        l_ref[...] = jnp.zeros_like(l_ref)
        acc_ref[...] = jnp.zeros_like(acc_ref)

    def step(masked):
        if masked:
            causal = _key_minus_query(tk, tq) <= qi * tq - ki * tk

        def scores(h):
            slot = slice(h * LANES, (h + 1) * LANES)
            return _dot(k_ref[:, slot], qt_ref[slot, :])

        def softmax_update(h, st):
            if masked:
                st = jnp.where(causal, st, NEG_BIG)
            cq = cq_ref[h:h + 1, :]
            m_old = m_ref[h:h + 1, :]
            m_new = jnp.maximum(m_old, jnp.max(st, axis=0, keepdims=True) + cq)
            alpha = jnp.exp(m_old - m_new)
            pt = jnp.exp(st - (m_new - cq))
            l_ref[h:h + 1, :] = alpha * l_ref[h:h + 1, :] + jnp.sum(pt, axis=0, keepdims=True)
            m_ref[h:h + 1, :] = m_new
            return alpha, pt.astype(BF16)

        st = {0: scores(0), 1: scores(1)}
        for h in range(N_HEADS):
            rows = slice(h * HEAD_DIM, (h + 1) * HEAD_DIM)
            if h + 2 < N_HEADS:
                st[h + 2] = scores(h + 2)
            alpha, pt = softmax_update(h, st.pop(h))
            acc_ref[rows, :] = alpha * acc_ref[rows, :] + _dot(vt_ref[rows, :], pt)

    @pl.when(ki < qi * ratio)
    def _():
        step(False)

    @pl.when(ki >= qi * ratio)
    def _():
        step(True)

    @pl.when(ki == (qi + 1) * ratio - 1)
    def _():
        for h in range(N_HEADS):
            rows = slice(h * HEAD_DIM, (h + 1) * HEAD_DIM)
            acc_ref[rows, :] = acc_ref[rows, :] / l_ref[h:h + 1, :]
        o_ref[...] = acc_ref[...].T


def _sb_prompt_kernel(qi_ref, ki_ref, qt_ref, k_ref, vt_ref, tri_ref, o_ref, carry_ref, acc_ref, *, tq, tk):
    s_id = pl.program_id(1)
    qi = qi_ref[s_id]
    ki = ki_ref[s_id]
    ratio = tq // tk

    @pl.when(ki == (qi + 1) * ratio - 1)
    def _():
        carry_ref[...] = jnp.zeros_like(carry_ref)
        acc_ref[...] = jnp.zeros_like(acc_ref)

    def step(masked):
        if masked:
            strict = _key_minus_query(tk, tq) < qi * tq - ki * tk

        def logits(h):
            slot = slice(h * LANES, (h + 1) * LANES)
            return _dot(k_ref[:, slot], qt_ref[slot, :])

        def neg_log_one_minus_beta(zt):
            sp = _softplus_exp2(zt)
            if masked:
                sp = jnp.where(strict, sp, 0.0)
            return (sp,) + tuple(_split_bf16(sp, 2))

        def later_keys_sum(hi, lo):
            return _dot(tri_ref[...], hi) + _dot(tri_ref[...], lo)

        def weights(zt, sp, rest):
            wgt = jnp.exp(zt - sp - rest)
            if masked:
                wgt = jnp.where(strict, wgt, 0.0)
            return wgt.astype(BF16)

        zt = {0: logits(0), 1: logits(1)}
        lg = {0: neg_log_one_minus_beta(zt[0])}
        rest = {}
        for h in range(N_HEADS + 1):
            if h < N_HEADS:
                rest[h] = later_keys_sum(*lg[h][1:])
            if h + 2 < N_HEADS:
                zt[h + 2] = logits(h + 2)
            if h >= 1:
                g = h - 1
                rows = slice(g * HEAD_DIM, (g + 1) * HEAD_DIM)
                sp, rs = lg.pop(g)[0], rest.pop(g)
                carry = carry_ref[g:g + 1, :]
                pv = _dot(vt_ref[rows, :], weights(zt.pop(g), sp, rs))
                acc_ref[rows, :] = acc_ref[rows, :] + pv * jnp.exp(-carry)
                carry_ref[g:g + 1, :] = carry + rs[0:1, :] + sp[0:1, :]
            if h + 1 < N_HEADS:
                lg[h + 1] = neg_log_one_minus_beta(zt[h + 1])

    @pl.when(ki >= qi * ratio)
    def _():
        step(True)

    @pl.when(ki < qi * ratio)
    def _():
        step(False)

    @pl.when(ki == 0)
    def _():
        o_ref[...] = acc_ref[...].T


def _prompt_attention(kind, qt, k, vt, extra, tq, tk):
    b, t, _ = k.shape
    nq = t // tq
    qi, ki = _pair_tables(nq, tq // tk, reverse=(kind == "sb"))
    in_specs = [pl.BlockSpec((None, D_AUG, tq), lambda bi, s, qi, ki: (bi, 0, qi[s])),
                pl.BlockSpec((None, tk, D_AUG), lambda bi, s, qi, ki: (bi, ki[s], 0)),
                pl.BlockSpec((None, D_GROUP, tk), lambda bi, s, qi, ki: (bi, 0, ki[s]))]
    if kind == "fox":
        in_specs.append(pl.BlockSpec((None, N_HEADS, tq), lambda bi, s, qi, ki: (bi, 0, qi[s])))
        body = _fox_prompt_kernel
        scratch = [pltpu.VMEM((N_HEADS, tq), F32), pltpu.VMEM((N_HEADS, tq), F32),
                   pltpu.VMEM((D_GROUP, tq), F32)]
    else:
        in_specs.append(pl.BlockSpec((tk, tk), lambda bi, s, qi, ki: (0, 0)))
        body = _sb_prompt_kernel
        scratch = [pltpu.VMEM((N_HEADS, tq), F32), pltpu.VMEM((D_GROUP, tq), F32)]
    grid_spec = pltpu.PrefetchScalarGridSpec(
        num_scalar_prefetch=2,
        grid=(b, int(qi.shape[0])),
        in_specs=in_specs,
        out_specs=pl.BlockSpec((None, tq, D_GROUP), lambda bi, s, qi, ki: (bi, qi[s], 0)),
        scratch_shapes=scratch,
    )
    return pl.pallas_call(
        functools.partial(body, tq=tq, tk=tk),
        out_shape=jax.ShapeDtypeStruct((b, t, D_GROUP), F32),
        grid_spec=grid_spec,
        compiler_params=_cparams(("arbitrary", "arbitrary")),
        name=kind + "_prompt",
    )(qi, ki, qt, k, vt, extra)


def _extract_heads(acc, t_new):
    hrow = lax.broadcasted_iota(jnp.int32, (N_HEADS, D_GROUP), 0)
    hcol = lax.broadcasted_iota(jnp.int32, (N_HEADS, D_GROUP), 1) // HEAD_DIM
    own = hrow == hcol
    outs = []
    for t in range(t_new):
        blk = acc[t * N_HEADS:(t + 1) * N_HEADS, :]
        outs.append(jnp.sum(jnp.where(own, blk, 0.0), axis=0, keepdims=True))
    return outs


def _fox_sample_stages(wq_ref, new_ref, gt_ref, grow_ref, tri_ref, cache_refs, lft_refs,
                       o_ref, m_ref, l_ref, acc_ref, dcar_ref, t_new):
    rows = t_new * N_HEADS

    def scores(kv_refs, biases, mask):
        s = []
        for kv_ref, bias in zip(kv_refs, biases):
            kt = kv_ref[0:D_GROUP, :].astype(BF16)
            s.append(_dot(wq_ref[...], kt) + jnp.concatenate([bias] * t_new, axis=0) + grow_ref[...])
        s = s[0] if len(s) == 1 else jnp.concatenate(s, axis=1)
        if mask is not None:
            s = jnp.where(mask, s, NEG_BIG)
        return s

    def attend(kv_refs, s):
        m_old = m_ref[...]
        m_new = jnp.maximum(m_old, jnp.max(s, axis=-1, keepdims=True))
        alpha = jnp.exp(m_old - m_new)
        pr = jnp.exp(s - m_new)
        l_ref[...] = alpha * l_ref[...] + jnp.sum(pr, axis=-1, keepdims=True)
        pv = None
        for i, kv_ref in enumerate(kv_refs):
            vt = kv_ref[D_GROUP:2 * D_GROUP, :].astype(BF16)
            t = _dot_nt(pr[:, i * PAGE:(i + 1) * PAGE].astype(BF16), vt)
            pv = t if pv is None else pv + t
        acc_ref[...] = alpha * acc_ref[...] + pv
        m_ref[...] = m_new

    def begin(first):
        @pl.when(first)
        def _():
            m_ref[...] = jnp.full_like(m_ref, NEG_BIG)
            l_ref[...] = jnp.zeros_like(l_ref)
            acc_ref[...] = jnp.zeros_like(acc_ref)
            dcar_ref[...] = jnp.zeros_like(dcar_ref)
            tok = lax.broadcasted_iota(jnp.int32, (rows, PAGE), 0) // N_HEADS
            key = lax.broadcasted_iota(jnp.int32, (rows, PAGE), 1)
            attend([new_ref], scores([new_ref], [-gt_ref[...]], key <= tok))

    def past_scores():
        lfs = [r[...] for r in lft_refs]
        parts = []
        for lf in lfs:
            parts += [x.astype(F32) for x in _split_bf16(lf, N_SPLIT)]
        r = _dot(jnp.concatenate(parts, axis=0).astype(BF16), tri_ref[...])
        run = dcar_ref[...]
        biases = []
        for i, lf in enumerate(lfs):
            base = i * N_SPLIT * N_HEADS
            later = r[base:base + N_HEADS]
            for j in range(1, N_SPLIT):
                later = later + r[base + j * N_HEADS:base + (j + 1) * N_HEADS]
            biases.append(later + run)
            run = run + later[:, 0:1] + lf[:, 0:1]
        dcar_ref[...] = run
        return scores(cache_refs, biases, None)

    def past_attend(s):
        attend(cache_refs, s)

    def finish(last):
        @pl.when(last)
        def _():
            out = acc_ref[...] / l_ref[...]
            for t, rowv in enumerate(_extract_heads(out, t_new)):
                o_ref[t:t + 1, :] = rowv

    return begin, past_scores, past_attend, finish


def _sb_sample_stages(wq_ref, new_ref, tri_ref, cache_refs, o_ref, acc_ref, car_ref, t_new):
    rows = t_new * N_HEADS

    def logits(kv_refs, mask):
        zs, l1ms, parts = [], [], []
        for kv_ref in kv_refs:
            kt = kv_ref[0:D_GROUP, :].astype(BF16)
            z = _dot(wq_ref[...], kt)
            l1m = _neg_softplus(z)
            if mask is not None:
                l1m = jnp.where(mask, l1m, 0.0)
            zs.append(z)
            l1ms.append(l1m)
            parts += [x.astype(F32) for x in _split_bf16(l1m, 2)]
        return zs, l1ms, jnp.concatenate(parts, axis=0).astype(BF16)

    def later_sums(stacked):
        return _dot(stacked, tri_ref[...])

    def attend(kv_refs, zs, l1ms, r, mask):
        run = car_ref[...]
        pv = None
        for i, kv_ref in enumerate(kv_refs):
            rest_i = r[2 * i * rows:(2 * i + 1) * rows] + r[(2 * i + 1) * rows:(2 * i + 2) * rows]
            wgt = jnp.exp(zs[i] + l1ms[i] + rest_i + run)
            if mask is not None:
                wgt = jnp.where(mask, wgt, 0.0)
            run = run + rest_i[:, 0:1] + l1ms[i][:, 0:1]
            vt = kv_ref[D_GROUP:2 * D_GROUP, :].astype(BF16)
            t = _dot_nt(wgt.astype(BF16), vt)
            pv = t if pv is None else pv + t
        car_ref[...] = run
        acc_ref[...] = acc_ref[...] + pv

    def begin(first):
        @pl.when(first)
        def _():
            acc_ref[...] = jnp.zeros_like(acc_ref)
            car_ref[...] = jnp.zeros_like(car_ref)
            tok = lax.broadcasted_iota(jnp.int32, (rows, PAGE), 0) // N_HEADS
            key = lax.broadcasted_iota(jnp.int32, (rows, PAGE), 1)
            mask = key < tok
            zs, l1ms, stacked = logits([new_ref], mask)
            attend([new_ref], zs, l1ms, later_sums(stacked), mask)

    def past_logits():
        return logits(cache_refs, None)

    def past_attend(zs, l1ms, r):
        attend(cache_refs, zs, l1ms, r, None)

    def finish(last):
        @pl.when(last)
        def _():
            for t, rowv in enumerate(_extract_heads(acc_ref[...], t_new)):
                o_ref[t:t + 1, :] = rowv

    return begin, past_logits, later_sums, past_attend, finish


def _outproj_kernel(x_ref, y0_ref, y1_ref, y2_ref, y3_ref, g_ref, w_ref, o_ref):
    acc = x_ref[...]
    for gi, y_ref in enumerate((y0_ref, y1_ref, y2_ref, y3_ref)):
        y = y_ref[...]
        ms = jnp.mean(y * y, axis=-1, keepdims=True)
        yn = (y * lax.rsqrt(ms + EPS) * g_ref[gi:gi + 1, :]).astype(BF16)
        acc = acc + _dot(yn, w_ref[gi * D_GROUP:(gi + 1) * D_GROUP, :])
    o_ref[...] = acc


def _outproj(x, ys, g, w, tm):
    m, d = x.shape
    yspec = pl.BlockSpec((tm, D_GROUP), lambda i: (i, 0))
    return pl.pallas_call(
        _outproj_kernel,
        out_shape=jax.ShapeDtypeStruct((m, d), F32),
        grid=(m // tm,),
        in_specs=[pl.BlockSpec((tm, d), lambda i: (i, 0)), yspec, yspec, yspec, yspec,
                  pl.BlockSpec((4, D_GROUP), lambda i: (0, 0)),
                  pl.BlockSpec((4 * D_GROUP, d), lambda i: (0, 0))],
        out_specs=pl.BlockSpec((tm, d), lambda i: (i, 0)),
        compiler_params=_cparams(("arbitrary",)),
        name="outproj",
    )(x, *ys, g, w)


def _mlp_kernel(x_ref, g_ref, wu_ref, wd_ref, o_ref, xn_ref):
    @pl.when(pl.program_id(1) == 0)
    def _():
        x = x_ref[...]
        ms = jnp.mean(x * x, axis=-1, keepdims=True)
        xn_ref[...] = (x * lax.rsqrt(ms + EPS) * g_ref[...]).astype(BF16)
        o_ref[...] = x

    hid = jnp.maximum(_dot(xn_ref[...], wu_ref[...]), 0.0)
    o_ref[...] += _dot((hid * hid).astype(BF16), wd_ref[...])


def _mlp(x, g, wu, wd, tm, tf):
    m, d = x.shape
    f = wu.shape[1]
    return pl.pallas_call(
        _mlp_kernel,
        out_shape=jax.ShapeDtypeStruct((m, d), F32),
        grid=(m // tm, f // tf),
        in_specs=[pl.BlockSpec((tm, d), lambda i, j: (i, 0)),
                  pl.BlockSpec((1, d), lambda i, j: (0, 0)),
                  pl.BlockSpec((d, tf), lambda i, j: (0, j)),
                  pl.BlockSpec((tf, d), lambda i, j: (j, 0))],
        out_specs=pl.BlockSpec((tm, d), lambda i, j: (i, 0)),
        scratch_shapes=[pltpu.VMEM((tm, d), BF16)],
        compiler_params=_cparams(("arbitrary", "arbitrary"), 56),
        name="mlp",
    )(x, g, wu, wd)


def _mlp_attn_kernel(pt_ref, x_ref, g_ref, wu_ref, wd_ref, fwq_ref, fnew_ref, gt_ref, grow_ref, tri_ref,
                     swq_ref, snew_ref, *rest, pages, groups, t_new):
    fcache = rest[0:pages]
    lft = rest[pages:2 * pages]
    scache = rest[2 * pages:3 * pages]
    (o_ref, of_ref, os_ref, xn_ref, fm_ref, fl_ref, facc_ref, fdcar_ref, sacc_ref,
     scar_ref) = rest[3 * pages:]
    j = pl.program_id(1)
    grp = lax.rem(pl.program_id(0) * pl.num_programs(1) + j, groups)
    first = grp == 0
    last = grp == groups - 1

    @pl.when(j == 0)
    def _():
        x = x_ref[...]
        ms = jnp.mean(x * x, axis=-1, keepdims=True)
        xn_ref[...] = (x * lax.rsqrt(ms + EPS) * g_ref[...]).astype(BF16)
        o_ref[...] = x

    f_begin, f_scores, f_attend, f_finish = _fox_sample_stages(
        fwq_ref, fnew_ref, gt_ref, grow_ref, tri_ref, fcache, lft, of_ref, fm_ref, fl_ref, facc_ref,
        fdcar_ref, t_new)
    s_begin, s_logits, s_later, s_attend, s_finish = _sb_sample_stages(
        swq_ref, snew_ref, tri_ref, scache, os_ref, sacc_ref, scar_ref, t_new)
    f_begin(first)
    s_begin(first)

    fs = f_scores()
    zs, l1ms, stacked = s_logits()
    hid = jnp.maximum(_dot(xn_ref[...], wu_ref[...]), 0.0)
    sr = s_later(stacked)
    f_attend(fs)
    o_ref[...] += _dot((hid * hid).astype(BF16), wd_ref[...])
    s_attend(zs, l1ms, sr)

    f_finish(last)
    s_finish(last)


def _mlp_attn(x, g, wu, wd, tm, tf, pt, fox, sb, tri, layer, t_new):
    m, d = x.shape
    f = wu.shape[1]
    fwq, fnew, gt, grow, fcache, lft_cache = fox
    swq, snew, scache = sb
    bs = fwq.shape[0]
    n_pages = pt.shape[0] // bs
    steps = (m // tm) * (f // tf)
    groups = steps // bs
    pages = n_pages // groups
    assert groups * bs == steps and pages * groups == n_pages, (steps, bs, n_pages)
    rows = t_new * N_HEADS
    nj = f // tf

    def seq_of(i, j):
        return lax.div(i * nj + j, groups)

    def page_spec(block, cache_layer, slot):
        def idx(i, j, pt):
            newest_first = lax.rem(i * nj + j, groups) * pages + slot
            return (cache_layer, pt[seq_of(i, j) * n_pages + (n_pages - 1 - newest_first)], 0, 0)
        return pl.BlockSpec((None, None) + block, idx)

    per_seq = lambda r, c: pl.BlockSpec((None, r, c), lambda i, j, pt: (seq_of(i, j), 0, 0))
    const = lambda shp: pl.BlockSpec(shp, lambda i, j, pt: (0,) * len(shp))
    in_specs = ([pl.BlockSpec((tm, d), lambda i, j, pt: (i, 0), pipeline_mode=pl.Buffered(1)), const((1, d)),
                 pl.BlockSpec((d, tf), lambda i, j, pt: (0, j)),
                 pl.BlockSpec((tf, d), lambda i, j, pt: (j, 0)),
                 per_seq(rows, D_GROUP), per_seq(2 * D_GROUP, PAGE), per_seq(N_HEADS, PAGE), per_seq(rows, 1),
                 const((PAGE, PAGE)), per_seq(rows, D_GROUP), per_seq(2 * D_GROUP, PAGE)]
                + [page_spec((2 * D_GROUP, PAGE), layer, s) for s in range(pages)]
                + [page_spec((N_HEADS, PAGE), layer, s) for s in range(pages)]
                + [page_spec((2 * D_GROUP, PAGE), layer, s) for s in range(pages)])
    grid_spec = pltpu.PrefetchScalarGridSpec(
        num_scalar_prefetch=1,
        grid=(m // tm, nj),
        in_specs=in_specs,
        out_specs=(pl.BlockSpec((tm, d), lambda i, j, pt: (i, 0), pipeline_mode=pl.Buffered(1)),
                   per_seq(t_new, D_GROUP), per_seq(t_new, D_GROUP)),
        scratch_shapes=[pltpu.VMEM((tm, d), BF16),
                        pltpu.VMEM((rows, 1), F32), pltpu.VMEM((rows, 1), F32),
                        pltpu.VMEM((rows, D_GROUP), F32), pltpu.VMEM((N_HEADS, 1), F32),
                        pltpu.VMEM((rows, D_GROUP), F32), pltpu.VMEM((rows, 1), F32)],
    )
    sds = jax.ShapeDtypeStruct
    return pl.pallas_call(
        functools.partial(_mlp_attn_kernel, pages=pages, groups=groups, t_new=t_new),
        out_shape=(sds((m, d), F32), sds((bs, t_new, D_GROUP), F32), sds((bs, t_new, D_GROUP), F32)),
        grid_spec=grid_spec,
        compiler_params=_cparams(("arbitrary", "arbitrary"), 60),
        name="mlp_attn",
    )(pt, x, g, wu, wd, fwq, fnew, gt, grow, tri, swq, snew,
      *((fcache,) * pages + (lft_cache,) * pages + (scache,) * pages))


def _block_diag(w):
    n, d, _ = w.shape
    eye = jnp.eye(n, dtype=w.dtype)
    return (eye[:, None, :, None] * w[:, :, None, :]).reshape(n * d, n * d)


def _layer_weights(l, norm1_g, w_in, pool_w, pool_scale, conv_w, conv_b, lru_wa, lru_ba, lru_wx, lru_bx,
                   lru_lambda, fox_bf, fox_q_g, fox_k_g, out_g, w_out, norm2_g, w_up, w_down):
    g = D_GROUP
    wi = w_in[l]
    nf = 6 * g
    w_main = jnp.concatenate([wi[:, :nf], wi[:, nf + N_HEADS:]], axis=1).astype(BF16)
    w_f = jnp.pad(wi[:, nf:nf + N_HEADS], ((0, 0), (0, LANES - N_HEADS))).astype(BF16)
    return dict(
        norm1_g=norm1_g[l][None, :], w_main=w_main, w_f=w_f,
        pool_w=pool_w[l].astype(BF16), pool_scale=pool_scale[l][None, :],
        conv_w=jnp.pad(conv_w[l], ((0, HIST_C - CONV_W), (0, 0))), conv_b=conv_b[l][None, :],
        wax=jnp.concatenate([_block_diag(lru_wa[l]), _block_diag(lru_wx[l])], axis=1).astype(BF16),
        ba=lru_ba[l][None, :], bx=lru_bx[l][None, :], lam=lru_lambda[l][None, :],
        bf=jnp.pad(fox_bf[l], (0, LANES - N_HEADS))[None, :],
        gq=jnp.tile(fox_q_g[l], N_HEADS)[None, :], gk=jnp.tile(fox_k_g[l], N_HEADS)[None, :],
        out_g=out_g[l].reshape(4, g), w_out=w_out[l].astype(BF16),
        norm2_g=norm2_g[l][None, :], w_up=w_up[l].astype(BF16), w_down=w_down[l].astype(BF16),
    )


def _tile(m, pref):
    t = min(m, pref)
    while m % t:
        t //= 2
    return t


def _slot_constants():
    place = np.zeros((D_GROUP, D_AUG), np.float32)
    for c in range(D_GROUP):
        place[c, (c // HEAD_DIM) * LANES + c % HEAD_DIM] = 1.0
    placec = np.zeros((N_SPLIT, LANES, D_AUG), np.float32)
    neg = np.zeros((1, D_AUG), np.float32)
    for i in range(N_SPLIT):
        for h in range(N_HEADS):
            placec[i, h, h * LANES + HEAD_DIM + i] = 1.0
            neg[0, h * LANES + HEAD_DIM + i] = -1.0
    return jnp.asarray(place, BF16), jnp.asarray(placec, BF16), jnp.asarray(neg, F32)


def _expand_queries(q, b, t_new):
    q4 = q.reshape(b, t_new, 1, D_GROUP)
    own = (jnp.arange(D_GROUP)[None, :] // HEAD_DIM) == jnp.arange(N_HEADS)[:, None]
    return jnp.where(own[None, None], q4, jnp.zeros_like(q4)).reshape(b, t_new * N_HEADS, D_GROUP)


def _pages_view(cache):
    d, n = cache.shape[0], cache.shape[1]
    return jnp.transpose(cache, (0, 1, 3, 4, 5, 2)).reshape(d, n, 2 * D_GROUP, PAGE)


def kernel(x_prompt, x_sample, cache_fox_kv, cache_fox_logf, cache_sb_kv, state_pool, state_conv, state_lru,
           page_table, norm1_g, w_in, pool_w, pool_scale, conv_w, conv_b, lru_wa, lru_ba, lru_wx, lru_bx,
           lru_lambda, fox_bf, fox_q_g, fox_k_g, out_g, w_out, norm2_g, w_up, w_down):
    depth = w_in.shape[0]
    bp, seq, d_model = x_prompt.shape
    bs, t_new, _ = x_sample.shape
    past_len = page_table.shape[1] * PAGE
    g = D_GROUP

    seg = jnp.asarray(np.kron(np.eye(N_HEADS), np.ones((HEAD_DIM, HEAD_DIM))), BF16)
    place, placec, neg = _slot_constants()
    tk = _tile(seq, 256)
    tq = _tile(seq, 512)
    tri_q = jnp.asarray(np.triu(np.ones((tk, tk)), 1), BF16)
    tri_p = jnp.asarray(np.tril(np.ones((PAGE, PAGE)), -1), BF16)
    pt_flat = page_table.reshape(-1).astype(jnp.int32)
    fox_cache = _pages_view(cache_fox_kv)
    sb_cache = _pages_view(cache_sb_kv)
    lft_cache = jnp.swapaxes(cache_fox_logf, 2, 3)
    t_pad = 8

    xp = x_prompt.reshape(bp * seq, d_model)
    xs = x_sample.reshape(bs * t_new, d_model)
    st_p, st_s = [], []
    for l in range(depth):
        w = _layer_weights(l, norm1_g, w_in, pool_w, pool_scale, conv_w, conv_b, lru_wa, lru_ba, lru_wx,
                           lru_bx, lru_lambda, fox_bf, fox_q_g, fox_k_g, out_g, w_out, norm2_g, w_up, w_down)

        def dense_tail(x, ys):
            rows = x.shape[0]
            x1 = _outproj(x, ys, w["out_g"], w["w_out"], _tile(rows, 512))
            return _mlp(x1, w["norm2_g"], w["w_up"], w["w_down"], _tile(rows, 1024),
                        _tile(w["w_up"].shape[1], 512))

        m = bp * seq
        proj, flog = _inproj(xp, w["norm1_g"], w["w_main"], w["w_f"], _tile(m, 1024))
        r3 = lambda a: a.reshape(bp, seq, a.shape[-1])
        fkvt, skvt, lft, ct, fqt, fka, fvt, sqt, ska, svt = _prep_prompt(
            r3(proj), r3(flog), w["gq"], w["gk"], w["bf"], seg, place, placec, neg, _tile(seq, 256))
        y_fox = _prompt_attention("fox", fqt, fka, fvt, ct, tq, tk)
        y_sb = _prompt_attention("sb", sqt, ska, svt, tri_q, tq, tk)
        tt = _tile(seq, 256)
        y_pool, y_lru, pnew, cnew, hnew = _seqmix(
            r3(proj), jnp.zeros((bp, HIST_P, g), F32), jnp.zeros((bp, HIST_C, g), F32),
            jnp.zeros((bp, 1, g), F32), w["pool_w"], w["pool_scale"], w["conv_w"], w["conv_b"], w["wax"],
            w["ba"], w["bx"], w["lam"], tt, tt, 0)
        flat = lambda a: a.reshape(m, g)
        x1p = _outproj(xp, (flat(y_pool), flat(y_lru), flat(y_fox), flat(y_sb)), w["out_g"], w["w_out"],
                       _tile(m, 512))
        kv_state = lambda a: jnp.transpose(a.reshape(bp, 2, N_HEADS, HEAD_DIM, seq), (0, 4, 1, 2, 3))
        st_p.append((kv_state(fkvt), jnp.swapaxes(lft, 1, 2), kv_state(skvt), pnew[:, 1:],
                     cnew[:, HIST_C - CONV_W + 1:], hnew[:, 0]))

        ms = bs * t_new
        proj, flog = _inproj(xs, w["norm1_g"], w["w_main"], w["w_f"], ms)
        fkv, skv, lf, c, ct, qf, qs = _prep_sample(proj, flog, w["gq"], w["gk"], w["bf"], seg, t_new)
        new_page = lambda a: jnp.pad(jnp.swapaxes(a.reshape(bs, t_new, 2 * g), 1, 2),
                                     ((0, 0), (0, 0), (0, PAGE - t_new)))
        gt = jnp.pad(jnp.swapaxes(ct.reshape(N_HEADS, bs, t_new), 0, 1), ((0, 0), (0, 0), (0, PAGE - t_new)))
        grow = c[:, :N_HEADS].reshape(bs, t_new * N_HEADS, 1)
        xp, y_fox, y_sb = _mlp_attn(
            x1p, w["norm2_g"], w["w_up"], w["w_down"], _tile(m, 1024), _tile(w["w_up"].shape[1], 512), pt_flat,
            (_expand_queries(qf, bs, t_new), new_page(fkv), gt, grow, fox_cache, lft_cache),
            (_expand_queries(qs, bs, t_new), new_page(skv), sb_cache), tri_p, l, t_new)
        proj3 = jnp.pad(proj.reshape(bs, t_new, -1), ((0, 0), (0, t_pad - t_new), (0, 0)))
        y_pool, y_lru, pnew, cnew, hnew = _seqmix(
            proj3, jnp.pad(state_pool[l], ((0, 0), (HIST_P - POOL_MAX + 1, 0), (0, 0))),
            jnp.pad(state_conv[l], ((0, 0), (HIST_C - CONV_W + 1, 0), (0, 0))), state_lru[l][:, None, :],
            w["pool_w"], w["pool_scale"], w["conv_w"], w["conv_b"], w["wax"], w["ba"], w["bx"], w["lam"],
            t_pad, t_new, past_len)
        cut = lambda a: a[:, :t_new].reshape(ms, g)
        xs = dense_tail(xs, (cut(y_pool), cut(y_lru), y_fox.reshape(ms, g), y_sb.reshape(ms, g)))
        st_s.append((fkv.reshape(bs, t_new, 2, N_HEADS, HEAD_DIM), lf[:, :N_HEADS].reshape(bs, t_new, N_HEADS),
                     skv.reshape(bs, t_new, 2, N_HEADS, HEAD_DIM), pnew[:, 1:], cnew[:, HIST_C - CONV_W + 1:],
                     hnew[:, 0]))

    stk = lambda sts, j: jnp.stack([s[j] for s in sts], axis=0)
    return ((xp.reshape(bp, seq, d_model), xs.reshape(bs, t_new, d_model))
            + tuple(stk(st_p, j) for j in range(6)) + tuple(stk(st_s, j) for j in range(6)))
```

```python
import functools

import numpy as np
import jax
import jax.numpy as jnp
from jax import lax
from jax.experimental import pallas as pl
from jax.experimental.pallas import tpu as pltpu

F32 = jnp.float32
BF16 = jnp.bfloat16

D_GROUP = 512
HEAD_DIM = 64
N_HEADS = D_GROUP // HEAD_DIM
POOL_WINDOWS = (2, 4, 8, 16)
POOL_MAX = max(POOL_WINDOWS)
POOL_CH = D_GROUP // len(POOL_WINDOWS)
CONV_W = 4
LRU_C = 8.0
EPS = 1e-6
PAGE = 128
LANES = 128
D_AUG = N_HEADS * LANES
N_SPLIT = 3
HIST_P = 16
HIST_C = 8
NEG_LOG2E = -1.4426950408889634
NEG_BIG = -1e30
MIB = 1024 * 1024


def _cparams(sem, vmem_mib=48):
    return pltpu.CompilerParams(dimension_semantics=sem, vmem_limit_bytes=vmem_mib * MIB)


def _split_bf16(x, parts):
    out = []
    r = x
    for _ in range(parts - 1):
        h = r.astype(BF16)
        out.append(h)
        r = r - h.astype(F32)
    out.append(r.astype(BF16))
    return out


def _softplus(x):
    return jnp.maximum(x, 0.0) + jnp.log1p(jnp.exp(-jnp.abs(x)))


def _log_sigmoid(x):
    return -_softplus(-x)


def _softplus_exp2(z):
    return jnp.maximum(z, 0.0) + jnp.log(1.0 + jnp.exp2(jnp.abs(z) * NEG_LOG2E))


def _neg_softplus(z):
    return -(jnp.maximum(z, 0.0) + jnp.log(1.0 + jnp.exp(-jnp.abs(z))))


def _dot_nt(a, b):
    return lax.dot_general(a, b, (((1,), (1,)), ((), ())), preferred_element_type=F32)


def _dot(a, b):
    return jnp.dot(a, b, preferred_element_type=F32)


def _inproj_kernel(x_ref, g_ref, w_ref, wf_ref, o_ref, of_ref, xn_ref):
    @pl.when(pl.program_id(1) == 0)
    def _():
        x = x_ref[...]
        ms = jnp.mean(x * x, axis=-1, keepdims=True)
        xn = (x * lax.rsqrt(ms + EPS) * g_ref[...]).astype(BF16)
        xn_ref[...] = xn
        of_ref[...] = _dot(xn, wf_ref[...])

    o_ref[...] = _dot(xn_ref[...], w_ref[...])


def _inproj(x, g, w, wf, tm):
    m, d = x.shape
    n = w.shape[1]
    tn = D_GROUP
    return pl.pallas_call(
        _inproj_kernel,
        out_shape=(jax.ShapeDtypeStruct((m, n), F32), jax.ShapeDtypeStruct((m, LANES), F32)),
        grid=(m // tm, n // tn),
        in_specs=[
            pl.BlockSpec((tm, d), lambda i, j: (i, 0)),
            pl.BlockSpec((1, d), lambda i, j: (0, 0)),
            pl.BlockSpec((d, tn), lambda i, j: (0, j)),
            pl.BlockSpec((d, LANES), lambda i, j: (0, 0)),
        ],
        out_specs=(
            pl.BlockSpec((tm, tn), lambda i, j: (i, j)),
            pl.BlockSpec((tm, LANES), lambda i, j: (i, 0)),
        ),
        scratch_shapes=[pltpu.VMEM((tm, d), BF16)],
        compiler_params=_cparams(("arbitrary", "arbitrary")),
        name="inproj",
    )(x, g, w, wf)


def _head_norm(x, g, seg_ref):
    hi, lo = _split_bf16(x * x, 2)
    ss = _dot(hi, seg_ref[...]) + _dot(lo, seg_ref[...])
    return x * lax.rsqrt(ss * (1.0 / HEAD_DIM) + EPS) * g


def _running_sum(lf, seq, tm):
    r = lax.broadcasted_iota(jnp.int32, (tm, tm), 0)
    c = lax.broadcasted_iota(jnp.int32, (tm, tm), 1)
    keep = c <= r
    if seq < tm:
        keep = jnp.logical_and(keep, (r // seq) == (c // seq))
    tri = jnp.where(keep, 1.0, 0.0).astype(BF16)
    cs = None
    for part in _split_bf16(lf, N_SPLIT):
        t = _dot(tri, part)
        cs = t if cs is None else cs + t
    return cs


def _prep_prompt_kernel(fq_ref, fk_ref, fv_ref, sq_ref, sk_ref, sv_ref, fl_ref, gq_ref, gk_ref, bf_ref,
                        seg_ref, place_ref, placec_ref, neg_ref,
                        fkvt_ref, skvt_ref, lft_ref, ct_ref,
                        fqt_ref, fka_ref, fvt_ref, sqt_ref, ska_ref, svt_ref, carry_ref, *, tm):
    scale = HEAD_DIM ** -0.5
    fq = _head_norm(fq_ref[...], gq_ref[...], seg_ref)
    fk = _head_norm(fk_ref[...], gk_ref[...], seg_ref)
    fv = fv_ref[...]
    sk = sk_ref[...]
    sv = sv_ref[...]

    fvt = fv.T
    svt = sv.T
    fkvt_ref[0:D_GROUP, :] = fk.T
    fkvt_ref[D_GROUP:2 * D_GROUP, :] = fvt
    skvt_ref[0:D_GROUP, :] = sk.T
    skvt_ref[D_GROUP:2 * D_GROUP, :] = svt
    fvt_ref[...] = fvt.astype(BF16)
    svt_ref[...] = svt.astype(BF16)

    lf = _log_sigmoid(fl_ref[...] + bf_ref[...])
    cs = _running_sum(lf, tm, tm)

    @pl.when(pl.program_id(1) == 0)
    def _():
        carry_ref[...] = jnp.zeros_like(carry_ref)
    cs = cs + carry_ref[...]
    carry_ref[...] = cs[tm - 1:tm, :]
    lft_ref[...] = lf.T[0:N_HEADS, :]
    ct_ref[...] = cs.T[0:N_HEADS, :]

    ka = _dot(fk.astype(BF16), place_ref[...])
    for i, part in enumerate(_split_bf16(cs, N_SPLIT)):
        ka = ka + _dot(part, placec_ref[i])
    fka_ref[...] = ka.astype(BF16)
    qa = _dot((fq * scale).astype(BF16), place_ref[...]) + neg_ref[...]
    fqt_ref[...] = qa.T.astype(BF16)
    ska_ref[...] = sk.astype(BF16)
    sqt_ref[...] = (sq_ref[...] * scale).T.astype(BF16)


def _prep_prompt(proj3, flog3, gq, gk, bf, seg, place, placec, neg, tm):
    b, t, _ = proj3.shape
    col = lambda s: pl.BlockSpec((None, tm, D_GROUP), lambda bi, ti, s=s: (bi, ti, s))
    const = lambda shp: pl.BlockSpec(shp, lambda bi, ti: (0,) * len(shp))
    tr = lambda rows: pl.BlockSpec((None, rows, tm), lambda bi, ti: (bi, 0, ti))
    nt = lambda cols: pl.BlockSpec((None, tm, cols), lambda bi, ti: (bi, ti, 0))
    sds = jax.ShapeDtypeStruct
    return pl.pallas_call(
        functools.partial(_prep_prompt_kernel, tm=tm),
        out_shape=(
            sds((b, 2 * D_GROUP, t), F32), sds((b, 2 * D_GROUP, t), F32),
            sds((b, N_HEADS, t), F32), sds((b, N_HEADS, t), F32),
            sds((b, D_AUG, t), BF16), sds((b, t, D_AUG), BF16), sds((b, D_GROUP, t), BF16),
            sds((b, D_GROUP, t), BF16), sds((b, t, D_GROUP), BF16), sds((b, D_GROUP, t), BF16),
        ),
        grid=(b, t // tm),
        in_specs=[col(3), col(4), col(5), col(6), col(7), col(8),
                  pl.BlockSpec((None, tm, LANES), lambda bi, ti: (bi, ti, 0)),
                  const((1, D_GROUP)), const((1, D_GROUP)), const((1, LANES)),
                  const((D_GROUP, D_GROUP)), const((D_GROUP, D_AUG)), const((N_SPLIT, LANES, D_AUG)),
                  const((1, D_AUG))],
        out_specs=(tr(2 * D_GROUP), tr(2 * D_GROUP), tr(N_HEADS), tr(N_HEADS),
                   tr(D_AUG), nt(D_AUG), tr(D_GROUP), tr(D_GROUP), nt(D_GROUP), tr(D_GROUP)),
        scratch_shapes=[pltpu.VMEM((1, LANES), F32)],
        compiler_params=_cparams(("arbitrary", "arbitrary")),
        name="prep_prompt",
    )(proj3, proj3, proj3, proj3, proj3, proj3, flog3, gq, gk, bf, seg, place, placec, neg)


def _prep_sample_kernel(fq_ref, fk_ref, fv_ref, sq_ref, sk_ref, sv_ref, fl_ref, gq_ref, gk_ref, bf_ref,
                        seg_ref, fkv_ref, skv_ref, lf_ref, c_ref, ct_ref, qf_ref, qs_ref, *, seq, tm):
    scale = HEAD_DIM ** -0.5
    fq = _head_norm(fq_ref[...], gq_ref[...], seg_ref)
    fk = _head_norm(fk_ref[...], gk_ref[...], seg_ref)
    qf_ref[...] = (fq * scale).astype(BF16)
    qs_ref[...] = (sq_ref[...] * scale).astype(BF16)
    fkv_ref[:, 0:D_GROUP] = fk
    fkv_ref[:, D_GROUP:2 * D_GROUP] = fv_ref[...]
    skv_ref[:, 0:D_GROUP] = sk_ref[...]
    skv_ref[:, D_GROUP:2 * D_GROUP] = sv_ref[...]
    lf = _log_sigmoid(fl_ref[...] + bf_ref[...])
    lf_ref[...] = lf
    cs = _running_sum(lf, seq, tm)
    c_ref[...] = cs
    ct_ref[...] = cs.T[0:N_HEADS, :]


def _prep_sample(proj, flog, gq, gk, bf, seg, seq):
    m = proj.shape[0]
    col = lambda s: pl.BlockSpec((m, D_GROUP), lambda i, s=s: (0, s))
    full = lambda shp: pl.BlockSpec(shp, lambda i: (0,) * len(shp))
    sds = jax.ShapeDtypeStruct
    return pl.pallas_call(
        functools.partial(_prep_sample_kernel, seq=seq, tm=m),
        out_shape=(sds((m, 2 * D_GROUP), F32), sds((m, 2 * D_GROUP), F32), sds((m, LANES), F32),
                   sds((m, LANES), F32), sds((N_HEADS, m), F32), sds((m, D_GROUP), BF16),
                   sds((m, D_GROUP), BF16)),
        grid=(1,),
        in_specs=[col(3), col(4), col(5), col(6), col(7), col(8), full((m, LANES)),
                  full((1, D_GROUP)), full((1, D_GROUP)), full((1, LANES)), full((D_GROUP, D_GROUP))],
        out_specs=(full((m, 2 * D_GROUP)), full((m, 2 * D_GROUP)), full((m, LANES)), full((m, LANES)),
                   full((N_HEADS, m)), full((m, D_GROUP)), full((m, D_GROUP))),
        compiler_params=_cparams(("arbitrary",)),
        name="prep_sample",
    )(proj, proj, proj, proj, proj, proj, flog, gq, gk, bf, seg)


def _seqmix_kernel(u_ref, x_ref, gate_ref, pprev_ref, cprev_ref, h0_ref,
                   pw_ref, pscale_ref, cw_ref, cb_ref, wax_ref, ba_ref, bx_ref, lam_ref,
                   yp_ref, yl_ref, pnew_ref, cnew_ref, hnew_ref,
                   extp_ref, extc_ref, h_ref, *, tt, tv, pos0):
    ti = pl.program_id(1)

    @pl.when(ti == 0)
    def _():
        extp_ref[0:HIST_P, :] = pprev_ref[...]
        extc_ref[0:HIST_C, :] = cprev_ref[...]
        h_ref[...] = h0_ref[...]

    extp_ref[HIST_P:HIST_P + tt, :] = u_ref[...]
    extc_ref[HIST_C:HIST_C + tt, :] = x_ref[...]
    pos = pos0 + ti * tt + lax.broadcasted_iota(jnp.int32, (tt, 1), 0)

    for g, w in enumerate(POOL_WINDOWS):
        lanes = slice(g * POOL_CH, (g + 1) * POOL_CH)
        tok = extp_ref[HIST_P:HIST_P + tt, lanes]
        win = tok
        for j in range(1, w):
            win = win + extp_ref[HIST_P - j:HIST_P - j + tt, lanes]
        cnt = jnp.minimum(w, pos + 1).astype(F32)
        d = win / cnt - tok
        y = _dot(d.astype(BF16), pw_ref[g]) * pscale_ref[:, lanes]
        yp_ref[:, lanes] = y
    hist = extp_ref[tv:tv + HIST_P, :]
    extp_ref[0:HIST_P, :] = hist
    pnew_ref[...] = hist

    base = HIST_C - (CONV_W - 1)
    xc = extc_ref[base:base + tt, :] * cw_ref[0:1, :]
    for k in range(1, CONV_W):
        xc = xc + extc_ref[base + k:base + k + tt, :] * cw_ref[k:k + 1, :]
    xc = cb_ref[...] + xc
    chist = extc_ref[tv:tv + HIST_C, :]
    extc_ref[0:HIST_C, :] = chist
    cnew_ref[...] = chist

    ri = _dot(xc.astype(BF16), wax_ref[...])
    r = jax.nn.sigmoid(ri[:, 0:D_GROUP] + ba_ref[...])
    gi = jax.nn.sigmoid(ri[:, D_GROUP:2 * D_GROUP] + bx_ref[...])
    log_a = -LRU_C * r * _softplus(-lam_ref[...])
    a = jnp.exp(log_a)
    mult = jnp.sqrt(-jnp.tanh(log_a) * (a * a + 1.0))
    mult = jnp.where(pos == 0, 1.0, mult)
    b = mult * (gi * xc)

    rows = lax.broadcasted_iota(jnp.int32, (tt, 1), 0)
    d = 1
    while d < tt:
        ok = rows >= d
        a_s = jnp.where(ok, pltpu.roll(a, d, 0), 1.0)
        b_s = jnp.where(ok, pltpu.roll(b, d, 0), 0.0)
        b = a * b_s + b
        a = a * a_s
        d *= 2
    h = b + a * h_ref[...]
    hlast = h[tv - 1:tv, :]
    h_ref[...] = hlast
    hnew_ref[...] = hlast
    yl_ref[...] = jax.nn.gelu(gate_ref[...]) * h


def _seqmix(proj3, pprev, cprev, h0, pw, pscale, cw, cb, wax, ba, bx, lam, tt, tv, pos0):
    b, t, _ = proj3.shape
    nt = t // tt
    col = lambda s: pl.BlockSpec((None, tt, D_GROUP), lambda bi, ti, s=s: (bi, ti, s))
    per_b = lambda r: pl.BlockSpec((None, r, D_GROUP), lambda bi, ti: (bi, 0, 0))
    const2 = lambda shp: pl.BlockSpec(shp, lambda bi, ti: (0, 0))
    return pl.pallas_call(
        functools.partial(_seqmix_kernel, tt=tt, tv=tv, pos0=pos0),
        out_shape=(
            jax.ShapeDtypeStruct((b, t, D_GROUP), F32),
            jax.ShapeDtypeStruct((b, t, D_GROUP), F32),
            jax.ShapeDtypeStruct((b, HIST_P, D_GROUP), F32),
            jax.ShapeDtypeStruct((b, HIST_C, D_GROUP), F32),
            jax.ShapeDtypeStruct((b, 1, D_GROUP), F32),
        ),
        grid=(b, nt),
        in_specs=[col(0), col(1), col(2), per_b(HIST_P), per_b(HIST_C), per_b(1),
                  pl.BlockSpec((len(POOL_WINDOWS), POOL_CH, POOL_CH), lambda bi, ti: (0, 0, 0)),
                  const2((1, D_GROUP)), const2((HIST_C, D_GROUP)), const2((1, D_GROUP)),
                  const2((D_GROUP, 2 * D_GROUP)), const2((1, D_GROUP)), const2((1, D_GROUP)),
                  const2((1, D_GROUP))],
        out_specs=(
            pl.BlockSpec((None, tt, D_GROUP), lambda bi, ti: (bi, ti, 0)),
            pl.BlockSpec((None, tt, D_GROUP), lambda bi, ti: (bi, ti, 0)),
            per_b(HIST_P), per_b(HIST_C), per_b(1),
        ),
        scratch_shapes=[pltpu.VMEM((HIST_P + tt, D_GROUP), F32),
                        pltpu.VMEM((HIST_C + tt, D_GROUP), F32),
                        pltpu.VMEM((1, D_GROUP), F32)],
        compiler_params=_cparams(("arbitrary", "arbitrary")),
        name="seqmix",
    )(proj3, proj3, proj3, pprev, cprev, h0, pw, pscale, cw, cb, wax, ba, bx, lam)


def _pair_tables(n, ratio, reverse):
    qi, ki = [], []
    for q in range(n):
        ks = range((q + 1) * ratio)
        for k in (reversed(ks) if reverse else ks):
            qi.append(q)
            ki.append(k)
    return jnp.asarray(np.array(qi, np.int32)), jnp.asarray(np.array(ki, np.int32))


def _key_minus_query(tk, tq):
    return (lax.broadcasted_iota(jnp.int32, (tk, tq), 0) - lax.broadcasted_iota(jnp.int32, (tk, tq), 1))


def _fox_prompt_kernel(qi_ref, ki_ref, qt_ref, k_ref, vt_ref, cq_ref, o_ref, m_ref, l_ref, acc_ref, *, tq, tk):
    s_id = pl.program_id(1)
    qi = qi_ref[s_id]
    ki = ki_ref[s_id]
    ratio = tq // tk

    @pl.when(ki == 0)
    def _():
        m_ref[...] = jnp.full_like(m_ref, NEG_BIG)
        l_ref[...] = jnp.zeros_like(l_ref)
        acc_ref[...] = jnp.zeros_like(acc_ref)

    def step(masked, first=0):
        win = slice(first, tq)
        if masked:
            causal = _key_minus_query(tk, tq - first) <= 0

        def scores(h):
            slot = slice(h * LANES, (h + 1) * LANES)
            return _dot(k_ref[:, slot], qt_ref[slot, win])

        def softmax_update(h, st):
            if masked:
                st = jnp.where(causal, st, NEG_BIG)
            cq = cq_ref[h:h + 1, win]
            m_old = m_ref[h:h + 1, win]
            m_new = jnp.maximum(m_old, jnp.max(st, axis=0, keepdims=True) + cq)
            alpha = jnp.exp(m_old - m_new)
            pt = jnp.exp(st - (m_new - cq))
            l_ref[h:h + 1, win] = alpha * l_ref[h:h + 1, win] + jnp.sum(pt, axis=0, keepdims=True)
            m_ref[h:h + 1, win] = m_new
            return alpha, pt.astype(BF16)

        st = {0: scores(0), 1: scores(1)}
        for h in range(N_HEADS):
            rows = slice(h * HEAD_DIM, (h + 1) * HEAD_DIM)
            if h + 2 < N_HEADS:
                st[h + 2] = scores(h + 2)
            alpha, pt = softmax_update(h, st.pop(h))
            acc_ref[rows, win] = alpha * acc_ref[rows, win] + _dot(vt_ref[rows, :], pt)

    @pl.when(ki < qi * ratio)
    def _():
        step(False)

    for diag in range(ratio):
        @pl.when(ki == qi * ratio + diag)
        def _(diag=diag):
            step(True, diag * tk)

    @pl.when(ki == (qi + 1) * ratio - 1)
    def _():
        for h in range(N_HEADS):
            rows = slice(h * HEAD_DIM, (h + 1) * HEAD_DIM)
            acc_ref[rows, :] = acc_ref[rows, :] / l_ref[h:h + 1, :]
        o_ref[...] = acc_ref[...].T


def _sb_prompt_kernel(qi_ref, ki_ref, qt_ref, k_ref, vt_ref, tri_ref, o_ref, carry_ref, acc_ref, *, tq, tk):
    s_id = pl.program_id(1)
    qi = qi_ref[s_id]
    ki = ki_ref[s_id]
    ratio = tq // tk

    @pl.when(ki == (qi + 1) * ratio - 1)
    def _():
        carry_ref[...] = jnp.zeros_like(carry_ref)
        acc_ref[...] = jnp.zeros_like(acc_ref)

    def step(masked, first=0):
        win = slice(first, tq)
        if masked:
            strict = _key_minus_query(tk, tq - first) < 0

        def logits(h):
            pair = slice((h // 2) * LANES, (h // 2 + 1) * LANES)
            qh = qt_ref[h * HEAD_DIM:(h + 1) * HEAD_DIM, win]
            zero = jnp.zeros_like(qh)
            qpair = jnp.concatenate([qh, zero] if h % 2 == 0 else [zero, qh], axis=0)
            return _dot(k_ref[:, pair], qpair)

        def neg_log_one_minus_beta(zt):
            sp = _softplus_exp2(zt)
            if masked:
                sp = jnp.where(strict, sp, 0.0)
            return (sp,) + tuple(_split_bf16(sp, 2))

        def later_keys_sum(hi, lo):
            return _dot(tri_ref[...], hi) + _dot(tri_ref[...], lo)

        def weights(zt, sp, rest):
            wgt = jnp.exp(zt - sp - rest)
            if masked:
                wgt = jnp.where(strict, wgt, 0.0)
            return wgt.astype(BF16)

        zt = {0: logits(0), 1: logits(1)}
        lg = {0: neg_log_one_minus_beta(zt[0])}
        rest = {}
        for h in range(N_HEADS + 1):
            if h < N_HEADS:
                rest[h] = later_keys_sum(*lg[h][1:])
            if h + 2 < N_HEADS:
                zt[h + 2] = logits(h + 2)
            if h >= 1:
                g = h - 1
                rows = slice(g * HEAD_DIM, (g + 1) * HEAD_DIM)
                sp, rs = lg.pop(g)[0], rest.pop(g)
                carry = carry_ref[g:g + 1, win]
                pv = _dot(vt_ref[rows, :], weights(zt.pop(g), sp, rs))
                acc_ref[rows, win] = acc_ref[rows, win] + pv * jnp.exp(-carry)
                carry_ref[g:g + 1, win] = carry + rs[0:1, :] + sp[0:1, :]
            if h + 1 < N_HEADS:
                lg[h + 1] = neg_log_one_minus_beta(zt[h + 1])

    for diag in range(ratio):
        @pl.when(ki == qi * ratio + diag)
        def _(diag=diag):
            step(True, diag * tk)

    @pl.when(ki < qi * ratio)
    def _():
        step(False)

    @pl.when(ki == 0)
    def _():
        o_ref[...] = acc_ref[...].T


def _prompt_attention(kind, qt, k, vt, extra, tq, tk):
    b, t, _ = k.shape
    nq = t // tq
    qi, ki = _pair_tables(nq, tq // tk, reverse=(kind == "sb"))
    d_qk = k.shape[2]
    in_specs = [pl.BlockSpec((None, d_qk, tq), lambda bi, s, qi, ki: (bi, 0, qi[s])),
                pl.BlockSpec((None, tk, d_qk), lambda bi, s, qi, ki: (bi, ki[s], 0)),
                pl.BlockSpec((None, D_GROUP, tk), lambda bi, s, qi, ki: (bi, 0, ki[s]))]
    if kind == "fox":
        in_specs.append(pl.BlockSpec((None, N_HEADS, tq), lambda bi, s, qi, ki: (bi, 0, qi[s])))
        body = _fox_prompt_kernel
        scratch = [pltpu.VMEM((N_HEADS, tq), F32), pltpu.VMEM((N_HEADS, tq), F32),
                   pltpu.VMEM((D_GROUP, tq), F32)]
    else:
        in_specs.append(pl.BlockSpec((tk, tk), lambda bi, s, qi, ki: (0, 0)))
        body = _sb_prompt_kernel
        scratch = [pltpu.VMEM((N_HEADS, tq), F32), pltpu.VMEM((D_GROUP, tq), F32)]
    grid_spec = pltpu.PrefetchScalarGridSpec(
        num_scalar_prefetch=2,
        grid=(b, int(qi.shape[0])),
        in_specs=in_specs,
        out_specs=pl.BlockSpec((None, tq, D_GROUP), lambda bi, s, qi, ki: (bi, qi[s], 0)),
        scratch_shapes=scratch,
    )
    return pl.pallas_call(
        functools.partial(body, tq=tq, tk=tk),
        out_shape=jax.ShapeDtypeStruct((b, t, D_GROUP), F32),
        grid_spec=grid_spec,
        compiler_params=_cparams(("arbitrary", "arbitrary")),
        name=kind + "_prompt",
    )(qi, ki, qt, k, vt, extra)


def _extract_heads(acc, t_new):
    hrow = lax.broadcasted_iota(jnp.int32, (N_HEADS, D_GROUP), 0)
    hcol = lax.broadcasted_iota(jnp.int32, (N_HEADS, D_GROUP), 1) // HEAD_DIM
    own = hrow == hcol
    outs = []
    for t in range(t_new):
        blk = acc[t * N_HEADS:(t + 1) * N_HEADS, :]
        outs.append(jnp.sum(jnp.where(own, blk, 0.0), axis=0, keepdims=True))
    return outs


def _fox_sample_stages(wq_ref, new_ref, gt_ref, grow_ref, tri_ref, cache_refs, lft_refs,
                       o_ref, m_ref, l_ref, acc_ref, dcar_ref, t_new):
    rows = t_new * N_HEADS

    def scores(kv_refs, biases, mask):
        s = []
        for kv_ref, bias in zip(kv_refs, biases):
            kt = kv_ref[0:D_GROUP, :].astype(BF16)
            s.append(_dot(wq_ref[...], kt) + jnp.concatenate([bias] * t_new, axis=0) + grow_ref[...])
        s = s[0] if len(s) == 1 else jnp.concatenate(s, axis=1)
        if mask is not None:
            s = jnp.where(mask, s, NEG_BIG)
        return s

    def attend(kv_refs, s):
        m_old = m_ref[...]
        m_new = jnp.maximum(m_old, jnp.max(s, axis=-1, keepdims=True))
        alpha = jnp.exp(m_old - m_new)
        pr = jnp.exp(s - m_new)
        l_ref[...] = alpha * l_ref[...] + jnp.sum(pr, axis=-1, keepdims=True)
        pv = None
        for i, kv_ref in enumerate(kv_refs):
            vt = kv_ref[D_GROUP:2 * D_GROUP, :].astype(BF16)
            t = _dot_nt(pr[:, i * PAGE:(i + 1) * PAGE].astype(BF16), vt)
            pv = t if pv is None else pv + t
        acc_ref[...] = alpha * acc_ref[...] + pv
        m_ref[...] = m_new

    def begin(first):
        @pl.when(first)
        def _():
            m_ref[...] = jnp.full_like(m_ref, NEG_BIG)
            l_ref[...] = jnp.zeros_like(l_ref)
            acc_ref[...] = jnp.zeros_like(acc_ref)
            dcar_ref[...] = jnp.zeros_like(dcar_ref)
            tok = lax.broadcasted_iota(jnp.int32, (rows, PAGE), 0) // N_HEADS
            key = lax.broadcasted_iota(jnp.int32, (rows, PAGE), 1)
            attend([new_ref], scores([new_ref], [-gt_ref[...]], key <= tok))

    def past_scores():
        lfs = [r[...] for r in lft_refs]
        parts = []
        for lf in lfs:
            parts += [x.astype(F32) for x in _split_bf16(lf, N_SPLIT)]
        r = _dot(jnp.concatenate(parts, axis=0).astype(BF16), tri_ref[...])
        run = dcar_ref[...]
        biases = []
        for i, lf in enumerate(lfs):
            base = i * N_SPLIT * N_HEADS
            later = r[base:base + N_HEADS]
            for j in range(1, N_SPLIT):
                later = later + r[base + j * N_HEADS:base + (j + 1) * N_HEADS]
            biases.append(later + run)
            run = run + later[:, 0:1] + lf[:, 0:1]
        dcar_ref[...] = run
        return scores(cache_refs, biases, None)

    def past_attend(s):
        attend(cache_refs, s)

    def finish(last):
        @pl.when(last)
        def _():
            out = acc_ref[...] / l_ref[...]
            for t, rowv in enumerate(_extract_heads(out, t_new)):
                o_ref[t:t + 1, :] = rowv

    return begin, past_scores, past_attend, finish


def _sb_sample_stages(wq_ref, new_ref, tri_ref, cache_refs, o_ref, acc_ref, car_ref, t_new):
    rows = t_new * N_HEADS

    def logits(kv_refs, mask):
        zs, l1ms, parts = [], [], []
        for kv_ref in kv_refs:
            kt = kv_ref[0:D_GROUP, :].astype(BF16)
            z = _dot(wq_ref[...], kt)
            l1m = _neg_softplus(z)
            if mask is not None:
                l1m = jnp.where(mask, l1m, 0.0)
            zs.append(z)
            l1ms.append(l1m)
            parts += [x.astype(F32) for x in _split_bf16(l1m, 2)]
        return zs, l1ms, jnp.concatenate(parts, axis=0).astype(BF16)

    def later_sums(stacked):
        return _dot(stacked, tri_ref[...])

    def attend(kv_refs, zs, l1ms, r, mask):
        run = car_ref[...]
        pv = None
        for i, kv_ref in enumerate(kv_refs):
            rest_i = r[2 * i * rows:(2 * i + 1) * rows] + r[(2 * i + 1) * rows:(2 * i + 2) * rows]
            wgt = jnp.exp(zs[i] + l1ms[i] + rest_i + run)
            if mask is not None:
                wgt = jnp.where(mask, wgt, 0.0)
            run = run + rest_i[:, 0:1] + l1ms[i][:, 0:1]
            vt = kv_ref[D_GROUP:2 * D_GROUP, :].astype(BF16)
            t = _dot_nt(wgt.astype(BF16), vt)
            pv = t if pv is None else pv + t
        car_ref[...] = run
        acc_ref[...] = acc_ref[...] + pv

    def begin(first):
        @pl.when(first)
        def _():
            acc_ref[...] = jnp.zeros_like(acc_ref)
            car_ref[...] = jnp.zeros_like(car_ref)
            tok = lax.broadcasted_iota(jnp.int32, (rows, PAGE), 0) // N_HEADS
            key = lax.broadcasted_iota(jnp.int32, (rows, PAGE), 1)
            mask = key < tok
            zs, l1ms, stacked = logits([new_ref], mask)
            attend([new_ref], zs, l1ms, later_sums(stacked), mask)

    def past_logits():
        return logits(cache_refs, None)

    def past_attend(zs, l1ms, r):
        attend(cache_refs, zs, l1ms, r, None)

    def finish(last):
        @pl.when(last)
        def _():
            for t, rowv in enumerate(_extract_heads(acc_ref[...], t_new)):
                o_ref[t:t + 1, :] = rowv

    return begin, past_logits, later_sums, past_attend, finish


def _outproj_kernel(x_ref, y0_ref, y1_ref, y2_ref, y3_ref, g_ref, w_ref, o_ref):
    acc = x_ref[...]
    for gi, y_ref in enumerate((y0_ref, y1_ref, y2_ref, y3_ref)):
        y = y_ref[...]
        ms = jnp.mean(y * y, axis=-1, keepdims=True)
        yn = (y * lax.rsqrt(ms + EPS) * g_ref[gi:gi + 1, :]).astype(BF16)
        acc = acc + _dot(yn, w_ref[gi * D_GROUP:(gi + 1) * D_GROUP, :])
    o_ref[...] = acc


def _outproj(x, ys, g, w, tm):
    m, d = x.shape
    yspec = pl.BlockSpec((tm, D_GROUP), lambda i: (i, 0))
    return pl.pallas_call(
        _outproj_kernel,
        out_shape=jax.ShapeDtypeStruct((m, d), F32),
        grid=(m // tm,),
        in_specs=[pl.BlockSpec((tm, d), lambda i: (i, 0)), yspec, yspec, yspec, yspec,
                  pl.BlockSpec((4, D_GROUP), lambda i: (0, 0)),
                  pl.BlockSpec((4 * D_GROUP, d), lambda i: (0, 0))],
        out_specs=pl.BlockSpec((tm, d), lambda i: (i, 0)),
        compiler_params=_cparams(("arbitrary",)),
        name="outproj",
    )(x, *ys, g, w)


def _mlp_kernel(x_ref, g_ref, wu_ref, wd_ref, o_ref, xn_ref):
    @pl.when(pl.program_id(1) == 0)
    def _():
        x = x_ref[...]
        ms = jnp.mean(x * x, axis=-1, keepdims=True)
        xn_ref[...] = (x * lax.rsqrt(ms + EPS) * g_ref[...]).astype(BF16)
        o_ref[...] = x

    hid = jnp.maximum(_dot(xn_ref[...], wu_ref[...]), 0.0)
    o_ref[...] += _dot((hid * hid).astype(BF16), wd_ref[...])


def _mlp(x, g, wu, wd, tm, tf):
    m, d = x.shape
    f = wu.shape[1]
    return pl.pallas_call(
        _mlp_kernel,
        out_shape=jax.ShapeDtypeStruct((m, d), F32),
        grid=(m // tm, f // tf),
        in_specs=[pl.BlockSpec((tm, d), lambda i, j: (i, 0)),
                  pl.BlockSpec((1, d), lambda i, j: (0, 0)),
                  pl.BlockSpec((d, tf), lambda i, j: (0, j)),
                  pl.BlockSpec((tf, d), lambda i, j: (j, 0))],
        out_specs=pl.BlockSpec((tm, d), lambda i, j: (i, 0)),
        scratch_shapes=[pltpu.VMEM((tm, d), BF16)],
        compiler_params=_cparams(("arbitrary", "arbitrary"), 56),
        name="mlp",
    )(x, g, wu, wd)


def _mlp_attn_kernel(pt_ref, x_ref, g_ref, wu_ref, wd_ref, fwq_ref, fnew_ref, gt_ref, grow_ref, tri_ref,
                     swq_ref, snew_ref, *rest, pages, groups, t_new):
    fcache = rest[0:pages]
    lft = rest[pages:2 * pages]
    scache = rest[2 * pages:3 * pages]
    (o_ref, of_ref, os_ref, xn_ref, fm_ref, fl_ref, facc_ref, fdcar_ref, sacc_ref,
     scar_ref) = rest[3 * pages:]
    j = pl.program_id(1)
    grp = lax.rem(pl.program_id(0) * pl.num_programs(1) + j, groups)
    first = grp == 0
    last = grp == groups - 1

    @pl.when(j == 0)
    def _():
        x = x_ref[...]
        ms = jnp.mean(x * x, axis=-1, keepdims=True)
        xn_ref[...] = (x * lax.rsqrt(ms + EPS) * g_ref[...]).astype(BF16)
        o_ref[...] = x

    f_begin, f_scores, f_attend, f_finish = _fox_sample_stages(
        fwq_ref, fnew_ref, gt_ref, grow_ref, tri_ref, fcache, lft, of_ref, fm_ref, fl_ref, facc_ref,
        fdcar_ref, t_new)
    s_begin, s_logits, s_later, s_attend, s_finish = _sb_sample_stages(
        swq_ref, snew_ref, tri_ref, scache, os_ref, sacc_ref, scar_ref, t_new)
    f_begin(first)
    s_begin(first)

    fs = f_scores()
    zs, l1ms, stacked = s_logits()
    hid = jnp.maximum(_dot(xn_ref[...], wu_ref[...]), 0.0)
    sr = s_later(stacked)
    f_attend(fs)
    o_ref[...] += _dot((hid * hid).astype(BF16), wd_ref[...])
    s_attend(zs, l1ms, sr)

    f_finish(last)
    s_finish(last)


def _mlp_attn(x, g, wu, wd, tm, tf, pt, fox, sb, tri, layer, t_new):
    m, d = x.shape
    f = wu.shape[1]
    fwq, fnew, gt, grow, fcache, lft_cache = fox
    swq, snew, scache = sb
    bs = fwq.shape[0]
    n_pages = pt.shape[0] // bs
    steps = (m // tm) * (f // tf)
    groups = steps // bs
    pages = n_pages // groups
    assert groups * bs == steps and pages * groups == n_pages, (steps, bs, n_pages)
    rows = t_new * N_HEADS
    nj = f // tf

    def seq_of(i, j):
        return lax.div(i * nj + j, groups)

    def page_spec(block, cache_layer, slot):
        def idx(i, j, pt):
            return (cache_layer, pt[(i * nj + j) * pages + slot], 0, 0)
        return pl.BlockSpec((None, None) + block, idx)

    per_seq = lambda r, c: pl.BlockSpec((None, r, c), lambda i, j, pt: (seq_of(i, j), 0, 0))
    const = lambda shp: pl.BlockSpec(shp, lambda i, j, pt: (0,) * len(shp))
    in_specs = ([pl.BlockSpec((tm, d), lambda i, j, pt: (i, 0)), const((1, d)),
                 pl.BlockSpec((d, tf), lambda i, j, pt: (0, j)),
                 pl.BlockSpec((tf, d), lambda i, j, pt: (j, 0)),
                 per_seq(rows, D_GROUP), per_seq(2 * D_GROUP, PAGE), per_seq(N_HEADS, PAGE), per_seq(rows, 1),
                 const((PAGE, PAGE)), per_seq(rows, D_GROUP), per_seq(2 * D_GROUP, PAGE)]
                + [page_spec((2 * D_GROUP, PAGE), layer, s) for s in range(pages)]
                + [page_spec((N_HEADS, PAGE), layer, s) for s in range(pages)]
                + [page_spec((2 * D_GROUP, PAGE), layer, s) for s in range(pages)])
    grid_spec = pltpu.PrefetchScalarGridSpec(
        num_scalar_prefetch=1,
        grid=(m // tm, nj),
        in_specs=in_specs,
        out_specs=(pl.BlockSpec((tm, d), lambda i, j, pt: (i, 0), pipeline_mode=pl.Buffered(1)),
                   per_seq(t_new, D_GROUP), per_seq(t_new, D_GROUP)),
        scratch_shapes=[pltpu.VMEM((tm, d), BF16),
                        pltpu.VMEM((rows, 1), F32), pltpu.VMEM((rows, 1), F32),
                        pltpu.VMEM((rows, D_GROUP), F32), pltpu.VMEM((N_HEADS, 1), F32),
                        pltpu.VMEM((rows, D_GROUP), F32), pltpu.VMEM((rows, 1), F32)],
    )
    sds = jax.ShapeDtypeStruct
    return pl.pallas_call(
        functools.partial(_mlp_attn_kernel, pages=pages, groups=groups, t_new=t_new),
        out_shape=(sds((m, d), F32), sds((bs, t_new, D_GROUP), F32), sds((bs, t_new, D_GROUP), F32)),
        grid_spec=grid_spec,
        compiler_params=_cparams(("arbitrary", "arbitrary"), 60),
        name="mlp_attn",
    )(pt, x, g, wu, wd, fwq, fnew, gt, grow, tri, swq, snew,
      *((fcache,) * pages + (lft_cache,) * pages + (scache,) * pages))


def _block_diag(w):
    n, d, _ = w.shape
    eye = jnp.eye(n, dtype=w.dtype)
    return (eye[:, None, :, None] * w[:, :, None, :]).reshape(n * d, n * d)


def _layer_weights(l, norm1_g, w_in, pool_w, pool_scale, conv_w, conv_b, lru_wa, lru_ba, lru_wx, lru_bx,
                   lru_lambda, fox_bf, fox_q_g, fox_k_g, out_g, w_out, norm2_g, w_up, w_down):
    g = D_GROUP
    wi = w_in[l]
    nf = 6 * g
    w_main = jnp.concatenate([wi[:, :nf], wi[:, nf + N_HEADS:]], axis=1).astype(BF16)
    w_f = jnp.pad(wi[:, nf:nf + N_HEADS], ((0, 0), (0, LANES - N_HEADS))).astype(BF16)
    return dict(
        norm1_g=norm1_g[l][None, :], w_main=w_main, w_f=w_f,
        pool_w=pool_w[l].astype(BF16), pool_scale=pool_scale[l][None, :],
        conv_w=jnp.pad(conv_w[l], ((0, HIST_C - CONV_W), (0, 0))), conv_b=conv_b[l][None, :],
        wax=jnp.concatenate([_block_diag(lru_wa[l]), _block_diag(lru_wx[l])], axis=1).astype(BF16),
        ba=lru_ba[l][None, :], bx=lru_bx[l][None, :], lam=lru_lambda[l][None, :],
        bf=jnp.pad(fox_bf[l], (0, LANES - N_HEADS))[None, :],
        gq=jnp.tile(fox_q_g[l], N_HEADS)[None, :], gk=jnp.tile(fox_k_g[l], N_HEADS)[None, :],
        out_g=out_g[l].reshape(4, g), w_out=w_out[l].astype(BF16),
        norm2_g=norm2_g[l][None, :], w_up=w_up[l].astype(BF16), w_down=w_down[l].astype(BF16),
    )


def _tile(m, pref):
    t = min(m, pref)
    while m % t:
        t //= 2
    return t


def _slot_constants():
    place = np.zeros((D_GROUP, D_AUG), np.float32)
    for c in range(D_GROUP):
        place[c, (c // HEAD_DIM) * LANES + c % HEAD_DIM] = 1.0
    placec = np.zeros((N_SPLIT, LANES, D_AUG), np.float32)
    neg = np.zeros((1, D_AUG), np.float32)
    for i in range(N_SPLIT):
        for h in range(N_HEADS):
            placec[i, h, h * LANES + HEAD_DIM + i] = 1.0
            neg[0, h * LANES + HEAD_DIM + i] = -1.0
    return jnp.asarray(place, BF16), jnp.asarray(placec, BF16), jnp.asarray(neg, F32)


def _expand_queries(q, b, t_new):
    q4 = q.reshape(b, t_new, 1, D_GROUP)
    own = (jnp.arange(D_GROUP)[None, :] // HEAD_DIM) == jnp.arange(N_HEADS)[:, None]
    return jnp.where(own[None, None], q4, jnp.zeros_like(q4)).reshape(b, t_new * N_HEADS, D_GROUP)


def _pages_view(cache):
    d, n = cache.shape[0], cache.shape[1]
    return jnp.transpose(cache, (0, 1, 3, 4, 5, 2)).reshape(d, n, 2 * D_GROUP, PAGE)


def kernel(x_prompt, x_sample, cache_fox_kv, cache_fox_logf, cache_sb_kv, state_pool, state_conv, state_lru,
           page_table, norm1_g, w_in, pool_w, pool_scale, conv_w, conv_b, lru_wa, lru_ba, lru_wx, lru_bx,
           lru_lambda, fox_bf, fox_q_g, fox_k_g, out_g, w_out, norm2_g, w_up, w_down):
    depth = w_in.shape[0]
    bp, seq, d_model = x_prompt.shape
    bs, t_new, _ = x_sample.shape
    past_len = page_table.shape[1] * PAGE
    g = D_GROUP

    seg = jnp.asarray(np.kron(np.eye(N_HEADS), np.ones((HEAD_DIM, HEAD_DIM))), BF16)
    place, placec, neg = _slot_constants()
    tk = _tile(seq, 256)
    tq = _tile(seq, 512)
    tri_q = jnp.asarray(np.triu(np.ones((tk, tk)), 1), BF16)
    tri_p = jnp.asarray(np.tril(np.ones((PAGE, PAGE)), -1), BF16)
    pt_flat = page_table[:, ::-1].reshape(-1).astype(jnp.int32)
    fox_cache = _pages_view(cache_fox_kv)
    sb_cache = _pages_view(cache_sb_kv)
    lft_cache = jnp.swapaxes(cache_fox_logf, 2, 3)
    t_pad = 8

    xp = x_prompt.reshape(bp * seq, d_model)
    xs = x_sample.reshape(bs * t_new, d_model)
    st_p, st_s = [], []
    for l in range(depth):
        w = _layer_weights(l, norm1_g, w_in, pool_w, pool_scale, conv_w, conv_b, lru_wa, lru_ba, lru_wx,
                           lru_bx, lru_lambda, fox_bf, fox_q_g, fox_k_g, out_g, w_out, norm2_g, w_up, w_down)

        def dense_tail(x, ys):
            rows = x.shape[0]
            x1 = _outproj(x, ys, w["out_g"], w["w_out"], _tile(rows, 512))
            return _mlp(x1, w["norm2_g"], w["w_up"], w["w_down"], _tile(rows, 1024),
                        _tile(w["w_up"].shape[1], 512))

        m = bp * seq
        proj, flog = _inproj(xp, w["norm1_g"], w["w_main"], w["w_f"], _tile(m, 1024))
        r3 = lambda a: a.reshape(bp, seq, a.shape[-1])
        fkvt, skvt, lft, ct, fqt, fka, fvt, sqt, ska, svt = _prep_prompt(
            r3(proj), r3(flog), w["gq"], w["gk"], w["bf"], seg, place, placec, neg, _tile(seq, 256))
        y_fox = _prompt_attention("fox", fqt, fka, fvt, ct, tq, tk)
        y_sb = _prompt_attention("sb", sqt, ska, svt, tri_q, tq, tk)
        tt = _tile(seq, 256)
        y_pool, y_lru, pnew, cnew, hnew = _seqmix(
            r3(proj), jnp.zeros((bp, HIST_P, g), F32), jnp.zeros((bp, HIST_C, g), F32),
            jnp.zeros((bp, 1, g), F32), w["pool_w"], w["pool_scale"], w["conv_w"], w["conv_b"], w["wax"],
            w["ba"], w["bx"], w["lam"], tt, tt, 0)
        flat = lambda a: a.reshape(m, g)
        x1p = _outproj(xp, (flat(y_pool), flat(y_lru), flat(y_fox), flat(y_sb)), w["out_g"], w["w_out"],
                       _tile(m, 512))
        kv_state = lambda a: jnp.transpose(a.reshape(bp, 2, N_HEADS, HEAD_DIM, seq), (0, 4, 1, 2, 3))
        st_p.append((kv_state(fkvt), jnp.swapaxes(lft, 1, 2), kv_state(skvt), pnew[:, 1:],
                     cnew[:, HIST_C - CONV_W + 1:], hnew[:, 0]))

        ms = bs * t_new
        proj, flog = _inproj(xs, w["norm1_g"], w["w_main"], w["w_f"], ms)
        fkv, skv, lf, c, ct, qf, qs = _prep_sample(proj, flog, w["gq"], w["gk"], w["bf"], seg, t_new)
        new_page = lambda a: jnp.pad(jnp.swapaxes(a.reshape(bs, t_new, 2 * g), 1, 2),
                                     ((0, 0), (0, 0), (0, PAGE - t_new)))
        gt = jnp.pad(jnp.swapaxes(ct.reshape(N_HEADS, bs, t_new), 0, 1), ((0, 0), (0, 0), (0, PAGE - t_new)))
        grow = c[:, :N_HEADS].reshape(bs, t_new * N_HEADS, 1)
        xp, y_fox, y_sb = _mlp_attn(
            x1p, w["norm2_g"], w["w_up"], w["w_down"], _tile(m, 1024), _tile(w["w_up"].shape[1], 512), pt_flat,
            (_expand_queries(qf, bs, t_new), new_page(fkv), gt, grow, fox_cache, lft_cache),
            (_expand_queries(qs, bs, t_new), new_page(skv), sb_cache), tri_p, l, t_new)
        proj3 = jnp.pad(proj.reshape(bs, t_new, -1), ((0, 0), (0, t_pad - t_new), (0, 0)))
        y_pool, y_lru, pnew, cnew, hnew = _seqmix(
            proj3, jnp.pad(state_pool[l], ((0, 0), (HIST_P - POOL_MAX + 1, 0), (0, 0))),
            jnp.pad(state_conv[l], ((0, 0), (HIST_C - CONV_W + 1, 0), (0, 0))), state_lru[l][:, None, :],
            w["pool_w"], w["pool_scale"], w["conv_w"], w["conv_b"], w["wax"], w["ba"], w["bx"], w["lam"],
            t_pad, t_new, past_len)
        cut = lambda a: a[:, :t_new].reshape(ms, g)
        xs = dense_tail(xs, (cut(y_pool), cut(y_lru), y_fox.reshape(ms, g), y_sb.reshape(ms, g)))
        st_s.append((fkv.reshape(bs, t_new, 2, N_HEADS, HEAD_DIM), lf[:, :N_HEADS].reshape(bs, t_new, N_HEADS),
                     skv.reshape(bs, t_new, 2, N_HEADS, HEAD_DIM), pnew[:, 1:], cnew[:, HIST_C - CONV_W + 1:],
                     hnew[:, 0]))

    stk = lambda sts, j: jnp.stack([s[j] for s in sts], axis=0)
    return ((xp.reshape(bp, seq, d_model), xs.reshape(bs, t_new, d_model))
            + tuple(stk(st_p, j) for j in range(6)) + tuple(stk(st_s, j) for j in range(6)))
```

```python
import functools

import numpy as np
import jax
import jax.numpy as jnp
from jax import lax
from jax.experimental import pallas as pl
from jax.experimental.pallas import tpu as pltpu

F32 = jnp.float32
BF16 = jnp.bfloat16

D_GROUP = 512
HEAD_DIM = 64
N_HEADS = D_GROUP // HEAD_DIM
POOL_WINDOWS = (2, 4, 8, 16)
POOL_MAX = max(POOL_WINDOWS)
POOL_CH = D_GROUP // len(POOL_WINDOWS)
CONV_W = 4
LRU_C = 8.0
EPS = 1e-6
PAGE = 128
LANES = 128
D_AUG = N_HEADS * LANES
N_SPLIT = 3
HIST_P = 16
HIST_C = 8
NEG_LOG2E = -1.4426950408889634
NEG_BIG = -1e30
MIB = 1024 * 1024


def _cparams(sem, vmem_mib=48):
    return pltpu.CompilerParams(dimension_semantics=sem, vmem_limit_bytes=vmem_mib * MIB)


def _split_bf16(x, parts):
    out = []
    r = x
    for _ in range(parts - 1):
        h = r.astype(BF16)
        out.append(h)
        r = r - h.astype(F32)
    out.append(r.astype(BF16))
    return out


def _softplus(x):
    return jnp.maximum(x, 0.0) + jnp.log1p(jnp.exp(-jnp.abs(x)))


def _log_sigmoid(x):
    return -_softplus(-x)


def _softplus_exp2(z):
    return jnp.maximum(z, 0.0) + jnp.log(1.0 + jnp.exp2(jnp.abs(z) * NEG_LOG2E))


def _neg_softplus(z):
    return -(jnp.maximum(z, 0.0) + jnp.log(1.0 + jnp.exp(-jnp.abs(z))))


def _dot_nt(a, b):
    return lax.dot_general(a, b, (((1,), (1,)), ((), ())), preferred_element_type=F32)


def _dot(a, b):
    return jnp.dot(a, b, preferred_element_type=F32)


def _inproj_kernel(x_ref, g_ref, w_ref, wf_ref, o_ref, of_ref, xn_ref):
    @pl.when(pl.program_id(1) == 0)
    def _():
        x = x_ref[...]
        ms = jnp.mean(x * x, axis=-1, keepdims=True)
        xn = (x * lax.rsqrt(ms + EPS) * g_ref[...]).astype(BF16)
        xn_ref[...] = xn
        of_ref[...] = _dot(xn, wf_ref[...])

    o_ref[...] = _dot(xn_ref[...], w_ref[...])


def _inproj(x, g, w, layer, tm):
    m, d = x.shape
    n = w.shape[2] - LANES
    tn = D_GROUP
    return pl.pallas_call(
        _inproj_kernel,
        out_shape=(jax.ShapeDtypeStruct((m, n), F32), jax.ShapeDtypeStruct((m, LANES), F32)),
        grid=(m // tm, n // tn),
        in_specs=[
            pl.BlockSpec((tm, d), lambda i, j: (i, 0)),
            pl.BlockSpec((1, d), lambda i, j: (0, 0)),
            pl.BlockSpec((None, d, tn), lambda i, j: (layer, 0, j)),
            pl.BlockSpec((None, d, LANES), lambda i, j: (layer, 0, n // LANES)),
        ],
        out_specs=(
            pl.BlockSpec((tm, tn), lambda i, j: (i, j)),
            pl.BlockSpec((tm, LANES), lambda i, j: (i, 0)),
        ),
        scratch_shapes=[pltpu.VMEM((tm, d), BF16)],
        compiler_params=_cparams(("arbitrary", "arbitrary")),
        name="inproj",
    )(x, g, w, w)


def _head_norm(x, g, seg_ref):
    hi, lo = _split_bf16(x * x, 2)
    ss = _dot(hi, seg_ref[...]) + _dot(lo, seg_ref[...])
    return x * lax.rsqrt(ss * (1.0 / HEAD_DIM) + EPS) * g


def _running_sum(lf, seq, tm):
    r = lax.broadcasted_iota(jnp.int32, (tm, tm), 0)
    c = lax.broadcasted_iota(jnp.int32, (tm, tm), 1)
    keep = c <= r
    if seq < tm:
        keep = jnp.logical_and(keep, (r // seq) == (c // seq))
    tri = jnp.where(keep, 1.0, 0.0).astype(BF16)
    cs = None
    for part in _split_bf16(lf, N_SPLIT):
        t = _dot(tri, part)
        cs = t if cs is None else cs + t
    return cs


def _prep_prompt_kernel(fq_ref, fk_ref, fv_ref, sq_ref, sk_ref, sv_ref, fl_ref, gq_ref, gk_ref, bf_ref,
                        seg_ref, place_ref, placec_ref, neg_ref,
                        fkvt_ref, skvt_ref, lft_ref, ct_ref,
                        fqt_ref, fka_ref, fvt_ref, sqt_ref, ska_ref, svt_ref, carry_ref, *, tm):
    scale = HEAD_DIM ** -0.5
    fq = _head_norm(fq_ref[...], gq_ref[...], seg_ref)
    fk = _head_norm(fk_ref[...], gk_ref[...], seg_ref)
    fv = fv_ref[...]
    sk = sk_ref[...]
    sv = sv_ref[...]

    fvt = fv.T
    svt = sv.T
    fkvt_ref[0:D_GROUP, :] = fk.T
    fkvt_ref[D_GROUP:2 * D_GROUP, :] = fvt
    skvt_ref[0:D_GROUP, :] = sk.T
    skvt_ref[D_GROUP:2 * D_GROUP, :] = svt
    fvt_ref[...] = fvt.astype(BF16)
    svt_ref[...] = svt.astype(BF16)

    lf = _log_sigmoid(fl_ref[...] + bf_ref[...])
    cs = _running_sum(lf, tm, tm)

    @pl.when(pl.program_id(1) == 0)
    def _():
        carry_ref[...] = jnp.zeros_like(carry_ref)
    cs = cs + carry_ref[...]
    carry_ref[...] = cs[tm - 1:tm, :]
    lft_ref[...] = lf.T[0:N_HEADS, :]
    ct_ref[...] = cs.T[0:N_HEADS, :]

    ka = _dot(fk.astype(BF16), place_ref[...])
    for i, part in enumerate(_split_bf16(cs, N_SPLIT)):
        ka = ka + _dot(part, placec_ref[i])
    fka_ref[...] = ka.astype(BF16)
    qa = _dot((fq * scale).astype(BF16), place_ref[...]) + neg_ref[...]
    fqt_ref[...] = qa.T.astype(BF16)
    ska_ref[...] = sk.astype(BF16)
    sqt_ref[...] = (sq_ref[...] * scale).T.astype(BF16)


def _prep_prompt(proj3, flog3, gq, gk, bf, seg, place, placec, neg, tm):
    b, t, _ = proj3.shape
    col = lambda s: pl.BlockSpec((None, tm, D_GROUP), lambda bi, ti, s=s: (bi, ti, s))
    const = lambda shp: pl.BlockSpec(shp, lambda bi, ti: (0,) * len(shp))
    tr = lambda rows: pl.BlockSpec((None, rows, tm), lambda bi, ti: (bi, 0, ti))
    nt = lambda cols: pl.BlockSpec((None, tm, cols), lambda bi, ti: (bi, ti, 0))
    sds = jax.ShapeDtypeStruct
    return pl.pallas_call(
        functools.partial(_prep_prompt_kernel, tm=tm),
        out_shape=(
            sds((b, 2 * D_GROUP, t), F32), sds((b, 2 * D_GROUP, t), F32),
            sds((b, N_HEADS, t), F32), sds((b, N_HEADS, t), F32),
            sds((b, D_AUG, t), BF16), sds((b, t, D_AUG), BF16), sds((b, D_GROUP, t), BF16),
            sds((b, D_GROUP, t), BF16), sds((b, t, D_GROUP), BF16), sds((b, D_GROUP, t), BF16),
        ),
        grid=(b, t // tm),
        in_specs=[col(3), col(4), col(5), col(6), col(7), col(8),
                  pl.BlockSpec((None, tm, LANES), lambda bi, ti: (bi, ti, 0)),
                  const((1, D_GROUP)), const((1, D_GROUP)), const((1, LANES)),
                  const((D_GROUP, D_GROUP)), const((D_GROUP, D_AUG)), const((N_SPLIT, LANES, D_AUG)),
                  const((1, D_AUG))],
        out_specs=(tr(2 * D_GROUP), tr(2 * D_GROUP), tr(N_HEADS), tr(N_HEADS),
                   tr(D_AUG), nt(D_AUG), tr(D_GROUP), tr(D_GROUP), nt(D_GROUP), tr(D_GROUP)),
        scratch_shapes=[pltpu.VMEM((1, LANES), F32)],
        compiler_params=_cparams(("arbitrary", "arbitrary")),
        name="prep_prompt",
    )(proj3, proj3, proj3, proj3, proj3, proj3, flog3, gq, gk, bf, seg, place, placec, neg)


def _prep_sample_kernel(fq_ref, fk_ref, fv_ref, sq_ref, sk_ref, sv_ref, fl_ref, gq_ref, gk_ref, bf_ref,
                        seg_ref, fkv_ref, skv_ref, lf_ref, c_ref, ct_ref, qf_ref, qs_ref, *, seq, tm):
    scale = HEAD_DIM ** -0.5
    fq = _head_norm(fq_ref[...], gq_ref[...], seg_ref)
    fk = _head_norm(fk_ref[...], gk_ref[...], seg_ref)
    qf_ref[...] = (fq * scale).astype(BF16)
    qs_ref[...] = (sq_ref[...] * scale).astype(BF16)
    fkv_ref[:, 0:D_GROUP] = fk
    fkv_ref[:, D_GROUP:2 * D_GROUP] = fv_ref[...]
    skv_ref[:, 0:D_GROUP] = sk_ref[...]
    skv_ref[:, D_GROUP:2 * D_GROUP] = sv_ref[...]
    lf = _log_sigmoid(fl_ref[...] + bf_ref[...])
    lf_ref[...] = lf
    cs = _running_sum(lf, seq, tm)
    c_ref[...] = cs
    ct_ref[...] = cs.T[0:N_HEADS, :]


def _prep_sample(proj, flog, gq, gk, bf, seg, seq):
    m = proj.shape[0]
    col = lambda s: pl.BlockSpec((m, D_GROUP), lambda i, s=s: (0, s))
    full = lambda shp: pl.BlockSpec(shp, lambda i: (0,) * len(shp))
    sds = jax.ShapeDtypeStruct
    return pl.pallas_call(
        functools.partial(_prep_sample_kernel, seq=seq, tm=m),
        out_shape=(sds((m, 2 * D_GROUP), F32), sds((m, 2 * D_GROUP), F32), sds((m, LANES), F32),
                   sds((m, LANES), F32), sds((N_HEADS, m), F32), sds((m, D_GROUP), BF16),
                   sds((m, D_GROUP), BF16)),
        grid=(1,),
        in_specs=[col(3), col(4), col(5), col(6), col(7), col(8), full((m, LANES)),
                  full((1, D_GROUP)), full((1, D_GROUP)), full((1, LANES)), full((D_GROUP, D_GROUP))],
        out_specs=(full((m, 2 * D_GROUP)), full((m, 2 * D_GROUP)), full((m, LANES)), full((m, LANES)),
                   full((N_HEADS, m)), full((m, D_GROUP)), full((m, D_GROUP))),
        compiler_params=_cparams(("arbitrary",)),
        name="prep_sample",
    )(proj, proj, proj, proj, proj, proj, flog, gq, gk, bf, seg)


def _seqmix_kernel(u_ref, x_ref, gate_ref, pprev_ref, cprev_ref, h0_ref,
                   pw_ref, pscale_ref, cw_ref, cb_ref, wax_ref, ba_ref, bx_ref, lam_ref,
                   yp_ref, yl_ref, pnew_ref, cnew_ref, hnew_ref,
                   extp_ref, extc_ref, h_ref, *, tt, tv, pos0):
    ti = pl.program_id(1)

    @pl.when(ti == 0)
    def _():
        extp_ref[0:HIST_P, :] = pprev_ref[...]
        extc_ref[0:HIST_C, :] = cprev_ref[...]
        h_ref[...] = h0_ref[...]

    extp_ref[HIST_P:HIST_P + tt, :] = u_ref[...]
    extc_ref[HIST_C:HIST_C + tt, :] = x_ref[...]
    pos = pos0 + ti * tt + lax.broadcasted_iota(jnp.int32, (tt, 1), 0)

    for g, w in enumerate(POOL_WINDOWS):
        lanes = slice(g * POOL_CH, (g + 1) * POOL_CH)
        tok = extp_ref[HIST_P:HIST_P + tt, lanes]
        win = tok
        for j in range(1, w):
            win = win + extp_ref[HIST_P - j:HIST_P - j + tt, lanes]
        cnt = jnp.minimum(w, pos + 1).astype(F32)
        d = win / cnt - tok
        y = _dot(d.astype(BF16), pw_ref[g]) * pscale_ref[:, lanes]
        yp_ref[:, lanes] = y
    hist = extp_ref[tv:tv + HIST_P, :]
    extp_ref[0:HIST_P, :] = hist
    pnew_ref[...] = hist

    base = HIST_C - (CONV_W - 1)
    xc = extc_ref[base:base + tt, :] * cw_ref[0:1, :]
    for k in range(1, CONV_W):
        xc = xc + extc_ref[base + k:base + k + tt, :] * cw_ref[k:k + 1, :]
    xc = cb_ref[...] + xc
    chist = extc_ref[tv:tv + HIST_C, :]
    extc_ref[0:HIST_C, :] = chist
    cnew_ref[...] = chist

    ri = _dot(xc.astype(BF16), wax_ref[...])
    r = jax.nn.sigmoid(ri[:, 0:D_GROUP] + ba_ref[...])
    gi = jax.nn.sigmoid(ri[:, D_GROUP:2 * D_GROUP] + bx_ref[...])
    log_a = -LRU_C * r * _softplus(-lam_ref[...])
    a = jnp.exp(log_a)
    mult = jnp.sqrt(-jnp.tanh(log_a) * (a * a + 1.0))
    mult = jnp.where(pos == 0, 1.0, mult)
    b = mult * (gi * xc)

    rows = lax.broadcasted_iota(jnp.int32, (tt, 1), 0)
    d = 1
    while d < tt:
        ok = rows >= d
        a_s = jnp.where(ok, pltpu.roll(a, d, 0), 1.0)
        b_s = jnp.where(ok, pltpu.roll(b, d, 0), 0.0)
        b = a * b_s + b
        a = a * a_s
        d *= 2
    h = b + a * h_ref[...]
    hlast = h[tv - 1:tv, :]
    h_ref[...] = hlast
    hnew_ref[...] = hlast
    yl_ref[...] = jax.nn.gelu(gate_ref[...]) * h


def _seqmix(proj3, pprev, cprev, h0, pw, pscale, cw, cb, wax, ba, bx, lam, tt, tv, pos0):
    b, t, _ = proj3.shape
    nt = t // tt
    col = lambda s: pl.BlockSpec((None, tt, D_GROUP), lambda bi, ti, s=s: (bi, ti, s))
    per_b = lambda r: pl.BlockSpec((None, r, D_GROUP), lambda bi, ti: (bi, 0, 0))
    const2 = lambda shp: pl.BlockSpec(shp, lambda bi, ti: (0, 0))
    return pl.pallas_call(
        functools.partial(_seqmix_kernel, tt=tt, tv=tv, pos0=pos0),
        out_shape=(
            jax.ShapeDtypeStruct((b, t, D_GROUP), F32),
            jax.ShapeDtypeStruct((b, t, D_GROUP), F32),
            jax.ShapeDtypeStruct((b, HIST_P, D_GROUP), F32),
            jax.ShapeDtypeStruct((b, HIST_C, D_GROUP), F32),
            jax.ShapeDtypeStruct((b, 1, D_GROUP), F32),
        ),
        grid=(b, nt),
        in_specs=[col(0), col(1), col(2), per_b(HIST_P), per_b(HIST_C), per_b(1),
                  pl.BlockSpec((len(POOL_WINDOWS), POOL_CH, POOL_CH), lambda bi, ti: (0, 0, 0)),
                  const2((1, D_GROUP)), const2((HIST_C, D_GROUP)), const2((1, D_GROUP)),
                  const2((D_GROUP, 2 * D_GROUP)), const2((1, D_GROUP)), const2((1, D_GROUP)),
                  const2((1, D_GROUP))],
        out_specs=(
            pl.BlockSpec((None, tt, D_GROUP), lambda bi, ti: (bi, ti, 0)),
            pl.BlockSpec((None, tt, D_GROUP), lambda bi, ti: (bi, ti, 0)),
            per_b(HIST_P), per_b(HIST_C), per_b(1),
        ),
        scratch_shapes=[pltpu.VMEM((HIST_P + tt, D_GROUP), F32),
                        pltpu.VMEM((HIST_C + tt, D_GROUP), F32),
                        pltpu.VMEM((1, D_GROUP), F32)],
        compiler_params=_cparams(("arbitrary", "arbitrary")),
        name="seqmix",
    )(proj3, proj3, proj3, pprev, cprev, h0, pw, pscale, cw, cb, wax, ba, bx, lam)


def _pair_tables(n, ratio, reverse):
    qi, ki = [], []
    for q in range(n):
        ks = range((q + 1) * ratio)
        for k in (reversed(ks) if reverse else ks):
            qi.append(q)
            ki.append(k)
    return jnp.asarray(np.array(qi, np.int32)), jnp.asarray(np.array(ki, np.int32))


def _key_minus_query(tk, tq):
    return (lax.broadcasted_iota(jnp.int32, (tk, tq), 0) - lax.broadcasted_iota(jnp.int32, (tk, tq), 1))


def _fox_prompt_kernel(qi_ref, ki_ref, qt_ref, k_ref, vt_ref, cq_ref, o_ref, m_ref, l_ref, acc_ref, *, tq, tk):
    s_id = pl.program_id(1)
    qi = qi_ref[s_id]
    ki = ki_ref[s_id]
    ratio = tq // tk

    @pl.when(ki == 0)
    def _():
        m_ref[...] = jnp.full_like(m_ref, NEG_BIG)
        l_ref[...] = jnp.zeros_like(l_ref)
        acc_ref[...] = jnp.zeros_like(acc_ref)

    def step(masked, first=0):
        win = slice(first, tq)
        if masked:
            causal = _key_minus_query(tk, tq - first) <= 0

        def scores(h):
            slot = slice(h * LANES, (h + 1) * LANES)
            return _dot(k_ref[:, slot], qt_ref[slot, win])

        def softmax_update(h, st):
            if masked:
                st = jnp.where(causal, st, NEG_BIG)
            cq = cq_ref[h:h + 1, win]
            m_old = m_ref[h:h + 1, win]
            m_new = jnp.maximum(m_old, jnp.max(st, axis=0, keepdims=True) + cq)
            alpha = jnp.exp(m_old - m_new)
            pt = jnp.exp(st - (m_new - cq))
            l_ref[h:h + 1, win] = alpha * l_ref[h:h + 1, win] + jnp.sum(pt, axis=0, keepdims=True)
            m_ref[h:h + 1, win] = m_new
            return alpha, pt.astype(BF16)

        st = {0: scores(0), 1: scores(1)}
        for h in range(N_HEADS):
            rows = slice(h * HEAD_DIM, (h + 1) * HEAD_DIM)
            if h + 2 < N_HEADS:
                st[h + 2] = scores(h + 2)
            alpha, pt = softmax_update(h, st.pop(h))
            acc_ref[rows, win] = alpha * acc_ref[rows, win] + _dot(vt_ref[rows, :], pt)

    @pl.when(ki < qi * ratio)
    def _():
        step(False)

    for diag in range(ratio):
        @pl.when(ki == qi * ratio + diag)
        def _(diag=diag):
            step(True, diag * tk)

    @pl.when(ki == (qi + 1) * ratio - 1)
    def _():
        for h in range(N_HEADS):
            rows = slice(h * HEAD_DIM, (h + 1) * HEAD_DIM)
            acc_ref[rows, :] = acc_ref[rows, :] / l_ref[h:h + 1, :]
        o_ref[...] = acc_ref[...].T


def _sb_prompt_kernel(qi_ref, ki_ref, qt_ref, k_ref, vt_ref, tri_ref, o_ref, carry_ref, acc_ref, *, tq, tk):
    s_id = pl.program_id(1)
    qi = qi_ref[s_id]
    ki = ki_ref[s_id]
    ratio = tq // tk

    @pl.when(ki == (qi + 1) * ratio - 1)
    def _():
        carry_ref[...] = jnp.zeros_like(carry_ref)
        acc_ref[...] = jnp.zeros_like(acc_ref)

    def step(masked, first=0):
        win = slice(first, tq)
        if masked:
            strict = _key_minus_query(tk, tq - first) < 0

        def logits(h):
            pair = slice((h // 2) * LANES, (h // 2 + 1) * LANES)
            qh = qt_ref[h * HEAD_DIM:(h + 1) * HEAD_DIM, win]
            zero = jnp.zeros_like(qh)
            qpair = jnp.concatenate([qh, zero] if h % 2 == 0 else [zero, qh], axis=0)
            return _dot(k_ref[:, pair], qpair)

        def neg_log_one_minus_beta(zt):
            sp = _softplus_exp2(zt)
            if masked:
                sp = jnp.where(strict, sp, 0.0)
            return (sp,) + tuple(_split_bf16(sp, 2))

        def later_keys_sum(hi, lo):
            return _dot(tri_ref[...], hi) + _dot(tri_ref[...], lo)

        def weights(zt, sp, rest):
            wgt = jnp.exp(zt - sp - rest)
            if masked:
                wgt = jnp.where(strict, wgt, 0.0)
            return wgt.astype(BF16)

        zt = {0: logits(0), 1: logits(1)}
        lg = {0: neg_log_one_minus_beta(zt[0])}
        rest = {}
        for h in range(N_HEADS + 1):
            if h < N_HEADS:
                rest[h] = later_keys_sum(*lg[h][1:])
            if h + 2 < N_HEADS:
                zt[h + 2] = logits(h + 2)
            if h >= 1:
                g = h - 1
                rows = slice(g * HEAD_DIM, (g + 1) * HEAD_DIM)
                sp, rs = lg.pop(g)[0], rest.pop(g)
                carry = carry_ref[g:g + 1, win]
                pv = _dot(vt_ref[rows, :], weights(zt.pop(g), sp, rs))
                acc_ref[rows, win] = acc_ref[rows, win] + pv * jnp.exp(-carry)
                carry_ref[g:g + 1, win] = carry + rs[0:1, :] + sp[0:1, :]
            if h + 1 < N_HEADS:
                lg[h + 1] = neg_log_one_minus_beta(zt[h + 1])

    for diag in range(ratio):
        @pl.when(ki == qi * ratio + diag)
        def _(diag=diag):
            step(True, diag * tk)

    @pl.when(ki < qi * ratio)
    def _():
        step(False)

    @pl.when(ki == 0)
    def _():
        o_ref[...] = acc_ref[...].T


def _prompt_attention(kind, qt, k, vt, extra, tq, tk):
    b, t, _ = k.shape
    nq = t // tq
    qi, ki = _pair_tables(nq, tq // tk, reverse=(kind == "sb"))
    d_qk = k.shape[2]
    in_specs = [pl.BlockSpec((None, d_qk, tq), lambda bi, s, qi, ki: (bi, 0, qi[s])),
                pl.BlockSpec((None, tk, d_qk), lambda bi, s, qi, ki: (bi, ki[s], 0)),
                pl.BlockSpec((None, D_GROUP, tk), lambda bi, s, qi, ki: (bi, 0, ki[s]))]
    if kind == "fox":
        in_specs.append(pl.BlockSpec((None, N_HEADS, tq), lambda bi, s, qi, ki: (bi, 0, qi[s])))
        body = _fox_prompt_kernel
        scratch = [pltpu.VMEM((N_HEADS, tq), F32), pltpu.VMEM((N_HEADS, tq), F32),
                   pltpu.VMEM((D_GROUP, tq), F32)]
    else:
        in_specs.append(pl.BlockSpec((tk, tk), lambda bi, s, qi, ki: (0, 0)))
        body = _sb_prompt_kernel
        scratch = [pltpu.VMEM((N_HEADS, tq), F32), pltpu.VMEM((D_GROUP, tq), F32)]
    grid_spec = pltpu.PrefetchScalarGridSpec(
        num_scalar_prefetch=2,
        grid=(b, int(qi.shape[0])),
        in_specs=in_specs,
        out_specs=pl.BlockSpec((None, tq, D_GROUP), lambda bi, s, qi, ki: (bi, qi[s], 0)),
        scratch_shapes=scratch,
    )
    return pl.pallas_call(
        functools.partial(body, tq=tq, tk=tk),
        out_shape=jax.ShapeDtypeStruct((b, t, D_GROUP), F32),
        grid_spec=grid_spec,
        compiler_params=_cparams(("arbitrary", "arbitrary")),
        name=kind + "_prompt",
    )(qi, ki, qt, k, vt, extra)


def _extract_heads(acc, t_new):
    hrow = lax.broadcasted_iota(jnp.int32, (N_HEADS, D_GROUP), 0)
    hcol = lax.broadcasted_iota(jnp.int32, (N_HEADS, D_GROUP), 1) // HEAD_DIM
    own = hrow == hcol
    outs = []
    for t in range(t_new):
        blk = acc[t * N_HEADS:(t + 1) * N_HEADS, :]
        outs.append(jnp.sum(jnp.where(own, blk, 0.0), axis=0, keepdims=True))
    return outs


def _fox_sample_stages(wq_ref, new_ref, gt_ref, grow_ref, tri_ref, cache_refs, lft_refs,
                       o_ref, m_ref, l_ref, acc_ref, dcar_ref, t_new):
    rows = t_new * N_HEADS

    def scores(kv_refs, biases, mask):
        s = []
        for kv_ref, bias in zip(kv_refs, biases):
            kt = kv_ref[0:D_GROUP, :].astype(BF16)
            s.append(_dot(wq_ref[...], kt) + jnp.concatenate([bias] * t_new, axis=0) + grow_ref[...])
        s = s[0] if len(s) == 1 else jnp.concatenate(s, axis=1)
        if mask is not None:
            s = jnp.where(mask, s, NEG_BIG)
        return s

    def attend(kv_refs, s):
        m_old = m_ref[...]
        m_new = jnp.maximum(m_old, jnp.max(s, axis=-1, keepdims=True))
        alpha = jnp.exp(m_old - m_new)
        pr = jnp.exp(s - m_new)
        l_ref[...] = alpha * l_ref[...] + jnp.sum(pr, axis=-1, keepdims=True)
        pv = None
        for i, kv_ref in enumerate(kv_refs):
            vt = kv_ref[D_GROUP:2 * D_GROUP, :].astype(BF16)
            t = _dot_nt(pr[:, i * PAGE:(i + 1) * PAGE].astype(BF16), vt)
            pv = t if pv is None else pv + t
        acc_ref[...] = alpha * acc_ref[...] + pv
        m_ref[...] = m_new

    def begin(first):
        @pl.when(first)
        def _():
            m_ref[...] = jnp.full_like(m_ref, NEG_BIG)
            l_ref[...] = jnp.zeros_like(l_ref)
            acc_ref[...] = jnp.zeros_like(acc_ref)
            dcar_ref[...] = jnp.zeros_like(dcar_ref)
            tok = lax.broadcasted_iota(jnp.int32, (rows, PAGE), 0) // N_HEADS
            key = lax.broadcasted_iota(jnp.int32, (rows, PAGE), 1)
            attend([new_ref], scores([new_ref], [-gt_ref[...]], key <= tok))

    def past_scores():
        lfs = [r[...] for r in lft_refs]
        parts = []
        for lf in lfs:
            parts += [x.astype(F32) for x in _split_bf16(lf, N_SPLIT)]
        r = _dot(jnp.concatenate(parts, axis=0).astype(BF16), tri_ref[...])
        run = dcar_ref[...]
        biases = []
        for i, lf in enumerate(lfs):
            base = i * N_SPLIT * N_HEADS
            later = r[base:base + N_HEADS]
            for j in range(1, N_SPLIT):
                later = later + r[base + j * N_HEADS:base + (j + 1) * N_HEADS]
            biases.append(later + run)
            run = run + later[:, 0:1] + lf[:, 0:1]
        dcar_ref[...] = run
        return scores(cache_refs, biases, None)

    def past_attend(s):
        attend(cache_refs, s)

    def finish(last):
        @pl.when(last)
        def _():
            out = acc_ref[...] / l_ref[...]
            for t, rowv in enumerate(_extract_heads(out, t_new)):
                o_ref[t:t + 1, :] = rowv

    return begin, past_scores, past_attend, finish


def _sb_sample_stages(wq_ref, new_ref, tri_ref, cache_refs, o_ref, acc_ref, car_ref, t_new):
    rows = t_new * N_HEADS

    def logits(kv_refs, mask):
        zs, l1ms, parts = [], [], []
        for kv_ref in kv_refs:
            kt = kv_ref[0:D_GROUP, :].astype(BF16)
            z = _dot(wq_ref[...], kt)
            l1m = _neg_softplus(z)
            if mask is not None:
                l1m = jnp.where(mask, l1m, 0.0)
            zs.append(z)
            l1ms.append(l1m)
            parts += [x.astype(F32) for x in _split_bf16(l1m, 2)]
        return zs, l1ms, jnp.concatenate(parts, axis=0).astype(BF16)

    def later_sums(stacked):
        return _dot(stacked, tri_ref[...])

    def attend(kv_refs, zs, l1ms, r, mask):
        run = car_ref[...]
        pv = None
        for i, kv_ref in enumerate(kv_refs):
            rest_i = r[2 * i * rows:(2 * i + 1) * rows] + r[(2 * i + 1) * rows:(2 * i + 2) * rows]
            wgt = jnp.exp(zs[i] + l1ms[i] + rest_i + run)
            if mask is not None:
                wgt = jnp.where(mask, wgt, 0.0)
            run = run + rest_i[:, 0:1] + l1ms[i][:, 0:1]
            vt = kv_ref[D_GROUP:2 * D_GROUP, :].astype(BF16)
            t = _dot_nt(wgt.astype(BF16), vt)
            pv = t if pv is None else pv + t
        car_ref[...] = run
        acc_ref[...] = acc_ref[...] + pv

    def begin(first):
        @pl.when(first)
        def _():
            acc_ref[...] = jnp.zeros_like(acc_ref)
            car_ref[...] = jnp.zeros_like(car_ref)
            tok = lax.broadcasted_iota(jnp.int32, (rows, PAGE), 0) // N_HEADS
            key = lax.broadcasted_iota(jnp.int32, (rows, PAGE), 1)
            mask = key < tok
            zs, l1ms, stacked = logits([new_ref], mask)
            attend([new_ref], zs, l1ms, later_sums(stacked), mask)

    def past_logits():
        return logits(cache_refs, None)

    def past_attend(zs, l1ms, r):
        attend(cache_refs, zs, l1ms, r, None)

    def finish(last):
        @pl.when(last)
        def _():
            for t, rowv in enumerate(_extract_heads(acc_ref[...], t_new)):
                o_ref[t:t + 1, :] = rowv

    return begin, past_logits, later_sums, past_attend, finish


def _outproj_kernel(x_ref, y0_ref, y1_ref, y2_ref, y3_ref, g_ref, w_ref, o_ref):
    acc = x_ref[...]
    for gi, y_ref in enumerate((y0_ref, y1_ref, y2_ref, y3_ref)):
        y = y_ref[...]
        ms = jnp.mean(y * y, axis=-1, keepdims=True)
        yn = (y * lax.rsqrt(ms + EPS) * g_ref[gi:gi + 1, :]).astype(BF16)
        acc = acc + _dot(yn, w_ref[gi * D_GROUP:(gi + 1) * D_GROUP, :])
    o_ref[...] = acc


def _outproj(x, ys, g, w, layer, tm):
    m, d = x.shape
    yspec = pl.BlockSpec((tm, D_GROUP), lambda i: (i, 0))
    return pl.pallas_call(
        _outproj_kernel,
        out_shape=jax.ShapeDtypeStruct((m, d), F32),
        grid=(m // tm,),
        in_specs=[pl.BlockSpec((tm, d), lambda i: (i, 0)), yspec, yspec, yspec, yspec,
                  pl.BlockSpec((4, D_GROUP), lambda i: (0, 0)),
                  pl.BlockSpec((None, 4 * D_GROUP, d), lambda i: (layer, 0, 0))],
        out_specs=pl.BlockSpec((tm, d), lambda i: (i, 0)),
        compiler_params=_cparams(("arbitrary",)),
        name="outproj",
    )(x, *ys, g, w)


def _mlp_kernel(x_ref, g_ref, wu_ref, wd_ref, o_ref, xn_ref):
    @pl.when(pl.program_id(1) == 0)
    def _():
        x = x_ref[...]
        ms = jnp.mean(x * x, axis=-1, keepdims=True)
        xn_ref[...] = (x * lax.rsqrt(ms + EPS) * g_ref[...]).astype(BF16)
        o_ref[...] = x

    hid = jnp.maximum(_dot(xn_ref[...], wu_ref[...]), 0.0)
    o_ref[...] += _dot((hid * hid).astype(BF16), wd_ref[...])


def _mlp(x, g, wu, wd, layer, tm, tf):
    m, d = x.shape
    f = wu.shape[2]
    return pl.pallas_call(
        _mlp_kernel,
        out_shape=jax.ShapeDtypeStruct((m, d), F32),
        grid=(m // tm, f // tf),
        in_specs=[pl.BlockSpec((tm, d), lambda i, j: (i, 0)),
                  pl.BlockSpec((1, d), lambda i, j: (0, 0)),
                  pl.BlockSpec((None, d, tf), lambda i, j: (layer, 0, j)),
                  pl.BlockSpec((None, tf, d), lambda i, j: (layer, j, 0))],
        out_specs=pl.BlockSpec((tm, d), lambda i, j: (i, 0)),
        scratch_shapes=[pltpu.VMEM((tm, d), BF16)],
        compiler_params=_cparams(("arbitrary", "arbitrary"), 56),
        name="mlp",
    )(x, g, wu, wd)


def _mlp_attn_kernel(pt_ref, x_ref, g_ref, wu_ref, wd_ref, fwq_ref, fnew_ref, gt_ref, grow_ref, tri_ref,
                     swq_ref, snew_ref, *rest, pages, groups, t_new):
    fcache = rest[0:pages]
    lft = rest[pages:2 * pages]
    scache = rest[2 * pages:3 * pages]
    (o_ref, of_ref, os_ref, xn_ref, fm_ref, fl_ref, facc_ref, fdcar_ref, sacc_ref,
     scar_ref) = rest[3 * pages:]
    j = pl.program_id(1)
    grp = lax.rem(pl.program_id(0) * pl.num_programs(1) + j, groups)
    first = grp == 0
    last = grp == groups - 1

    @pl.when(j == 0)
    def _():
        x = x_ref[...]
        ms = jnp.mean(x * x, axis=-1, keepdims=True)
        xn_ref[...] = (x * lax.rsqrt(ms + EPS) * g_ref[...]).astype(BF16)
        o_ref[...] = x

    f_begin, f_scores, f_attend, f_finish = _fox_sample_stages(
        fwq_ref, fnew_ref, gt_ref, grow_ref, tri_ref, fcache, lft, of_ref, fm_ref, fl_ref, facc_ref,
        fdcar_ref, t_new)
    s_begin, s_logits, s_later, s_attend, s_finish = _sb_sample_stages(
        swq_ref, snew_ref, tri_ref, scache, os_ref, sacc_ref, scar_ref, t_new)
    f_begin(first)
    s_begin(first)

    fs = f_scores()
    zs, l1ms, stacked = s_logits()
    hid = jnp.maximum(_dot(xn_ref[...], wu_ref[...]), 0.0)
    sr = s_later(stacked)
    f_attend(fs)
    o_ref[...] += _dot((hid * hid).astype(BF16), wd_ref[...])
    s_attend(zs, l1ms, sr)

    f_finish(last)
    s_finish(last)


def _mlp_attn(x, g, wu, wd, tm, tf, pt, fox, sb, tri, layer, t_new):
    m, d = x.shape
    f = wu.shape[2]
    fwq, fnew, gt, grow, fcache, lft_cache = fox
    swq, snew, scache = sb
    bs = fwq.shape[0]
    n_pages = pt.shape[0] // bs
    steps = (m // tm) * (f // tf)
    groups = steps // bs
    pages = n_pages // groups
    assert groups * bs == steps and pages * groups == n_pages, (steps, bs, n_pages)
    rows = t_new * N_HEADS
    nj = f // tf

    def seq_of(i, j):
        return lax.div(i * nj + j, groups)

    def page_spec(block, cache_layer, slot):
        def idx(i, j, pt):
            return (cache_layer, pt[(i * nj + j) * pages + slot], 0, 0)
        return pl.BlockSpec((None, None) + block, idx)

    per_seq = lambda r, c: pl.BlockSpec((None, r, c), lambda i, j, pt: (seq_of(i, j), 0, 0))
    const = lambda shp: pl.BlockSpec(shp, lambda i, j, pt: (0,) * len(shp))
    in_specs = ([pl.BlockSpec((tm, d), lambda i, j, pt: (i, 0)), const((1, d)),
                 pl.BlockSpec((None, d, tf), lambda i, j, pt: (layer, 0, j)),
                 pl.BlockSpec((None, tf, d), lambda i, j, pt: (layer, j, 0)),
                 per_seq(rows, D_GROUP), per_seq(2 * D_GROUP, PAGE), per_seq(N_HEADS, PAGE), per_seq(rows, 1),
                 const((PAGE, PAGE)), per_seq(rows, D_GROUP), per_seq(2 * D_GROUP, PAGE)]
                + [page_spec((2 * D_GROUP, PAGE), layer, s) for s in range(pages)]
                + [page_spec((N_HEADS, PAGE), layer, s) for s in range(pages)]
                + [page_spec((2 * D_GROUP, PAGE), layer, s) for s in range(pages)])
    grid_spec = pltpu.PrefetchScalarGridSpec(
        num_scalar_prefetch=1,
        grid=(m // tm, nj),
        in_specs=in_specs,
        out_specs=(pl.BlockSpec((tm, d), lambda i, j, pt: (i, 0), pipeline_mode=pl.Buffered(1)),
                   per_seq(t_new, D_GROUP), per_seq(t_new, D_GROUP)),
        scratch_shapes=[pltpu.VMEM((tm, d), BF16),
                        pltpu.VMEM((rows, 1), F32), pltpu.VMEM((rows, 1), F32),
                        pltpu.VMEM((rows, D_GROUP), F32), pltpu.VMEM((N_HEADS, 1), F32),
                        pltpu.VMEM((rows, D_GROUP), F32), pltpu.VMEM((rows, 1), F32)],
    )
    sds = jax.ShapeDtypeStruct
    return pl.pallas_call(
        functools.partial(_mlp_attn_kernel, pages=pages, groups=groups, t_new=t_new),
        out_shape=(sds((m, d), F32), sds((bs, t_new, D_GROUP), F32), sds((bs, t_new, D_GROUP), F32)),
        grid_spec=grid_spec,
        compiler_params=_cparams(("arbitrary", "arbitrary"), 60),
        name="mlp_attn",
    )(pt, x, g, wu, wd, fwq, fnew, gt, grow, tri, swq, snew,
      *((fcache,) * pages + (lft_cache,) * pages + (scache,) * pages))


def _block_diag(w):
    n, d, _ = w.shape
    eye = jnp.eye(n, dtype=w.dtype)
    return (eye[:, None, :, None] * w[:, :, None, :]).reshape(n * d, n * d)


def _input_weights(w_in):
    nf = 6 * D_GROUP
    pad = jnp.zeros(w_in.shape[:2] + (LANES - N_HEADS,), w_in.dtype)
    return jnp.concatenate([w_in[:, :, :nf], w_in[:, :, nf + N_HEADS:], w_in[:, :, nf:nf + N_HEADS], pad],
                           axis=2).astype(BF16)


def _layer_weights(l, norm1_g, pool_w, pool_scale, conv_w, conv_b, lru_wa, lru_ba, lru_wx, lru_bx,
                   lru_lambda, fox_bf, fox_q_g, fox_k_g, out_g, norm2_g):
    g = D_GROUP
    return dict(
        norm1_g=norm1_g[l][None, :],
        pool_w=pool_w[l].astype(BF16), pool_scale=pool_scale[l][None, :],
        conv_w=jnp.pad(conv_w[l], ((0, HIST_C - CONV_W), (0, 0))), conv_b=conv_b[l][None, :],
        wax=jnp.concatenate([_block_diag(lru_wa[l]), _block_diag(lru_wx[l])], axis=1).astype(BF16),
        ba=lru_ba[l][None, :], bx=lru_bx[l][None, :], lam=lru_lambda[l][None, :],
        bf=jnp.pad(fox_bf[l], (0, LANES - N_HEADS))[None, :],
        gq=jnp.tile(fox_q_g[l], N_HEADS)[None, :], gk=jnp.tile(fox_k_g[l], N_HEADS)[None, :],
        out_g=out_g[l].reshape(4, g), norm2_g=norm2_g[l][None, :],
    )


def _tile(m, pref):
    t = min(m, pref)
    while m % t:
        t //= 2
    return t


def _slot_constants():
    place = np.zeros((D_GROUP, D_AUG), np.float32)
    for c in range(D_GROUP):
        place[c, (c // HEAD_DIM) * LANES + c % HEAD_DIM] = 1.0
    placec = np.zeros((N_SPLIT, LANES, D_AUG), np.float32)
    neg = np.zeros((1, D_AUG), np.float32)
    for i in range(N_SPLIT):
        for h in range(N_HEADS):
            placec[i, h, h * LANES + HEAD_DIM + i] = 1.0
            neg[0, h * LANES + HEAD_DIM + i] = -1.0
    return jnp.asarray(place, BF16), jnp.asarray(placec, BF16), jnp.asarray(neg, F32)


def _expand_queries(q, b, t_new):
    q4 = q.reshape(b, t_new, 1, D_GROUP)
    own = (jnp.arange(D_GROUP)[None, :] // HEAD_DIM) == jnp.arange(N_HEADS)[:, None]
    return jnp.where(own[None, None], q4, jnp.zeros_like(q4)).reshape(b, t_new * N_HEADS, D_GROUP)


def _pages_view(cache):
    d, n = cache.shape[0], cache.shape[1]
    return jnp.transpose(cache, (0, 1, 3, 4, 5, 2)).reshape(d, n, 2 * D_GROUP, PAGE)


def kernel(x_prompt, x_sample, cache_fox_kv, cache_fox_logf, cache_sb_kv, state_pool, state_conv, state_lru,
           page_table, norm1_g, w_in, pool_w, pool_scale, conv_w, conv_b, lru_wa, lru_ba, lru_wx, lru_bx,
           lru_lambda, fox_bf, fox_q_g, fox_k_g, out_g, w_out, norm2_g, w_up, w_down):
    depth = w_in.shape[0]
    bp, seq, d_model = x_prompt.shape
    bs, t_new, _ = x_sample.shape
    past_len = page_table.shape[1] * PAGE
    g = D_GROUP

    seg = jnp.asarray(np.kron(np.eye(N_HEADS), np.ones((HEAD_DIM, HEAD_DIM))), BF16)
    place, placec, neg = _slot_constants()
    tk = _tile(seq, 256)
    tq = _tile(seq, 512)
    tri_q = jnp.asarray(np.triu(np.ones((tk, tk)), 1), BF16)
    tri_p = jnp.asarray(np.tril(np.ones((PAGE, PAGE)), -1), BF16)
    pt_flat = page_table[:, ::-1].reshape(-1).astype(jnp.int32)
    fox_cache = _pages_view(cache_fox_kv)
    sb_cache = _pages_view(cache_sb_kv)
    lft_cache = jnp.swapaxes(cache_fox_logf, 2, 3)
    t_pad = 8

    xp = x_prompt.reshape(bp * seq, d_model)
    xs = x_sample.reshape(bs * t_new, d_model)
    st_p, st_s = [], []
    w_in_b = _input_weights(w_in)
    w_out_b, w_up_b, w_down_b = w_out.astype(BF16), w_up.astype(BF16), w_down.astype(BF16)
    tf = _tile(w_up.shape[2], 512)
    for l in range(depth):
        w = _layer_weights(l, norm1_g, pool_w, pool_scale, conv_w, conv_b, lru_wa, lru_ba, lru_wx,
                           lru_bx, lru_lambda, fox_bf, fox_q_g, fox_k_g, out_g, norm2_g)

        m = bp * seq
        proj, flog = _inproj(xp, w["norm1_g"], w_in_b, l, _tile(m, 1024))
        r3 = lambda a: a.reshape(bp, seq, a.shape[-1])
        fkvt, skvt, lft, ct, fqt, fka, fvt, sqt, ska, svt = _prep_prompt(
            r3(proj), r3(flog), w["gq"], w["gk"], w["bf"], seg, place, placec, neg, _tile(seq, 256))
        y_fox = _prompt_attention("fox", fqt, fka, fvt, ct, tq, tk)
        y_sb = _prompt_attention("sb", sqt, ska, svt, tri_q, tq, tk)
        tt = _tile(seq, 256)
        y_pool, y_lru, pnew, cnew, hnew = _seqmix(
            r3(proj), jnp.zeros((bp, HIST_P, g), F32), jnp.zeros((bp, HIST_C, g), F32),
            jnp.zeros((bp, 1, g), F32), w["pool_w"], w["pool_scale"], w["conv_w"], w["conv_b"], w["wax"],
            w["ba"], w["bx"], w["lam"], tt, tt, 0)
        flat = lambda a: a.reshape(m, g)
        x1p = _outproj(xp, (flat(y_pool), flat(y_lru), flat(y_fox), flat(y_sb)), w["out_g"], w_out_b, l,
                       _tile(m, 512))
        kv_state = lambda a: jnp.transpose(a.reshape(bp, 2, N_HEADS, HEAD_DIM, seq), (0, 4, 1, 2, 3))
        st_p.append((kv_state(fkvt), jnp.swapaxes(lft, 1, 2), kv_state(skvt), pnew[:, 1:],
                     cnew[:, HIST_C - CONV_W + 1:], hnew[:, 0]))

        ms = bs * t_new
        proj, flog = _inproj(xs, w["norm1_g"], w_in_b, l, ms)
        fkv, skv, lf, c, ct, qf, qs = _prep_sample(proj, flog, w["gq"], w["gk"], w["bf"], seg, t_new)
        new_page = lambda a: jnp.pad(jnp.swapaxes(a.reshape(bs, t_new, 2 * g), 1, 2),
                                     ((0, 0), (0, 0), (0, PAGE - t_new)))
        gt = jnp.pad(jnp.swapaxes(ct.reshape(N_HEADS, bs, t_new), 0, 1), ((0, 0), (0, 0), (0, PAGE - t_new)))
        grow = c[:, :N_HEADS].reshape(bs, t_new * N_HEADS, 1)
        xp, y_fox, y_sb = _mlp_attn(
            x1p, w["norm2_g"], w_up_b, w_down_b, _tile(m, 1024), tf, pt_flat,
            (_expand_queries(qf, bs, t_new), new_page(fkv), gt, grow, fox_cache, lft_cache),
            (_expand_queries(qs, bs, t_new), new_page(skv), sb_cache), tri_p, l, t_new)
        proj3 = jnp.pad(proj.reshape(bs, t_new, -1), ((0, 0), (0, t_pad - t_new), (0, 0)))
        y_pool, y_lru, pnew, cnew, hnew = _seqmix(
            proj3, jnp.pad(state_pool[l], ((0, 0), (HIST_P - POOL_MAX + 1, 0), (0, 0))),
            jnp.pad(state_conv[l], ((0, 0), (HIST_C - CONV_W + 1, 0), (0, 0))), state_lru[l][:, None, :],
            w["pool_w"], w["pool_scale"], w["conv_w"], w["conv_b"], w["wax"], w["ba"], w["bx"], w["lam"],
            t_pad, t_new, past_len)
        cut = lambda a: a[:, :t_new].reshape(ms, g)
        x1s = _outproj(xs, (cut(y_pool), cut(y_lru), y_fox.reshape(ms, g), y_sb.reshape(ms, g)), w["out_g"],
                       w_out_b, l, ms)
        xs = _mlp(x1s, w["norm2_g"], w_up_b, w_down_b, l, ms, tf)
        st_s.append((fkv.reshape(bs, t_new, 2, N_HEADS, HEAD_DIM), lf[:, :N_HEADS].reshape(bs, t_new, N_HEADS),
                     skv.reshape(bs, t_new, 2, N_HEADS, HEAD_DIM), pnew[:, 1:], cnew[:, HIST_C - CONV_W + 1:],
                     hnew[:, 0]))

    stk = lambda sts, j: jnp.stack([s[j] for s in sts], axis=0)
    return ((xp.reshape(bp, seq, d_model), xs.reshape(bs, t_new, d_model))
            + tuple(stk(st_p, j) for j in range(6)) + tuple(stk(st_s, j) for j in range(6)))
```

```python
import functools

import numpy as np
import jax
import jax.numpy as jnp
from jax import lax
from jax.experimental import pallas as pl
from jax.experimental.pallas import tpu as pltpu

F32 = jnp.float32
BF16 = jnp.bfloat16

D_GROUP = 512
HEAD_DIM = 64
N_HEADS = D_GROUP // HEAD_DIM
POOL_WINDOWS = (2, 4, 8, 16)
POOL_MAX = max(POOL_WINDOWS)
POOL_CH = D_GROUP // len(POOL_WINDOWS)
CONV_W = 4
LRU_C = 8.0
EPS = 1e-6
PAGE = 128
LANES = 128
D_AUG = N_HEADS * LANES
N_SPLIT = 3
HIST_P = 16
HIST_C = 8
NEG_LOG2E = -1.4426950408889634
NEG_BIG = -1e30
PAGES_PER_STEP = 8
MIB = 1024 * 1024


def _cparams(sem, vmem_mib=48):
    return pltpu.CompilerParams(dimension_semantics=sem, vmem_limit_bytes=vmem_mib * MIB)


def _split_bf16(x, parts):
    out = []
    r = x
    for _ in range(parts - 1):
        h = r.astype(BF16)
        out.append(h)
        r = r - h.astype(F32)
    out.append(r.astype(BF16))
    return out


def _softplus(x):
    return jnp.maximum(x, 0.0) + jnp.log1p(jnp.exp(-jnp.abs(x)))


def _log_sigmoid(x):
    return -_softplus(-x)


def _softplus_exp2(z):
    return jnp.maximum(z, 0.0) + jnp.log(1.0 + jnp.exp2(jnp.abs(z) * NEG_LOG2E))


def _neg_softplus(z):
    return -(jnp.maximum(z, 0.0) + jnp.log(1.0 + jnp.exp(-jnp.abs(z))))


def _dot_nt(a, b):
    return lax.dot_general(a, b, (((1,), (1,)), ((), ())), preferred_element_type=F32)


def _dot(a, b):
    return jnp.dot(a, b, preferred_element_type=F32)


def _inproj_kernel(x_ref, g_ref, w_ref, wf_ref, o_ref, of_ref, xn_ref):
    @pl.when(pl.program_id(1) == 0)
    def _():
        x = x_ref[...]
        ms = jnp.mean(x * x, axis=-1, keepdims=True)
        xn = (x * lax.rsqrt(ms + EPS) * g_ref[...]).astype(BF16)
        xn_ref[...] = xn
        of_ref[...] = _dot_nt(xn, wf_ref[...])

    o_ref[...] = _dot_nt(xn_ref[...], w_ref[...])


def _inproj(x, g, w, layer, tm):
    m, d = x.shape
    n = w.shape[1] - LANES
    tn = D_GROUP
    return pl.pallas_call(
        _inproj_kernel,
        out_shape=(jax.ShapeDtypeStruct((m, n), F32), jax.ShapeDtypeStruct((m, LANES), F32)),
        grid=(m // tm, n // tn),
        in_specs=[
            pl.BlockSpec((tm, d), lambda i, j: (i, 0)),
            pl.BlockSpec((1, d), lambda i, j: (0, 0)),
            pl.BlockSpec((None, tn, d), lambda i, j: (layer, j, 0)),
            pl.BlockSpec((None, LANES, d), lambda i, j: (layer, n // LANES, 0)),
        ],
        out_specs=(
            pl.BlockSpec((tm, tn), lambda i, j: (i, j)),
            pl.BlockSpec((tm, LANES), lambda i, j: (i, 0)),
        ),
        scratch_shapes=[pltpu.VMEM((tm, d), BF16)],
        compiler_params=_cparams(("arbitrary", "arbitrary")),
        name="inproj",
    )(x, g, w, w)


def _head_norm(x, g, seg_ref):
    hi, lo = _split_bf16(x * x, 2)
    ss = _dot(hi, seg_ref[...]) + _dot(lo, seg_ref[...])
    return x * lax.rsqrt(ss * (1.0 / HEAD_DIM) + EPS) * g


def _running_sum(lf, seq, tm):
    r = lax.broadcasted_iota(jnp.int32, (tm, tm), 0)
    c = lax.broadcasted_iota(jnp.int32, (tm, tm), 1)
    keep = c <= r
    if seq < tm:
        keep = jnp.logical_and(keep, (r // seq) == (c // seq))
    tri = jnp.where(keep, 1.0, 0.0).astype(BF16)
    cs = None
    for part in _split_bf16(lf, N_SPLIT):
        t = _dot(tri, part)
        cs = t if cs is None else cs + t
    return cs


def _prep_prompt_kernel(fq_ref, fk_ref, fv_ref, sq_ref, sk_ref, sv_ref, fl_ref, gq_ref, gk_ref, bf_ref,
                        seg_ref, place_ref, placec_ref, neg_ref,
                        fkvt_ref, skvt_ref, lft_ref, ct_ref,
                        fqt_ref, fka_ref, fvt_ref, sqt_ref, ska_ref, svt_ref, carry_ref, *, tm):
    scale = HEAD_DIM ** -0.5
    fq = _head_norm(fq_ref[...], gq_ref[...], seg_ref)
    fk = _head_norm(fk_ref[...], gk_ref[...], seg_ref)
    fv = fv_ref[...]
    sk = sk_ref[...]
    sv = sv_ref[...]

    fvt = fv.T
    svt = sv.T
    fkvt_ref[0:D_GROUP, :] = fk.T
    fkvt_ref[D_GROUP:2 * D_GROUP, :] = fvt
    skvt_ref[0:D_GROUP, :] = sk.T
    skvt_ref[D_GROUP:2 * D_GROUP, :] = svt
    fvt_ref[...] = fvt.astype(BF16)
    svt_ref[...] = svt.astype(BF16)

    lf = _log_sigmoid(fl_ref[...] + bf_ref[...])
    cs = _running_sum(lf, tm, tm)

    @pl.when(pl.program_id(1) == 0)
    def _():
        carry_ref[...] = jnp.zeros_like(carry_ref)
    cs = cs + carry_ref[...]
    carry_ref[...] = cs[tm - 1:tm, :]
    lft_ref[...] = lf.T[0:N_HEADS, :]
    ct_ref[...] = cs.T[0:N_HEADS, :]

    ka = _dot(fk.astype(BF16), place_ref[...])
    for i, part in enumerate(_split_bf16(cs, N_SPLIT)):
        ka = ka + _dot(part, placec_ref[i])
    fka_ref[...] = ka.astype(BF16)
    qa = _dot((fq * scale).astype(BF16), place_ref[...]) + neg_ref[...]
    fqt_ref[...] = qa.T.astype(BF16)
    ska_ref[...] = sk.astype(BF16)
    sqt_ref[...] = (sq_ref[...] * scale).T.astype(BF16)


def _prep_prompt(proj3, flog3, gq, gk, bf, seg, place, placec, neg, tm):
    b, t, _ = proj3.shape
    col = lambda s: pl.BlockSpec((None, tm, D_GROUP), lambda bi, ti, s=s: (bi, ti, s))
    const = lambda shp: pl.BlockSpec(shp, lambda bi, ti: (0,) * len(shp))
    tr = lambda rows: pl.BlockSpec((None, rows, tm), lambda bi, ti: (bi, 0, ti))
    nt = lambda cols: pl.BlockSpec((None, tm, cols), lambda bi, ti: (bi, ti, 0))
    sds = jax.ShapeDtypeStruct
    return pl.pallas_call(
        functools.partial(_prep_prompt_kernel, tm=tm),
        out_shape=(
            sds((b, 2 * D_GROUP, t), F32), sds((b, 2 * D_GROUP, t), F32),
            sds((b, N_HEADS, t), F32), sds((b, N_HEADS, t), F32),
            sds((b, D_AUG, t), BF16), sds((b, t, D_AUG), BF16), sds((b, D_GROUP, t), BF16),
            sds((b, D_GROUP, t), BF16), sds((b, t, D_GROUP), BF16), sds((b, D_GROUP, t), BF16),
        ),
        grid=(b, t // tm),
        in_specs=[col(3), col(4), col(5), col(6), col(7), col(8),
                  pl.BlockSpec((None, tm, LANES), lambda bi, ti: (bi, ti, 0)),
                  const((1, D_GROUP)), const((1, D_GROUP)), const((1, LANES)),
                  const((D_GROUP, D_GROUP)), const((D_GROUP, D_AUG)), const((N_SPLIT, LANES, D_AUG)),
                  const((1, D_AUG))],
        out_specs=(tr(2 * D_GROUP), tr(2 * D_GROUP), tr(N_HEADS), tr(N_HEADS),
                   tr(D_AUG), nt(D_AUG), tr(D_GROUP), tr(D_GROUP), nt(D_GROUP), tr(D_GROUP)),
        scratch_shapes=[pltpu.VMEM((1, LANES), F32)],
        compiler_params=_cparams(("arbitrary", "arbitrary")),
        name="prep_prompt",
    )(proj3, proj3, proj3, proj3, proj3, proj3, flog3, gq, gk, bf, seg, place, placec, neg)


def _prep_sample_kernel(fq_ref, fk_ref, fv_ref, sq_ref, sk_ref, sv_ref, fl_ref, gq_ref, gk_ref, bf_ref,
                        seg_ref, fkv_ref, skv_ref, lf_ref, c_ref, ct_ref, qf_ref, qs_ref, *, seq, tm):
    scale = HEAD_DIM ** -0.5
    fq = _head_norm(fq_ref[...], gq_ref[...], seg_ref)
    fk = _head_norm(fk_ref[...], gk_ref[...], seg_ref)
    qf_ref[...] = (fq * scale).astype(BF16)
    qs_ref[...] = (sq_ref[...] * scale).astype(BF16)
    fkv_ref[:, 0:D_GROUP] = fk
    fkv_ref[:, D_GROUP:2 * D_GROUP] = fv_ref[...]
    skv_ref[:, 0:D_GROUP] = sk_ref[...]
    skv_ref[:, D_GROUP:2 * D_GROUP] = sv_ref[...]
    lf = _log_sigmoid(fl_ref[...] + bf_ref[...])
    lf_ref[...] = lf
    cs = _running_sum(lf, seq, tm)
    c_ref[...] = cs
    ct_ref[...] = cs.T[0:N_HEADS, :]


def _prep_sample(proj, flog, gq, gk, bf, seg, seq):
    m = proj.shape[0]
    col = lambda s: pl.BlockSpec((m, D_GROUP), lambda i, s=s: (0, s))
    full = lambda shp: pl.BlockSpec(shp, lambda i: (0,) * len(shp))
    sds = jax.ShapeDtypeStruct
    return pl.pallas_call(
        functools.partial(_prep_sample_kernel, seq=seq, tm=m),
        out_shape=(sds((m, 2 * D_GROUP), F32), sds((m, 2 * D_GROUP), F32), sds((m, LANES), F32),
                   sds((m, LANES), F32), sds((N_HEADS, m), F32), sds((m, D_GROUP), BF16),
                   sds((m, D_GROUP), BF16)),
        grid=(1,),
        in_specs=[col(3), col(4), col(5), col(6), col(7), col(8), full((m, LANES)),
                  full((1, D_GROUP)), full((1, D_GROUP)), full((1, LANES)), full((D_GROUP, D_GROUP))],
        out_specs=(full((m, 2 * D_GROUP)), full((m, 2 * D_GROUP)), full((m, LANES)), full((m, LANES)),
                   full((N_HEADS, m)), full((m, D_GROUP)), full((m, D_GROUP))),
        compiler_params=_cparams(("arbitrary",)),
        name="prep_sample",
    )(proj, proj, proj, proj, proj, proj, flog, gq, gk, bf, seg)


def _seqmix_kernel(u_ref, x_ref, gate_ref, pprev_ref, cprev_ref, h0_ref,
                   pw_ref, pscale_ref, cw_ref, cb_ref, wax_ref, ba_ref, bx_ref, lam_ref,
                   yp_ref, yl_ref, pnew_ref, cnew_ref, hnew_ref,
                   extp_ref, extc_ref, h_ref, *, tt, tv, pos0):
    ti = pl.program_id(1)

    @pl.when(ti == 0)
    def _():
        extp_ref[0:HIST_P, :] = pprev_ref[...]
        extc_ref[0:HIST_C, :] = cprev_ref[...]
        h_ref[...] = h0_ref[...]

    extp_ref[HIST_P:HIST_P + tt, :] = u_ref[...]
    extc_ref[HIST_C:HIST_C + tt, :] = x_ref[...]
    pos = pos0 + ti * tt + lax.broadcasted_iota(jnp.int32, (tt, 1), 0)

    for g, w in enumerate(POOL_WINDOWS):
        lanes = slice(g * POOL_CH, (g + 1) * POOL_CH)
        tok = extp_ref[HIST_P:HIST_P + tt, lanes]
        win = tok
        for j in range(1, w):
            win = win + extp_ref[HIST_P - j:HIST_P - j + tt, lanes]
        cnt = jnp.minimum(w, pos + 1).astype(F32)
        d = win / cnt - tok
        y = _dot(d.astype(BF16), pw_ref[g]) * pscale_ref[:, lanes]
        yp_ref[:, lanes] = y
    hist = extp_ref[tv:tv + HIST_P, :]
    extp_ref[0:HIST_P, :] = hist
    pnew_ref[...] = hist

    base = HIST_C - (CONV_W - 1)
    xc = extc_ref[base:base + tt, :] * cw_ref[0:1, :]
    for k in range(1, CONV_W):
        xc = xc + extc_ref[base + k:base + k + tt, :] * cw_ref[k:k + 1, :]
    xc = cb_ref[...] + xc
    chist = extc_ref[tv:tv + HIST_C, :]
    extc_ref[0:HIST_C, :] = chist
    cnew_ref[...] = chist

    ri = _dot(xc.astype(BF16), wax_ref[...])
    r = jax.nn.sigmoid(ri[:, 0:D_GROUP] + ba_ref[...])
    gi = jax.nn.sigmoid(ri[:, D_GROUP:2 * D_GROUP] + bx_ref[...])
    log_a = -LRU_C * r * _softplus(-lam_ref[...])
    a = jnp.exp(log_a)
    mult = jnp.sqrt(-jnp.tanh(log_a) * (a * a + 1.0))
    mult = jnp.where(pos == 0, 1.0, mult)
    b = mult * (gi * xc)

    rows = lax.broadcasted_iota(jnp.int32, (tt, 1), 0)
    d = 1
    while d < tt:
        ok = rows >= d
        a_s = jnp.where(ok, pltpu.roll(a, d, 0), 1.0)
        b_s = jnp.where(ok, pltpu.roll(b, d, 0), 0.0)
        b = a * b_s + b
        a = a * a_s
        d *= 2
    h = b + a * h_ref[...]
    hlast = h[tv - 1:tv, :]
    h_ref[...] = hlast
    hnew_ref[...] = hlast
    yl_ref[...] = jax.nn.gelu(gate_ref[...]) * h


def _seqmix(proj3, pprev, cprev, h0, pw, pscale, cw, cb, wax, ba, bx, lam, tt, tv, pos0):
    b, t, _ = proj3.shape
    nt = t // tt
    col = lambda s: pl.BlockSpec((None, tt, D_GROUP), lambda bi, ti, s=s: (bi, ti, s))
    per_b = lambda r: pl.BlockSpec((None, r, D_GROUP), lambda bi, ti: (bi, 0, 0))
    const2 = lambda shp: pl.BlockSpec(shp, lambda bi, ti: (0, 0))
    return pl.pallas_call(
        functools.partial(_seqmix_kernel, tt=tt, tv=tv, pos0=pos0),
        out_shape=(
            jax.ShapeDtypeStruct((b, t, D_GROUP), F32),
            jax.ShapeDtypeStruct((b, t, D_GROUP), F32),
            jax.ShapeDtypeStruct((b, HIST_P, D_GROUP), F32),
            jax.ShapeDtypeStruct((b, HIST_C, D_GROUP), F32),
            jax.ShapeDtypeStruct((b, 1, D_GROUP), F32),
        ),
        grid=(b, nt),
        in_specs=[col(0), col(1), col(2), per_b(HIST_P), per_b(HIST_C), per_b(1),
                  pl.BlockSpec((len(POOL_WINDOWS), POOL_CH, POOL_CH), lambda bi, ti: (0, 0, 0)),
                  const2((1, D_GROUP)), const2((HIST_C, D_GROUP)), const2((1, D_GROUP)),
                  const2((D_GROUP, 2 * D_GROUP)), const2((1, D_GROUP)), const2((1, D_GROUP)),
                  const2((1, D_GROUP))],
        out_specs=(
            pl.BlockSpec((None, tt, D_GROUP), lambda bi, ti: (bi, ti, 0)),
            pl.BlockSpec((None, tt, D_GROUP), lambda bi, ti: (bi, ti, 0)),
            per_b(HIST_P), per_b(HIST_C), per_b(1),
        ),
        scratch_shapes=[pltpu.VMEM((HIST_P + tt, D_GROUP), F32),
                        pltpu.VMEM((HIST_C + tt, D_GROUP), F32),
                        pltpu.VMEM((1, D_GROUP), F32)],
        compiler_params=_cparams(("arbitrary", "arbitrary")),
        name="seqmix",
    )(proj3, proj3, proj3, pprev, cprev, h0, pw, pscale, cw, cb, wax, ba, bx, lam)


def _pair_tables(n, ratio, reverse):
    qi, ki = [], []
    for q in range(n):
        ks = range((q + 1) * ratio)
        for k in (reversed(ks) if reverse else ks):
            qi.append(q)
            ki.append(k)
    return jnp.asarray(np.array(qi, np.int32)), jnp.asarray(np.array(ki, np.int32))


def _key_minus_query(tk, tq):
    return (lax.broadcasted_iota(jnp.int32, (tk, tq), 0) - lax.broadcasted_iota(jnp.int32, (tk, tq), 1))


def _fox_prompt_kernel(qi_ref, ki_ref, qt_ref, k_ref, vt_ref, cq_ref, o_ref, m_ref, l_ref, acc_ref, *, tq, tk):
    s_id = pl.program_id(1)
    qi = qi_ref[s_id]
    ki = ki_ref[s_id]
    ratio = tq // tk

    @pl.when(ki == 0)
    def _():
        m_ref[...] = jnp.full_like(m_ref, NEG_BIG)
        l_ref[...] = jnp.zeros_like(l_ref)
        acc_ref[...] = jnp.zeros_like(acc_ref)

    def step(masked, first=0):
        win = slice(first, tq)
        if masked:
            causal = _key_minus_query(tk, tq - first) <= 0

        def scores(h):
            slot = slice(h * LANES, (h + 1) * LANES)
            return _dot(k_ref[:, slot], qt_ref[slot, win])

        def softmax_update(h, st):
            if masked:
                st = jnp.where(causal, st, NEG_BIG)
            cq = cq_ref[h:h + 1, win]
            m_old = m_ref[h:h + 1, win]
            m_new = jnp.maximum(m_old, jnp.max(st, axis=0, keepdims=True) + cq)
            alpha = jnp.exp(m_old - m_new)
            pt = jnp.exp(st - (m_new - cq))
            l_ref[h:h + 1, win] = alpha * l_ref[h:h + 1, win] + jnp.sum(pt, axis=0, keepdims=True)
            m_ref[h:h + 1, win] = m_new
            return alpha, pt.astype(BF16)

        st = {0: scores(0), 1: scores(1)}
        for h in range(N_HEADS):
            rows = slice(h * HEAD_DIM, (h + 1) * HEAD_DIM)
            if h + 2 < N_HEADS:
                st[h + 2] = scores(h + 2)
            alpha, pt = softmax_update(h, st.pop(h))
            acc_ref[rows, win] = alpha * acc_ref[rows, win] + _dot(vt_ref[rows, :], pt)

    @pl.when(ki < qi * ratio)
    def _():
        step(False)

    for diag in range(ratio):
        @pl.when(ki == qi * ratio + diag)
        def _(diag=diag):
            step(True, diag * tk)

    @pl.when(ki == (qi + 1) * ratio - 1)
    def _():
        for h in range(N_HEADS):
            rows = slice(h * HEAD_DIM, (h + 1) * HEAD_DIM)
            acc_ref[rows, :] = acc_ref[rows, :] / l_ref[h:h + 1, :]
        o_ref[...] = acc_ref[...].T


def _sb_prompt_kernel(qi_ref, ki_ref, pt_ref, qt_ref, k_ref, vt_ref, tri_ref, fwq_ref, fnew_ref, gt_ref,
                      grow_ref, trip_ref, swq_ref, snew_ref, *rest, tq, tk, pages, groups, tasks, t_new):
    fcache = rest[0:pages]
    lft = rest[pages:2 * pages]
    scache = rest[2 * pages:3 * pages]
    (o_ref, of_ref, os_ref, carry_ref, acc_ref, fm_ref, fl_ref, facc_ref, fdcar_ref, sacc_ref,
     scar_ref) = rest[3 * pages:]
    s_id = pl.program_id(1)
    qi = qi_ref[s_id]
    ki = ki_ref[s_id]
    ratio = tq // tk

    n = pl.program_id(0) * pl.num_programs(1) + s_id
    valid = n < tasks
    grp = lax.rem(jnp.minimum(n, tasks - 1), groups)
    f_begin, f_scores, f_attend, f_finish = _fox_sample_stages(
        fwq_ref, fnew_ref, gt_ref, grow_ref, trip_ref, fcache, lft, of_ref, fm_ref, fl_ref, facc_ref,
        fdcar_ref, t_new)
    s_begin, s_logits, s_later, s_attend, s_finish = _sb_sample_stages(
        swq_ref, snew_ref, trip_ref, scache, os_ref, sacc_ref, scar_ref, t_new)
    f_begin(jnp.logical_and(valid, grp == 0))
    s_begin(jnp.logical_and(valid, grp == 0))

    @pl.when(ki == (qi + 1) * ratio - 1)
    def _():
        carry_ref[...] = jnp.zeros_like(carry_ref)
        acc_ref[...] = jnp.zeros_like(acc_ref)

    def step(masked, first=0):
        win = slice(first, tq)
        if masked:
            strict = _key_minus_query(tk, tq - first) < 0

        def logits(h):
            pair = slice((h // 2) * LANES, (h // 2 + 1) * LANES)
            qh = qt_ref[h * HEAD_DIM:(h + 1) * HEAD_DIM, win]
            zero = jnp.zeros_like(qh)
            qpair = jnp.concatenate([qh, zero] if h % 2 == 0 else [zero, qh], axis=0)
            return _dot(k_ref[:, pair], qpair)

        def neg_log_one_minus_beta(zt):
            sp = _softplus_exp2(zt)
            if masked:
                sp = jnp.where(strict, sp, 0.0)
            return (sp,) + tuple(_split_bf16(sp, 2))

        def later_keys_sum(hi, lo):
            return _dot(tri_ref[...], hi) + _dot(tri_ref[...], lo)

        def weights(zt, sp, rest):
            wgt = jnp.exp(zt - sp - rest)
            if masked:
                wgt = jnp.where(strict, wgt, 0.0)
            return wgt.astype(BF16)

        fs = f_scores(valid)
        zs, l1ms, stacked = s_logits()
        zt = {0: logits(0), 1: logits(1)}
        lg = {0: neg_log_one_minus_beta(zt[0])}
        rest = {}
        for h in range(N_HEADS + 1):
            if h == 2:
                sr = s_later(stacked)
            elif h == 4:
                f_attend(fs, valid)
            elif h == 6:
                s_attend(zs, l1ms, sr, valid)
            if h < N_HEADS:
                rest[h] = later_keys_sum(*lg[h][1:])
            if h + 2 < N_HEADS:
                zt[h + 2] = logits(h + 2)
            if h >= 1:
                g = h - 1
                rows = slice(g * HEAD_DIM, (g + 1) * HEAD_DIM)
                sp, rs = lg.pop(g)[0], rest.pop(g)
                carry = carry_ref[g:g + 1, win]
                pv = _dot(vt_ref[rows, :], weights(zt.pop(g), sp, rs))
                acc_ref[rows, win] = acc_ref[rows, win] + pv * jnp.exp(-carry)
                carry_ref[g:g + 1, win] = carry + rs[0:1, :] + sp[0:1, :]
            if h + 1 < N_HEADS:
                lg[h + 1] = neg_log_one_minus_beta(zt[h + 1])

    for diag in range(ratio):
        @pl.when(ki == qi * ratio + diag)
        def _(diag=diag):
            step(True, diag * tk)

    @pl.when(ki < qi * ratio)
    def _():
        step(False)

    @pl.when(ki == 0)
    def _():
        o_ref[...] = acc_ref[...].T

    f_finish(jnp.logical_and(valid, grp == groups - 1))
    s_finish(jnp.logical_and(valid, grp == groups - 1))


def _fox_prompt_attention(qt, k, vt, cq, tq, tk):
    b, t, d_qk = k.shape
    qi, ki = _pair_tables(t // tq, tq // tk, reverse=False)
    grid_spec = pltpu.PrefetchScalarGridSpec(
        num_scalar_prefetch=2,
        grid=(b, int(qi.shape[0])),
        in_specs=[pl.BlockSpec((None, d_qk, tq), lambda bi, s, qi, ki: (bi, 0, qi[s])),
                  pl.BlockSpec((None, tk, d_qk), lambda bi, s, qi, ki: (bi, ki[s], 0)),
                  pl.BlockSpec((None, D_GROUP, tk), lambda bi, s, qi, ki: (bi, 0, ki[s])),
                  pl.BlockSpec((None, N_HEADS, tq), lambda bi, s, qi, ki: (bi, 0, qi[s]))],
        out_specs=pl.BlockSpec((None, tq, D_GROUP), lambda bi, s, qi, ki: (bi, qi[s], 0)),
        scratch_shapes=[pltpu.VMEM((N_HEADS, tq), F32), pltpu.VMEM((N_HEADS, tq), F32),
                        pltpu.VMEM((D_GROUP, tq), F32)],
    )
    return pl.pallas_call(
        functools.partial(_fox_prompt_kernel, tq=tq, tk=tk),
        out_shape=jax.ShapeDtypeStruct((b, t, D_GROUP), F32),
        grid_spec=grid_spec,
        compiler_params=_cparams(("arbitrary", "arbitrary")),
        name="fox_prompt",
    )(qi, ki, qt, k, vt, cq)


def _sb_prompt_attention(qt, k, vt, tri, tq, tk, pt, fox, sb, tri_page, layer, t_new, pages):
    b, t, d_qk = k.shape
    qi, ki = _pair_tables(t // tq, tq // tk, reverse=True)
    n_steps = int(qi.shape[0])
    fwq, fnew, gt, grow, fcache, lft_cache = fox
    swq, snew, scache = sb
    bs = fwq.shape[0]
    n_pages = pt.shape[0] // bs
    groups = n_pages // pages
    tasks = bs * groups
    assert groups * pages == n_pages and tasks <= b * n_steps, (n_pages, pages, tasks, b * n_steps)
    rows = t_new * N_HEADS

    def task_of(bi, s):
        return jnp.minimum(bi * n_steps + s, tasks - 1)

    def page_spec(block, slot):
        return pl.BlockSpec((None, None) + block,
                            lambda bi, s, qi, ki, pt: (layer, pt[task_of(bi, s) * pages + slot], 0, 0))

    per_seq = lambda r, c: pl.BlockSpec(
        (None, r, c), lambda bi, s, qi, ki, pt: (lax.div(task_of(bi, s), groups), 0, 0))
    const = lambda shp: pl.BlockSpec(shp, lambda bi, s, qi, ki, pt: (0,) * len(shp))
    in_specs = ([pl.BlockSpec((None, d_qk, tq), lambda bi, s, qi, ki, pt: (bi, 0, qi[s])),
                 pl.BlockSpec((None, tk, d_qk), lambda bi, s, qi, ki, pt: (bi, ki[s], 0)),
                 pl.BlockSpec((None, D_GROUP, tk), lambda bi, s, qi, ki, pt: (bi, 0, ki[s])),
                 const((tk, tk)),
                 per_seq(rows, D_GROUP), per_seq(2 * D_GROUP, PAGE), per_seq(N_HEADS, PAGE), per_seq(rows, 1),
                 const((PAGE, PAGE)), per_seq(rows, D_GROUP), per_seq(2 * D_GROUP, PAGE)]
                + [page_spec((2 * D_GROUP, PAGE), slot) for slot in range(pages)]
                + [page_spec((N_HEADS, PAGE), slot) for slot in range(pages)]
                + [page_spec((2 * D_GROUP, PAGE), slot) for slot in range(pages)])
    grid_spec = pltpu.PrefetchScalarGridSpec(
        num_scalar_prefetch=3,
        grid=(b, n_steps),
        in_specs=in_specs,
        out_specs=(pl.BlockSpec((None, tq, D_GROUP), lambda bi, s, qi, ki, pt: (bi, qi[s], 0)),
                   per_seq(t_new, D_GROUP), per_seq(t_new, D_GROUP)),
        scratch_shapes=[pltpu.VMEM((N_HEADS, tq), F32), pltpu.VMEM((D_GROUP, tq), F32),
                        pltpu.VMEM((rows, 1), F32), pltpu.VMEM((rows, 1), F32),
                        pltpu.VMEM((rows, D_GROUP), F32), pltpu.VMEM((N_HEADS, 1), F32),
                        pltpu.VMEM((rows, D_GROUP), F32), pltpu.VMEM((rows, 1), F32)],
    )
    sds = jax.ShapeDtypeStruct
    return pl.pallas_call(
        functools.partial(_sb_prompt_kernel, tq=tq, tk=tk, pages=pages, groups=groups, tasks=tasks,
                          t_new=t_new),
        out_shape=(sds((b, t, D_GROUP), F32), sds((bs, t_new, D_GROUP), F32), sds((bs, t_new, D_GROUP), F32)),
        grid_spec=grid_spec,
        compiler_params=_cparams(("arbitrary", "arbitrary")),
        name="sb_prompt_attn",
    )(qi, ki, pt, qt, k, vt, tri, fwq, fnew, gt, grow, tri_page, swq, snew,
      *((fcache,) * pages + (lft_cache,) * pages + (scache,) * pages))


def _extract_heads(acc, t_new):
    hrow = lax.broadcasted_iota(jnp.int32, (N_HEADS, D_GROUP), 0)
    hcol = lax.broadcasted_iota(jnp.int32, (N_HEADS, D_GROUP), 1) // HEAD_DIM
    own = hrow == hcol
    outs = []
    for t in range(t_new):
        blk = acc[t * N_HEADS:(t + 1) * N_HEADS, :]
        outs.append(jnp.sum(jnp.where(own, blk, 0.0), axis=0, keepdims=True))
    return outs


def _fox_sample_stages(wq_ref, new_ref, gt_ref, grow_ref, tri_ref, cache_refs, lft_refs,
                       o_ref, m_ref, l_ref, acc_ref, dcar_ref, t_new):
    rows = t_new * N_HEADS

    def scores(kv_refs, biases, mask):
        s = []
        for kv_ref, bias in zip(kv_refs, biases):
            kt = kv_ref[0:D_GROUP, :].astype(BF16)
            s.append(_dot(wq_ref[...], kt) + jnp.concatenate([bias] * t_new, axis=0) + grow_ref[...])
        s = s[0] if len(s) == 1 else jnp.concatenate(s, axis=1)
        if mask is not None:
            s = jnp.where(mask, s, NEG_BIG)
        return s

    def keep(valid, new, old):
        return new if valid is None else jnp.where(valid, new, old)

    def attend(kv_refs, s, valid=None):
        m_old = m_ref[...]
        m_new = jnp.maximum(m_old, jnp.max(s, axis=-1, keepdims=True))
        alpha = jnp.exp(m_old - m_new)
        pr = jnp.exp(s - m_new)
        l_old = l_ref[...]
        l_ref[...] = keep(valid, alpha * l_old + jnp.sum(pr, axis=-1, keepdims=True), l_old)
        pv = None
        for i, kv_ref in enumerate(kv_refs):
            vt = kv_ref[D_GROUP:2 * D_GROUP, :].astype(BF16)
            t = _dot_nt(pr[:, i * PAGE:(i + 1) * PAGE].astype(BF16), vt)
            pv = t if pv is None else pv + t
        acc_old = acc_ref[...]
        acc_ref[...] = keep(valid, alpha * acc_old + pv, acc_old)
        m_ref[...] = keep(valid, m_new, m_old)

    def begin(first):
        @pl.when(first)
        def _():
            m_ref[...] = jnp.full_like(m_ref, NEG_BIG)
            l_ref[...] = jnp.zeros_like(l_ref)
            acc_ref[...] = jnp.zeros_like(acc_ref)
            dcar_ref[...] = jnp.zeros_like(dcar_ref)
            tok = lax.broadcasted_iota(jnp.int32, (rows, PAGE), 0) // N_HEADS
            key = lax.broadcasted_iota(jnp.int32, (rows, PAGE), 1)
            attend([new_ref], scores([new_ref], [-gt_ref[...]], key <= tok))

    def past_scores(valid=None):
        lfs = [r[...] for r in lft_refs]
        parts = []
        for lf in lfs:
            parts += [x.astype(F32) for x in _split_bf16(lf, N_SPLIT)]
        r = _dot(jnp.concatenate(parts, axis=0).astype(BF16), tri_ref[...])
        run0 = run = dcar_ref[...]
        biases = []
        for i, lf in enumerate(lfs):
            base = i * N_SPLIT * N_HEADS
            later = r[base:base + N_HEADS]
            for j in range(1, N_SPLIT):
                later = later + r[base + j * N_HEADS:base + (j + 1) * N_HEADS]
            biases.append(later + run)
            run = run + later[:, 0:1] + lf[:, 0:1]
        dcar_ref[...] = keep(valid, run, run0)
        return scores(cache_refs, biases, None)

    def past_attend(s, valid=None):
        attend(cache_refs, s, valid)

    def finish(last):
        @pl.when(last)
        def _():
            out = acc_ref[...] / l_ref[...]
            for t, rowv in enumerate(_extract_heads(out, t_new)):
                o_ref[t:t + 1, :] = rowv

    return begin, past_scores, past_attend, finish


def _sb_sample_stages(wq_ref, new_ref, tri_ref, cache_refs, o_ref, acc_ref, car_ref, t_new):
    rows = t_new * N_HEADS

    def logits(kv_refs, mask):
        zs, l1ms, parts = [], [], []
        for kv_ref in kv_refs:
            kt = kv_ref[0:D_GROUP, :].astype(BF16)
            z = _dot(wq_ref[...], kt)
            l1m = _neg_softplus(z)
            if mask is not None:
                l1m = jnp.where(mask, l1m, 0.0)
            zs.append(z)
            l1ms.append(l1m)
            parts += [x.astype(F32) for x in _split_bf16(l1m, 2)]
        return zs, l1ms, jnp.concatenate(parts, axis=0).astype(BF16)

    def later_sums(stacked):
        return _dot(stacked, tri_ref[...])

    def keep(valid, new, old):
        return new if valid is None else jnp.where(valid, new, old)

    def attend(kv_refs, zs, l1ms, r, mask, valid=None):
        run0 = run = car_ref[...]
        pv = None
        for i, kv_ref in enumerate(kv_refs):
            rest_i = r[2 * i * rows:(2 * i + 1) * rows] + r[(2 * i + 1) * rows:(2 * i + 2) * rows]
            wgt = jnp.exp(zs[i] + l1ms[i] + rest_i + run)
            if mask is not None:
                wgt = jnp.where(mask, wgt, 0.0)
            run = run + rest_i[:, 0:1] + l1ms[i][:, 0:1]
            vt = kv_ref[D_GROUP:2 * D_GROUP, :].astype(BF16)
            t = _dot_nt(wgt.astype(BF16), vt)
            pv = t if pv is None else pv + t
        car_ref[...] = keep(valid, run, run0)
        acc_old = acc_ref[...]
        acc_ref[...] = keep(valid, acc_old + pv, acc_old)

    def begin(first):
        @pl.when(first)
        def _():
            acc_ref[...] = jnp.zeros_like(acc_ref)
            car_ref[...] = jnp.zeros_like(car_ref)
            tok = lax.broadcasted_iota(jnp.int32, (rows, PAGE), 0) // N_HEADS
            key = lax.broadcasted_iota(jnp.int32, (rows, PAGE), 1)
            mask = key < tok
            zs, l1ms, stacked = logits([new_ref], mask)
            attend([new_ref], zs, l1ms, later_sums(stacked), mask)

    def past_logits():
        return logits(cache_refs, None)

    def past_attend(zs, l1ms, r, valid=None):
        attend(cache_refs, zs, l1ms, r, None, valid)

    def finish(last):
        @pl.when(last)
        def _():
            for t, rowv in enumerate(_extract_heads(acc_ref[...], t_new)):
                o_ref[t:t + 1, :] = rowv

    return begin, past_logits, later_sums, past_attend, finish


def _outproj_kernel(x_ref, y0_ref, y1_ref, y2_ref, y3_ref, g_ref, w_ref, o_ref):
    acc = x_ref[...]
    for gi, y_ref in enumerate((y0_ref, y1_ref, y2_ref, y3_ref)):
        y = y_ref[...]
        ms = jnp.mean(y * y, axis=-1, keepdims=True)
        yn = (y * lax.rsqrt(ms + EPS) * g_ref[gi:gi + 1, :]).astype(BF16)
        acc = acc + _dot(yn, w_ref[gi * D_GROUP:(gi + 1) * D_GROUP, :])
    o_ref[...] = acc


def _outproj(x, ys, g, w, layer, tm):
    m, d = x.shape
    yspec = pl.BlockSpec((tm, D_GROUP), lambda i: (i, 0))
    return pl.pallas_call(
        _outproj_kernel,
        out_shape=jax.ShapeDtypeStruct((m, d), F32),
        grid=(m // tm,),
        in_specs=[pl.BlockSpec((tm, d), lambda i: (i, 0)), yspec, yspec, yspec, yspec,
                  pl.BlockSpec((4, D_GROUP), lambda i: (0, 0)),
                  pl.BlockSpec((None, 4 * D_GROUP, d), lambda i: (layer, 0, 0))],
        out_specs=pl.BlockSpec((tm, d), lambda i: (i, 0)),
        compiler_params=_cparams(("arbitrary",)),
        name="outproj",
    )(x, *ys, g, w)


def _mlp_kernel(x_ref, g_ref, wu_ref, wd_ref, o_ref, xn_ref):
    @pl.when(pl.program_id(1) == 0)
    def _():
        x = x_ref[...]
        ms = jnp.mean(x * x, axis=-1, keepdims=True)
        xn_ref[...] = (x * lax.rsqrt(ms + EPS) * g_ref[...]).astype(BF16)
        o_ref[...] = x

    hid = jnp.maximum(_dot(xn_ref[...], wu_ref[...]), 0.0)
    o_ref[...] += _dot((hid * hid).astype(BF16), wd_ref[...])


def _mlp(x, g, wu, wd, layer, tm, tf):
    m, d = x.shape
    f = wu.shape[2]
    return pl.pallas_call(
        _mlp_kernel,
        out_shape=jax.ShapeDtypeStruct((m, d), F32),
        grid=(m // tm, f // tf),
        in_specs=[pl.BlockSpec((tm, d), lambda i, j: (i, 0)),
                  pl.BlockSpec((1, d), lambda i, j: (0, 0)),
                  pl.BlockSpec((None, d, tf), lambda i, j: (layer, 0, j)),
                  pl.BlockSpec((None, tf, d), lambda i, j: (layer, j, 0))],
        out_specs=pl.BlockSpec((tm, d), lambda i, j: (i, 0)),
        scratch_shapes=[pltpu.VMEM((tm, d), BF16)],
        compiler_params=_cparams(("arbitrary", "arbitrary"), 56),
        name="mlp",
    )(x, g, wu, wd)


def _block_diag(w):
    n, d, _ = w.shape
    eye = jnp.eye(n, dtype=w.dtype)
    return (eye[:, None, :, None] * w[:, :, None, :]).reshape(n * d, n * d)


def _input_weights(w_in):
    nf = 6 * D_GROUP
    wt = jnp.swapaxes(w_in, 1, 2)
    pad = jnp.zeros((wt.shape[0], LANES - N_HEADS, wt.shape[2]), wt.dtype)
    return jnp.concatenate([wt[:, :nf], wt[:, nf + N_HEADS:], wt[:, nf:nf + N_HEADS], pad], axis=1).astype(BF16)


def _layer_weights(l, norm1_g, pool_w, pool_scale, conv_w, conv_b, lru_wa, lru_ba, lru_wx, lru_bx,
                   lru_lambda, fox_bf, fox_q_g, fox_k_g, out_g, norm2_g):
    g = D_GROUP
    return dict(
        norm1_g=norm1_g[l][None, :],
        pool_w=pool_w[l].astype(BF16), pool_scale=pool_scale[l][None, :],
        conv_w=jnp.pad(conv_w[l], ((0, HIST_C - CONV_W), (0, 0))), conv_b=conv_b[l][None, :],
        wax=jnp.concatenate([_block_diag(lru_wa[l]), _block_diag(lru_wx[l])], axis=1).astype(BF16),
        ba=lru_ba[l][None, :], bx=lru_bx[l][None, :], lam=lru_lambda[l][None, :],
        bf=jnp.pad(fox_bf[l], (0, LANES - N_HEADS))[None, :],
        gq=jnp.tile(fox_q_g[l], N_HEADS)[None, :], gk=jnp.tile(fox_k_g[l], N_HEADS)[None, :],
        out_g=out_g[l].reshape(4, g), norm2_g=norm2_g[l][None, :],
    )


def _tile(m, pref):
    t = min(m, pref)
    while m % t:
        t //= 2
    return t


def _slot_constants():
    place = np.zeros((D_GROUP, D_AUG), np.float32)
    for c in range(D_GROUP):
        place[c, (c // HEAD_DIM) * LANES + c % HEAD_DIM] = 1.0
    placec = np.zeros((N_SPLIT, LANES, D_AUG), np.float32)
    neg = np.zeros((1, D_AUG), np.float32)
    for i in range(N_SPLIT):
        for h in range(N_HEADS):
            placec[i, h, h * LANES + HEAD_DIM + i] = 1.0
            neg[0, h * LANES + HEAD_DIM + i] = -1.0
    return jnp.asarray(place, BF16), jnp.asarray(placec, BF16), jnp.asarray(neg, F32)


def _expand_queries(q, b, t_new):
    q4 = q.reshape(b, t_new, 1, D_GROUP)
    own = (jnp.arange(D_GROUP)[None, :] // HEAD_DIM) == jnp.arange(N_HEADS)[:, None]
    return jnp.where(own[None, None], q4, jnp.zeros_like(q4)).reshape(b, t_new * N_HEADS, D_GROUP)


def _pages_view(cache):
    d, n = cache.shape[0], cache.shape[1]
    return jnp.transpose(cache, (0, 1, 3, 4, 5, 2)).reshape(d, n, 2 * D_GROUP, PAGE)


def kernel(x_prompt, x_sample, cache_fox_kv, cache_fox_logf, cache_sb_kv, state_pool, state_conv, state_lru,
           page_table, norm1_g, w_in, pool_w, pool_scale, conv_w, conv_b, lru_wa, lru_ba, lru_wx, lru_bx,
           lru_lambda, fox_bf, fox_q_g, fox_k_g, out_g, w_out, norm2_g, w_up, w_down):
    depth = w_in.shape[0]
    bp, seq, d_model = x_prompt.shape
    bs, t_new, _ = x_sample.shape
    past_len = page_table.shape[1] * PAGE
    g = D_GROUP

    seg = jnp.asarray(np.kron(np.eye(N_HEADS), np.ones((HEAD_DIM, HEAD_DIM))), BF16)
    place, placec, neg = _slot_constants()
    tk = _tile(seq, 256)
    tq = _tile(seq, 512)
    tri_q = jnp.asarray(np.triu(np.ones((tk, tk)), 1), BF16)
    tri_p = jnp.asarray(np.tril(np.ones((PAGE, PAGE)), -1), BF16)
    pt_flat = page_table[:, ::-1].reshape(-1).astype(jnp.int32)
    fox_cache = _pages_view(cache_fox_kv)
    sb_cache = _pages_view(cache_sb_kv)
    lft_cache = jnp.swapaxes(cache_fox_logf, 2, 3)
    t_pad = 8

    xp = x_prompt.reshape(bp * seq, d_model)
    xs = x_sample.reshape(bs * t_new, d_model)
    st_p, st_s = [], []
    w_in_b = _input_weights(w_in)
    w_out_b, w_up_b, w_down_b = w_out.astype(BF16), w_up.astype(BF16), w_down.astype(BF16)
    tf = _tile(w_up.shape[2], 512)
    for l in range(depth):
        w = _layer_weights(l, norm1_g, pool_w, pool_scale, conv_w, conv_b, lru_wa, lru_ba, lru_wx,
                           lru_bx, lru_lambda, fox_bf, fox_q_g, fox_k_g, out_g, norm2_g)

        m = bp * seq
        proj, flog = _inproj(xp, w["norm1_g"], w_in_b, l, _tile(m, 1024))
        r3 = lambda a: a.reshape(bp, seq, a.shape[-1])
        fkvt, skvt, lft, ct, fqt, fka, fvt, sqt, ska, svt = _prep_prompt(
            r3(proj), r3(flog), w["gq"], w["gk"], w["bf"], seg, place, placec, neg, _tile(seq, 256))
        y_fox = _fox_prompt_attention(fqt, fka, fvt, ct, tq, tk)

        ms = bs * t_new
        proj_s, flog_s = _inproj(xs, w["norm1_g"], w_in_b, l, ms)
        fkv, skv, lf, c_s, ct_s, qf, qs = _prep_sample(proj_s, flog_s, w["gq"], w["gk"], w["bf"], seg, t_new)
        new_page = lambda a: jnp.pad(jnp.swapaxes(a.reshape(bs, t_new, 2 * g), 1, 2),
                                     ((0, 0), (0, 0), (0, PAGE - t_new)))
        gt = jnp.pad(jnp.swapaxes(ct_s.reshape(N_HEADS, bs, t_new), 0, 1), ((0, 0), (0, 0), (0, PAGE - t_new)))
        grow = c_s[:, :N_HEADS].reshape(bs, t_new * N_HEADS, 1)
        y_sb, y_fox_s, y_sb_s = _sb_prompt_attention(
            sqt, ska, svt, tri_q, tq, tk, pt_flat,
            (_expand_queries(qf, bs, t_new), new_page(fkv), gt, grow, fox_cache, lft_cache),
            (_expand_queries(qs, bs, t_new), new_page(skv), sb_cache), tri_p, l, t_new, PAGES_PER_STEP)
        tt = _tile(seq, 256)
        y_pool, y_lru, pnew, cnew, hnew = _seqmix(
            r3(proj), jnp.zeros((bp, HIST_P, g), F32), jnp.zeros((bp, HIST_C, g), F32),
            jnp.zeros((bp, 1, g), F32), w["pool_w"], w["pool_scale"], w["conv_w"], w["conv_b"], w["wax"],
            w["ba"], w["bx"], w["lam"], tt, tt, 0)
        flat = lambda a: a.reshape(m, g)
        x1p = _outproj(xp, (flat(y_pool), flat(y_lru), flat(y_fox), flat(y_sb)), w["out_g"], w_out_b, l,
                       _tile(m, 512))
        xp = _mlp(x1p, w["norm2_g"], w_up_b, w_down_b, l, _tile(m, 1024), tf)
        kv_state = lambda a: jnp.transpose(a.reshape(bp, 2, N_HEADS, HEAD_DIM, seq), (0, 4, 1, 2, 3))
        st_p.append((kv_state(fkvt), jnp.swapaxes(lft, 1, 2), kv_state(skvt), pnew[:, 1:],
                     cnew[:, HIST_C - CONV_W + 1:], hnew[:, 0]))

        proj3 = jnp.pad(proj_s.reshape(bs, t_new, -1), ((0, 0), (0, t_pad - t_new), (0, 0)))
        y_pool, y_lru, pnew, cnew, hnew = _seqmix(
            proj3, jnp.pad(state_pool[l], ((0, 0), (HIST_P - POOL_MAX + 1, 0), (0, 0))),
            jnp.pad(state_conv[l], ((0, 0), (HIST_C - CONV_W + 1, 0), (0, 0))), state_lru[l][:, None, :],
            w["pool_w"], w["pool_scale"], w["conv_w"], w["conv_b"], w["wax"], w["ba"], w["bx"], w["lam"],
            t_pad, t_new, past_len)
        cut = lambda a: a[:, :t_new].reshape(ms, g)
        x1s = _outproj(xs, (cut(y_pool), cut(y_lru), y_fox_s.reshape(ms, g), y_sb_s.reshape(ms, g)), w["out_g"],
                       w_out_b, l, ms)
        xs = _mlp(x1s, w["norm2_g"], w_up_b, w_down_b, l, ms, tf)
        st_s.append((fkv.reshape(bs, t_new, 2, N_HEADS, HEAD_DIM), lf[:, :N_HEADS].reshape(bs, t_new, N_HEADS),
                     skv.reshape(bs, t_new, 2, N_HEADS, HEAD_DIM), pnew[:, 1:], cnew[:, HIST_C - CONV_W + 1:],
                     hnew[:, 0]))

    stk = lambda sts, j: jnp.stack([s[j] for s in sts], axis=0)
    return ((xp.reshape(bp, seq, d_model), xs.reshape(bs, t_new, d_model))
            + tuple(stk(st_p, j) for j in range(6)) + tuple(stk(st_s, j) for j in range(6)))
```

```python
import functools

import numpy as np
import jax
import jax.numpy as jnp
from jax import lax
from jax.experimental import pallas as pl
from jax.experimental.pallas import tpu as pltpu

F32 = jnp.float32
BF16 = jnp.bfloat16

D_GROUP = 512
HEAD_DIM = 64
N_HEADS = D_GROUP // HEAD_DIM
POOL_WINDOWS = (2, 4, 8, 16)
POOL_MAX = max(POOL_WINDOWS)
POOL_CH = D_GROUP // len(POOL_WINDOWS)
CONV_W = 4
LRU_C = 8.0
EPS = 1e-6
PAGE = 128
LANES = 128
D_AUG = N_HEADS * LANES
N_SPLIT = 3
HIST_P = 16
HIST_C = 8
NEG_LOG2E = -1.4426950408889634
NEG_BIG = -1e30
PAGES_PER_STEP = 8
MIB = 1024 * 1024


def _cparams(sem, vmem_mib=48):
    return pltpu.CompilerParams(dimension_semantics=sem, vmem_limit_bytes=vmem_mib * MIB)


def _split_bf16(x, parts):
    out = []
    r = x
    for _ in range(parts - 1):
        h = r.astype(BF16)
        out.append(h)
        r = r - h.astype(F32)
    out.append(r.astype(BF16))
    return out


def _softplus(x):
    return jnp.maximum(x, 0.0) + jnp.log1p(jnp.exp(-jnp.abs(x)))


def _log_sigmoid(x):
    return -_softplus(-x)


def _softplus_exp2(z):
    return jnp.maximum(z, 0.0) + jnp.log(1.0 + jnp.exp2(jnp.abs(z) * NEG_LOG2E))


def _neg_softplus(z):
    return -(jnp.maximum(z, 0.0) + jnp.log(1.0 + jnp.exp(-jnp.abs(z))))


def _dot_nt(a, b):
    return lax.dot_general(a, b, (((1,), (1,)), ((), ())), preferred_element_type=F32)


def _dot(a, b):
    return jnp.dot(a, b, preferred_element_type=F32)


def _inproj_kernel(x_ref, g_ref, w_ref, wf_ref, o_ref, of_ref, xn_ref):
    @pl.when(pl.program_id(1) == 0)
    def _():
        x = x_ref[...]
        ms = jnp.mean(x * x, axis=-1, keepdims=True)
        xn = (x * lax.rsqrt(ms + EPS) * g_ref[...]).astype(BF16)
        xn_ref[...] = xn
        of_ref[...] = _dot_nt(xn, wf_ref[...])

    o_ref[...] = _dot_nt(xn_ref[...], w_ref[...])


def _inproj(x, g, w, layer, tm):
    m, d = x.shape
    n = w.shape[1] - LANES
    tn = D_GROUP
    return pl.pallas_call(
        _inproj_kernel,
        out_shape=(jax.ShapeDtypeStruct((m, n), F32), jax.ShapeDtypeStruct((m, LANES), F32)),
        grid=(m // tm, n // tn),
        in_specs=[
            pl.BlockSpec((tm, d), lambda i, j: (i, 0)),
            pl.BlockSpec((1, d), lambda i, j: (0, 0)),
            pl.BlockSpec((None, tn, d), lambda i, j: (layer, j, 0)),
            pl.BlockSpec((None, LANES, d), lambda i, j: (layer, n // LANES, 0)),
        ],
        out_specs=(
            pl.BlockSpec((tm, tn), lambda i, j: (i, j)),
            pl.BlockSpec((tm, LANES), lambda i, j: (i, 0)),
        ),
        scratch_shapes=[pltpu.VMEM((tm, d), BF16)],
        compiler_params=_cparams(("arbitrary", "arbitrary")),
        name="inproj",
    )(x, g, w, w)


def _head_norm(x, g, seg_ref):
    hi, lo = _split_bf16(x * x, 2)
    ss = _dot(hi, seg_ref[...]) + _dot(lo, seg_ref[...])
    return x * lax.rsqrt(ss * (1.0 / HEAD_DIM) + EPS) * g


def _running_sum(lf, seq, tm):
    r = lax.broadcasted_iota(jnp.int32, (tm, tm), 0)
    c = lax.broadcasted_iota(jnp.int32, (tm, tm), 1)
    keep = c <= r
    if seq < tm:
        keep = jnp.logical_and(keep, (r // seq) == (c // seq))
    tri = jnp.where(keep, 1.0, 0.0).astype(BF16)
    cs = None
    for part in _split_bf16(lf, N_SPLIT):
        t = _dot(tri, part)
        cs = t if cs is None else cs + t
    return cs


def _prep_prompt_kernel(fq_ref, fk_ref, fv_ref, sq_ref, sk_ref, sv_ref, fl_ref, gq_ref, gk_ref, bf_ref,
                        seg_ref, place_ref, placec_ref, neg_ref,
                        fkvt_ref, skvt_ref, lft_ref, ct_ref,
                        fqt_ref, fka_ref, fvt_ref, sqt_ref, ska_ref, svt_ref, carry_ref, *, tm):
    scale = HEAD_DIM ** -0.5
    fq = _head_norm(fq_ref[...], gq_ref[...], seg_ref)
    fk = _head_norm(fk_ref[...], gk_ref[...], seg_ref)
    fv = fv_ref[...]
    sk = sk_ref[...]
    sv = sv_ref[...]

    fvt = fv.T
    svt = sv.T
    fkvt_ref[0:D_GROUP, :] = fk.T
    fkvt_ref[D_GROUP:2 * D_GROUP, :] = fvt
    skvt_ref[0:D_GROUP, :] = sk.T
    skvt_ref[D_GROUP:2 * D_GROUP, :] = svt
    fvt_ref[...] = fvt.astype(BF16)
    svt_ref[...] = svt.astype(BF16)

    lf = _log_sigmoid(fl_ref[...] + bf_ref[...])
    cs = _running_sum(lf, tm, tm)

    @pl.when(pl.program_id(1) == 0)
    def _():
        carry_ref[...] = jnp.zeros_like(carry_ref)
    cs = cs + carry_ref[...]
    carry_ref[...] = cs[tm - 1:tm, :]
    lft_ref[...] = lf.T[0:N_HEADS, :]
    ct_ref[...] = cs.T[0:N_HEADS, :]

    ka = _dot(fk.astype(BF16), place_ref[...])
    for i, part in enumerate(_split_bf16(cs, N_SPLIT)):
        ka = ka + _dot(part, placec_ref[i])
    fka_ref[...] = ka.astype(BF16)
    qa = _dot((fq * scale).astype(BF16), place_ref[...]) + neg_ref[...]
    fqt_ref[...] = qa.T.astype(BF16)
    ska_ref[...] = sk.astype(BF16)
    sqt_ref[...] = (sq_ref[...] * scale).T.astype(BF16)


def _prep_prompt(proj3, flog3, gq, gk, bf, seg, place, placec, neg, tm):
    b, t, _ = proj3.shape
    col = lambda s: pl.BlockSpec((None, tm, D_GROUP), lambda bi, ti, s=s: (bi, ti, s))
    const = lambda shp: pl.BlockSpec(shp, lambda bi, ti: (0,) * len(shp))
    tr = lambda rows: pl.BlockSpec((None, rows, tm), lambda bi, ti: (bi, 0, ti))
    nt = lambda cols: pl.BlockSpec((None, tm, cols), lambda bi, ti: (bi, ti, 0))
    sds = jax.ShapeDtypeStruct
    return pl.pallas_call(
        functools.partial(_prep_prompt_kernel, tm=tm),
        out_shape=(
            sds((b, 2 * D_GROUP, t), F32), sds((b, 2 * D_GROUP, t), F32),
            sds((b, N_HEADS, t), F32), sds((b, N_HEADS, t), F32),
            sds((b, D_AUG, t), BF16), sds((b, t, D_AUG), BF16), sds((b, D_GROUP, t), BF16),
            sds((b, D_GROUP, t), BF16), sds((b, t, D_GROUP), BF16), sds((b, D_GROUP, t), BF16),
        ),
        grid=(b, t // tm),
        in_specs=[col(3), col(4), col(5), col(6), col(7), col(8),
                  pl.BlockSpec((None, tm, LANES), lambda bi, ti: (bi, ti, 0)),
                  const((1, D_GROUP)), const((1, D_GROUP)), const((1, LANES)),
                  const((D_GROUP, D_GROUP)), const((D_GROUP, D_AUG)), const((N_SPLIT, LANES, D_AUG)),
                  const((1, D_AUG))],
        out_specs=(tr(2 * D_GROUP), tr(2 * D_GROUP), tr(N_HEADS), tr(N_HEADS),
                   tr(D_AUG), nt(D_AUG), tr(D_GROUP), tr(D_GROUP), nt(D_GROUP), tr(D_GROUP)),
        scratch_shapes=[pltpu.VMEM((1, LANES), F32)],
        compiler_params=_cparams(("arbitrary", "arbitrary")),
        name="prep_prompt",
    )(proj3, proj3, proj3, proj3, proj3, proj3, flog3, gq, gk, bf, seg, place, placec, neg)


def _prep_sample_kernel(fq_ref, fk_ref, fv_ref, sq_ref, sk_ref, sv_ref, fl_ref, gq_ref, gk_ref, bf_ref,
                        seg_ref, fkv_ref, skv_ref, lf_ref, c_ref, ct_ref, qf_ref, qs_ref, *, seq, tm):
    scale = HEAD_DIM ** -0.5
    fq = _head_norm(fq_ref[...], gq_ref[...], seg_ref)
    fk = _head_norm(fk_ref[...], gk_ref[...], seg_ref)
    qf_ref[...] = (fq * scale).astype(BF16)
    qs_ref[...] = (sq_ref[...] * scale).astype(BF16)
    fkv_ref[:, 0:D_GROUP] = fk
    fkv_ref[:, D_GROUP:2 * D_GROUP] = fv_ref[...]
    skv_ref[:, 0:D_GROUP] = sk_ref[...]
    skv_ref[:, D_GROUP:2 * D_GROUP] = sv_ref[...]
    lf = _log_sigmoid(fl_ref[...] + bf_ref[...])
    lf_ref[...] = lf
    cs = _running_sum(lf, seq, tm)
    c_ref[...] = cs
    ct_ref[...] = cs.T[0:N_HEADS, :]


def _prep_sample(proj, flog, gq, gk, bf, seg, seq):
    m = proj.shape[0]
    col = lambda s: pl.BlockSpec((m, D_GROUP), lambda i, s=s: (0, s))
    full = lambda shp: pl.BlockSpec(shp, lambda i: (0,) * len(shp))
    sds = jax.ShapeDtypeStruct
    return pl.pallas_call(
        functools.partial(_prep_sample_kernel, seq=seq, tm=m),
        out_shape=(sds((m, 2 * D_GROUP), F32), sds((m, 2 * D_GROUP), F32), sds((m, LANES), F32),
                   sds((m, LANES), F32), sds((N_HEADS, m), F32), sds((m, D_GROUP), BF16),
                   sds((m, D_GROUP), BF16)),
        grid=(1,),
        in_specs=[col(3), col(4), col(5), col(6), col(7), col(8), full((m, LANES)),
                  full((1, D_GROUP)), full((1, D_GROUP)), full((1, LANES)), full((D_GROUP, D_GROUP))],
        out_specs=(full((m, 2 * D_GROUP)), full((m, 2 * D_GROUP)), full((m, LANES)), full((m, LANES)),
                   full((N_HEADS, m)), full((m, D_GROUP)), full((m, D_GROUP))),
        compiler_params=_cparams(("arbitrary",)),
        name="prep_sample",
    )(proj, proj, proj, proj, proj, proj, flog, gq, gk, bf, seg)


def _seqmix_kernel(u_ref, x_ref, gate_ref, pprev_ref, cprev_ref, h0_ref,
                   pw_ref, pscale_ref, cw_ref, cb_ref, wax_ref, ba_ref, bx_ref, lam_ref,
                   yp_ref, yl_ref, pnew_ref, cnew_ref, hnew_ref,
                   extp_ref, extc_ref, h_ref, *, tt, tv, pos0):
    ti = pl.program_id(1)

    @pl.when(ti == 0)
    def _():
        extp_ref[0:HIST_P, :] = pprev_ref[...]
        extc_ref[0:HIST_C, :] = cprev_ref[...]
        h_ref[...] = h0_ref[...]

    extp_ref[HIST_P:HIST_P + tt, :] = u_ref[...]
    extc_ref[HIST_C:HIST_C + tt, :] = x_ref[...]
    pos = pos0 + ti * tt + lax.broadcasted_iota(jnp.int32, (tt, 1), 0)

    for g, w in enumerate(POOL_WINDOWS):
        lanes = slice(g * POOL_CH, (g + 1) * POOL_CH)
        tok = extp_ref[HIST_P:HIST_P + tt, lanes]
        win = tok
        for j in range(1, w):
            win = win + extp_ref[HIST_P - j:HIST_P - j + tt, lanes]
        cnt = jnp.minimum(w, pos + 1).astype(F32)
        d = win / cnt - tok
        y = _dot(d.astype(BF16), pw_ref[g]) * pscale_ref[:, lanes]
        yp_ref[:, lanes] = y
    hist = extp_ref[tv:tv + HIST_P, :]
    extp_ref[0:HIST_P, :] = hist
    pnew_ref[...] = hist

    base = HIST_C - (CONV_W - 1)
    xc = extc_ref[base:base + tt, :] * cw_ref[0:1, :]
    for k in range(1, CONV_W):
        xc = xc + extc_ref[base + k:base + k + tt, :] * cw_ref[k:k + 1, :]
    xc = cb_ref[...] + xc
    chist = extc_ref[tv:tv + HIST_C, :]
    extc_ref[0:HIST_C, :] = chist
    cnew_ref[...] = chist

    ri = _dot(xc.astype(BF16), wax_ref[...])
    r = jax.nn.sigmoid(ri[:, 0:D_GROUP] + ba_ref[...])
    gi = jax.nn.sigmoid(ri[:, D_GROUP:2 * D_GROUP] + bx_ref[...])
    log_a = -LRU_C * r * _softplus(-lam_ref[...])
    a = jnp.exp(log_a)
    mult = jnp.sqrt(-jnp.tanh(log_a) * (a * a + 1.0))
    mult = jnp.where(pos == 0, 1.0, mult)
    b = mult * (gi * xc)

    rows = lax.broadcasted_iota(jnp.int32, (tt, 1), 0)
    d = 1
    while d < tt:
        ok = rows >= d
        a_s = jnp.where(ok, pltpu.roll(a, d, 0), 1.0)
        b_s = jnp.where(ok, pltpu.roll(b, d, 0), 0.0)
        b = a * b_s + b
        a = a * a_s
        d *= 2
    h = b + a * h_ref[...]
    hlast = h[tv - 1:tv, :]
    h_ref[...] = hlast
    hnew_ref[...] = hlast
    yl_ref[...] = jax.nn.gelu(gate_ref[...]) * h


def _seqmix(proj3, pprev, cprev, h0, pw, pscale, cw, cb, wax, ba, bx, lam, tt, tv, pos0):
    b, t, _ = proj3.shape
    nt = t // tt
    col = lambda s: pl.BlockSpec((None, tt, D_GROUP), lambda bi, ti, s=s: (bi, ti, s))
    per_b = lambda r: pl.BlockSpec((None, r, D_GROUP), lambda bi, ti: (bi, 0, 0))
    const2 = lambda shp: pl.BlockSpec(shp, lambda bi, ti: (0, 0))
    return pl.pallas_call(
        functools.partial(_seqmix_kernel, tt=tt, tv=tv, pos0=pos0),
        out_shape=(
            jax.ShapeDtypeStruct((b, t, D_GROUP), F32),
            jax.ShapeDtypeStruct((b, t, D_GROUP), F32),
            jax.ShapeDtypeStruct((b, HIST_P, D_GROUP), F32),
            jax.ShapeDtypeStruct((b, HIST_C, D_GROUP), F32),
            jax.ShapeDtypeStruct((b, 1, D_GROUP), F32),
        ),
        grid=(b, nt),
        in_specs=[col(0), col(1), col(2), per_b(HIST_P), per_b(HIST_C), per_b(1),
                  pl.BlockSpec((len(POOL_WINDOWS), POOL_CH, POOL_CH), lambda bi, ti: (0, 0, 0)),
                  const2((1, D_GROUP)), const2((HIST_C, D_GROUP)), const2((1, D_GROUP)),
                  const2((D_GROUP, 2 * D_GROUP)), const2((1, D_GROUP)), const2((1, D_GROUP)),
                  const2((1, D_GROUP))],
        out_specs=(
            pl.BlockSpec((None, tt, D_GROUP), lambda bi, ti: (bi, ti, 0)),
            pl.BlockSpec((None, tt, D_GROUP), lambda bi, ti: (bi, ti, 0)),
            per_b(HIST_P), per_b(HIST_C), per_b(1),
        ),
        scratch_shapes=[pltpu.VMEM((HIST_P + tt, D_GROUP), F32),
                        pltpu.VMEM((HIST_C + tt, D_GROUP), F32),
                        pltpu.VMEM((1, D_GROUP), F32)],
        compiler_params=_cparams(("arbitrary", "arbitrary")),
        name="seqmix",
    )(proj3, proj3, proj3, pprev, cprev, h0, pw, pscale, cw, cb, wax, ba, bx, lam)


def _pair_tables(n, ratio, reverse):
    qi, ki = [], []
    for q in range(n):
        ks = range((q + 1) * ratio)
        for k in (reversed(ks) if reverse else ks):
            qi.append(q)
            ki.append(k)
    return jnp.asarray(np.array(qi, np.int32)), jnp.asarray(np.array(ki, np.int32))


def _key_minus_query(tk, tq):
    return (lax.broadcasted_iota(jnp.int32, (tk, tq), 0) - lax.broadcasted_iota(jnp.int32, (tk, tq), 1))


def _fox_prompt_kernel(qi_ref, ki_ref, qt_ref, k_ref, vt_ref, cq_ref, o_ref, m_ref, l_ref, acc_ref, *, tq, tk):
    s_id = pl.program_id(1)
    qi = qi_ref[s_id]
    ki = ki_ref[s_id]
    ratio = tq // tk

    @pl.when(ki == 0)
    def _():
        m_ref[...] = jnp.full_like(m_ref, NEG_BIG)
        l_ref[...] = jnp.zeros_like(l_ref)
        acc_ref[...] = jnp.zeros_like(acc_ref)

    def step(masked, first=0):
        win = slice(first, tq)
        if masked:
            causal = _key_minus_query(tk, tq - first) <= 0

        def scores(h):
            slot = slice(h * LANES, (h + 1) * LANES)
            return _dot(k_ref[:, slot], qt_ref[slot, win])

        def softmax_update(h, st):
            if masked:
                st = jnp.where(causal, st, NEG_BIG)
            cq = cq_ref[h:h + 1, win]
            m_old = m_ref[h:h + 1, win]
            m_new = jnp.maximum(m_old, jnp.max(st, axis=0, keepdims=True) + cq)
            alpha = jnp.exp(m_old - m_new)
            pt = jnp.exp(st - (m_new - cq))
            l_ref[h:h + 1, win] = alpha * l_ref[h:h + 1, win] + jnp.sum(pt, axis=0, keepdims=True)
            m_ref[h:h + 1, win] = m_new
            return alpha, pt.astype(BF16)

        st = {0: scores(0), 1: scores(1)}
        for h in range(N_HEADS):
            rows = slice(h * HEAD_DIM, (h + 1) * HEAD_DIM)
            if h + 2 < N_HEADS:
                st[h + 2] = scores(h + 2)
            alpha, pt = softmax_update(h, st.pop(h))
            acc_ref[rows, win] = alpha * acc_ref[rows, win] + _dot(vt_ref[rows, :], pt)

    @pl.when(ki < qi * ratio)
    def _():
        step(False)

    for diag in range(ratio):
        @pl.when(ki == qi * ratio + diag)
        def _(diag=diag):
            step(True, diag * tk)

    @pl.when(ki == (qi + 1) * ratio - 1)
    def _():
        for h in range(N_HEADS):
            rows = slice(h * HEAD_DIM, (h + 1) * HEAD_DIM)
            acc_ref[rows, :] = acc_ref[rows, :] / l_ref[h:h + 1, :]
        o_ref[...] = acc_ref[...].T


def _sb_prompt_kernel(qi_ref, ki_ref, pt_ref, qt_ref, k_ref, vt_ref, tri_ref, fwq_ref, fnew_ref, gt_ref,
                      grow_ref, trip_ref, swq_ref, snew_ref, *rest, tq, tk, pages, groups, tasks, t_new):
    fcache = rest[0:pages]
    lft = rest[pages:2 * pages]
    scache = rest[2 * pages:3 * pages]
    (o_ref, of_ref, os_ref, carry_ref, acc_ref, fm_ref, fl_ref, facc_ref, fdcar_ref, sacc_ref,
     scar_ref) = rest[3 * pages:]
    s_id = pl.program_id(1)
    qi = qi_ref[s_id]
    ki = ki_ref[s_id]
    ratio = tq // tk

    n = pl.program_id(0) * pl.num_programs(1) + s_id
    valid = n < tasks
    grp = lax.rem(jnp.minimum(n, tasks - 1), groups)
    f_begin, f_scores, f_attend, f_finish = _fox_sample_stages(
        fwq_ref, fnew_ref, gt_ref, grow_ref, trip_ref, fcache, lft, of_ref, fm_ref, fl_ref, facc_ref,
        fdcar_ref, t_new)
    s_begin, s_logits, s_later, s_attend, s_finish = _sb_sample_stages(
        swq_ref, snew_ref, trip_ref, scache, os_ref, sacc_ref, scar_ref, t_new)
    f_begin(jnp.logical_and(valid, grp == 0))
    s_begin(jnp.logical_and(valid, grp == 0))

    @pl.when(ki == (qi + 1) * ratio - 1)
    def _():
        carry_ref[...] = jnp.zeros_like(carry_ref)
        acc_ref[...] = jnp.zeros_like(acc_ref)

    def step(masked, first=0):
        win = slice(first, tq)
        if masked:
            strict = _key_minus_query(tk, tq - first) < 0

        def logits(h):
            pair = slice((h // 2) * LANES, (h // 2 + 1) * LANES)
            qh = qt_ref[h * HEAD_DIM:(h + 1) * HEAD_DIM, win]
            zero = jnp.zeros_like(qh)
            qpair = jnp.concatenate([qh, zero] if h % 2 == 0 else [zero, qh], axis=0)
            return _dot(k_ref[:, pair], qpair)

        def neg_log_one_minus_beta(zt):
            sp = _softplus_exp2(zt)
            if masked:
                sp = jnp.where(strict, sp, 0.0)
            return (sp,) + tuple(_split_bf16(sp, 2))

        def later_keys_sum(hi, lo):
            return _dot(tri_ref[...], hi) + _dot(tri_ref[...], lo)

        def weights(zt, sp, rest):
            wgt = jnp.exp(zt - sp - rest)
            if masked:
                wgt = jnp.where(strict, wgt, 0.0)
            return wgt.astype(BF16)

        fs = f_scores(valid)
        zs, l1ms, stacked = s_logits()
        zt = {0: logits(0), 1: logits(1)}
        lg = {0: neg_log_one_minus_beta(zt[0])}
        rest = {}
        for h in range(N_HEADS + 1):
            if h == 2:
                sr = s_later(stacked)
            elif h == 4:
                f_attend(fs, valid)
            elif h == 6:
                s_attend(zs, l1ms, sr, valid)
            if h < N_HEADS:
                rest[h] = later_keys_sum(*lg[h][1:])
            if h + 2 < N_HEADS:
                zt[h + 2] = logits(h + 2)
            if h >= 1:
                g = h - 1
                rows = slice(g * HEAD_DIM, (g + 1) * HEAD_DIM)
                sp, rs = lg.pop(g)[0], rest.pop(g)
                carry = carry_ref[g:g + 1, win]
                pv = _dot(vt_ref[rows, :], weights(zt.pop(g), sp, rs))
                acc_ref[rows, win] = acc_ref[rows, win] + pv * jnp.exp(-carry)
                carry_ref[g:g + 1, win] = carry + rs[0:1, :] + sp[0:1, :]
            if h + 1 < N_HEADS:
                lg[h + 1] = neg_log_one_minus_beta(zt[h + 1])

    for diag in range(ratio):
        @pl.when(ki == qi * ratio + diag)
        def _(diag=diag):
            step(True, diag * tk)

    @pl.when(ki < qi * ratio)
    def _():
        step(False)

    @pl.when(ki == 0)
    def _():
        o_ref[...] = acc_ref[...].T

    f_finish(jnp.logical_and(valid, grp == groups - 1))
    s_finish(jnp.logical_and(valid, grp == groups - 1))


def _fox_prompt_attention(qt, k, vt, cq, tq, tk):
    b, t, d_qk = k.shape
    qi, ki = _pair_tables(t // tq, tq // tk, reverse=False)
    grid_spec = pltpu.PrefetchScalarGridSpec(
        num_scalar_prefetch=2,
        grid=(b, int(qi.shape[0])),
        in_specs=[pl.BlockSpec((None, d_qk, tq), lambda bi, s, qi, ki: (bi, 0, qi[s])),
                  pl.BlockSpec((None, tk, d_qk), lambda bi, s, qi, ki: (bi, ki[s], 0)),
                  pl.BlockSpec((None, D_GROUP, tk), lambda bi, s, qi, ki: (bi, 0, ki[s])),
                  pl.BlockSpec((None, N_HEADS, tq), lambda bi, s, qi, ki: (bi, 0, qi[s]))],
        out_specs=pl.BlockSpec((None, tq, D_GROUP), lambda bi, s, qi, ki: (bi, qi[s], 0)),
        scratch_shapes=[pltpu.VMEM((N_HEADS, tq), F32), pltpu.VMEM((N_HEADS, tq), F32),
                        pltpu.VMEM((D_GROUP, tq), F32)],
    )
    return pl.pallas_call(
        functools.partial(_fox_prompt_kernel, tq=tq, tk=tk),
        out_shape=jax.ShapeDtypeStruct((b, t, D_GROUP), F32),
        grid_spec=grid_spec,
        compiler_params=_cparams(("arbitrary", "arbitrary")),
        name="fox_prompt",
    )(qi, ki, qt, k, vt, cq)


def _sb_prompt_attention(qt, k, vt, tri, tq, tk, pt, fox, sb, tri_page, layer, t_new, pages):
    b, t, d_qk = k.shape
    qi, ki = _pair_tables(t // tq, tq // tk, reverse=True)
    n_steps = int(qi.shape[0])
    fwq, fnew, gt, grow, fcache, lft_cache = fox
    swq, snew, scache = sb
    bs = fwq.shape[0]
    n_pages = pt.shape[0] // bs
    groups = n_pages // pages
    tasks = bs * groups
    assert groups * pages == n_pages and tasks <= b * n_steps, (n_pages, pages, tasks, b * n_steps)
    rows = t_new * N_HEADS

    def task_of(bi, s):
        return jnp.minimum(bi * n_steps + s, tasks - 1)

    def page_spec(block, slot):
        return pl.BlockSpec((None, None) + block,
                            lambda bi, s, qi, ki, pt: (layer, pt[task_of(bi, s) * pages + slot], 0, 0))

    per_seq = lambda r, c: pl.BlockSpec(
        (None, r, c), lambda bi, s, qi, ki, pt: (lax.div(task_of(bi, s), groups), 0, 0))
    const = lambda shp: pl.BlockSpec(shp, lambda bi, s, qi, ki, pt: (0,) * len(shp))
    in_specs = ([pl.BlockSpec((None, d_qk, tq), lambda bi, s, qi, ki, pt: (bi, 0, qi[s])),
                 pl.BlockSpec((None, tk, d_qk), lambda bi, s, qi, ki, pt: (bi, ki[s], 0)),
                 pl.BlockSpec((None, D_GROUP, tk), lambda bi, s, qi, ki, pt: (bi, 0, ki[s])),
                 const((tk, tk)),
                 per_seq(rows, D_GROUP), per_seq(2 * D_GROUP, PAGE), per_seq(N_HEADS, PAGE), per_seq(rows, 1),
                 const((PAGE, PAGE)), per_seq(rows, D_GROUP), per_seq(2 * D_GROUP, PAGE)]
                + [page_spec((2 * D_GROUP, PAGE), slot) for slot in range(pages)]
                + [page_spec((N_HEADS, PAGE), slot) for slot in range(pages)]
                + [page_spec((2 * D_GROUP, PAGE), slot) for slot in range(pages)])
    grid_spec = pltpu.PrefetchScalarGridSpec(
        num_scalar_prefetch=3,
        grid=(b, n_steps),
        in_specs=in_specs,
        out_specs=(pl.BlockSpec((None, tq, D_GROUP), lambda bi, s, qi, ki, pt: (bi, qi[s], 0)),
                   per_seq(t_new, D_GROUP), per_seq(t_new, D_GROUP)),
        scratch_shapes=[pltpu.VMEM((N_HEADS, tq), F32), pltpu.VMEM((D_GROUP, tq), F32),
                        pltpu.VMEM((rows, 1), F32), pltpu.VMEM((rows, 1), F32),
                        pltpu.VMEM((rows, D_GROUP), F32), pltpu.VMEM((N_HEADS, 1), F32),
                        pltpu.VMEM((rows, D_GROUP), F32), pltpu.VMEM((rows, 1), F32)],
    )
    sds = jax.ShapeDtypeStruct
    return pl.pallas_call(
        functools.partial(_sb_prompt_kernel, tq=tq, tk=tk, pages=pages, groups=groups, tasks=tasks,
                          t_new=t_new),
        out_shape=(sds((b, t, D_GROUP), F32), sds((bs, t_new, D_GROUP), F32), sds((bs, t_new, D_GROUP), F32)),
        grid_spec=grid_spec,
        compiler_params=_cparams(("arbitrary", "arbitrary")),
        name="sb_prompt_attn",
    )(qi, ki, pt, qt, k, vt, tri, fwq, fnew, gt, grow, tri_page, swq, snew,
      *((fcache,) * pages + (lft_cache,) * pages + (scache,) * pages))


def _extract_heads(acc, t_new):
    hrow = lax.broadcasted_iota(jnp.int32, (N_HEADS, D_GROUP), 0)
    hcol = lax.broadcasted_iota(jnp.int32, (N_HEADS, D_GROUP), 1) // HEAD_DIM
    own = hrow == hcol
    outs = []
    for t in range(t_new):
        blk = acc[t * N_HEADS:(t + 1) * N_HEADS, :]
        outs.append(jnp.sum(jnp.where(own, blk, 0.0), axis=0, keepdims=True))
    return outs


def _fox_sample_stages(wq_ref, new_ref, gt_ref, grow_ref, tri_ref, cache_refs, lft_refs,
                       o_ref, m_ref, l_ref, acc_ref, dcar_ref, t_new):
    rows = t_new * N_HEADS

    def scores(kv_refs, biases, mask):
        s = []
        for kv_ref, bias in zip(kv_refs, biases):
            kt = kv_ref[0:D_GROUP, :].astype(BF16)
            s.append(_dot(wq_ref[...], kt) + jnp.concatenate([bias] * t_new, axis=0) + grow_ref[...])
        s = s[0] if len(s) == 1 else jnp.concatenate(s, axis=1)
        if mask is not None:
            s = jnp.where(mask, s, NEG_BIG)
        return s

    def keep(valid, new, old):
        return new if valid is None else jnp.where(valid, new, old)

    def attend(kv_refs, s, valid=None):
        m_old = m_ref[...]
        m_new = jnp.maximum(m_old, jnp.max(s, axis=-1, keepdims=True))
        alpha = jnp.exp(m_old - m_new)
        pr = jnp.exp(s - m_new)
        l_old = l_ref[...]
        l_ref[...] = keep(valid, alpha * l_old + jnp.sum(pr, axis=-1, keepdims=True), l_old)
        pv = None
        for i, kv_ref in enumerate(kv_refs):
            vt = kv_ref[D_GROUP:2 * D_GROUP, :].astype(BF16)
            t = _dot_nt(pr[:, i * PAGE:(i + 1) * PAGE].astype(BF16), vt)
            pv = t if pv is None else pv + t
        acc_old = acc_ref[...]
        acc_ref[...] = keep(valid, alpha * acc_old + pv, acc_old)
        m_ref[...] = keep(valid, m_new, m_old)

    def begin(first):
        @pl.when(first)
        def _():
            m_ref[...] = jnp.full_like(m_ref, NEG_BIG)
            l_ref[...] = jnp.zeros_like(l_ref)
            acc_ref[...] = jnp.zeros_like(acc_ref)
            dcar_ref[...] = jnp.zeros_like(dcar_ref)
            tok = lax.broadcasted_iota(jnp.int32, (rows, PAGE), 0) // N_HEADS
            key = lax.broadcasted_iota(jnp.int32, (rows, PAGE), 1)
            attend([new_ref], scores([new_ref], [-gt_ref[...]], key <= tok))

    def past_scores(valid=None):
        lfs = [r[...] for r in lft_refs]
        parts = []
        for lf in lfs:
            parts += [x.astype(F32) for x in _split_bf16(lf, N_SPLIT)]
        r = _dot(jnp.concatenate(parts, axis=0).astype(BF16), tri_ref[...])
        run0 = run = dcar_ref[...]
        biases = []
        for i, lf in enumerate(lfs):
            base = i * N_SPLIT * N_HEADS
            later = r[base:base + N_HEADS]
            for j in range(1, N_SPLIT):
                later = later + r[base + j * N_HEADS:base + (j + 1) * N_HEADS]
            biases.append(later + run)
            run = run + later[:, 0:1] + lf[:, 0:1]
        dcar_ref[...] = keep(valid, run, run0)
        return scores(cache_refs, biases, None)

    def past_attend(s, valid=None):
        attend(cache_refs, s, valid)

    def finish(last):
        @pl.when(last)
        def _():
            out = acc_ref[...] / l_ref[...]
            for t, rowv in enumerate(_extract_heads(out, t_new)):
                o_ref[t:t + 1, :] = rowv

    return begin, past_scores, past_attend, finish


def _sb_sample_stages(wq_ref, new_ref, tri_ref, cache_refs, o_ref, acc_ref, car_ref, t_new):
    rows = t_new * N_HEADS

    def logits(kv_refs, mask):
        zs, l1ms, parts = [], [], []
        for kv_ref in kv_refs:
            kt = kv_ref[0:D_GROUP, :].astype(BF16)
            z = _dot(wq_ref[...], kt)
            l1m = _neg_softplus(z)
            if mask is not None:
                l1m = jnp.where(mask, l1m, 0.0)
            zs.append(z)
            l1ms.append(l1m)
            parts += [x.astype(F32) for x in _split_bf16(l1m, 2)]
        return zs, l1ms, jnp.concatenate(parts, axis=0).astype(BF16)

    def later_sums(stacked):
        return _dot(stacked, tri_ref[...])

    def keep(valid, new, old):
        return new if valid is None else jnp.where(valid, new, old)

    def attend(kv_refs, zs, l1ms, r, mask, valid=None):
        run0 = run = car_ref[...]
        pv = None
        for i, kv_ref in enumerate(kv_refs):
            rest_i = r[2 * i * rows:(2 * i + 1) * rows] + r[(2 * i + 1) * rows:(2 * i + 2) * rows]
            wgt = jnp.exp(zs[i] + l1ms[i] + rest_i + run)
            if mask is not None:
                wgt = jnp.where(mask, wgt, 0.0)
            run = run + rest_i[:, 0:1] + l1ms[i][:, 0:1]
            vt = kv_ref[D_GROUP:2 * D_GROUP, :].astype(BF16)
            t = _dot_nt(wgt.astype(BF16), vt)
            pv = t if pv is None else pv + t
        car_ref[...] = keep(valid, run, run0)
        acc_old = acc_ref[...]
        acc_ref[...] = keep(valid, acc_old + pv, acc_old)

    def begin(first):
        @pl.when(first)
        def _():
            acc_ref[...] = jnp.zeros_like(acc_ref)
            car_ref[...] = jnp.zeros_like(car_ref)
            tok = lax.broadcasted_iota(jnp.int32, (rows, PAGE), 0) // N_HEADS
            key = lax.broadcasted_iota(jnp.int32, (rows, PAGE), 1)
            mask = key < tok
            zs, l1ms, stacked = logits([new_ref], mask)
            attend([new_ref], zs, l1ms, later_sums(stacked), mask)

    def past_logits():
        return logits(cache_refs, None)

    def past_attend(zs, l1ms, r, valid=None):
        attend(cache_refs, zs, l1ms, r, None, valid)

    def finish(last):
        @pl.when(last)
        def _():
            for t, rowv in enumerate(_extract_heads(acc_ref[...], t_new)):
                o_ref[t:t + 1, :] = rowv

    return begin, past_logits, later_sums, past_attend, finish


def _outproj_kernel(x_ref, y0_ref, y1_ref, y2_ref, y3_ref, g_ref, w_ref, o_ref):
    acc = x_ref[...]
    for gi, y_ref in enumerate((y0_ref, y1_ref, y2_ref, y3_ref)):
        y = y_ref[...]
        ms = jnp.mean(y * y, axis=-1, keepdims=True)
        yn = (y * lax.rsqrt(ms + EPS) * g_ref[gi:gi + 1, :]).astype(BF16)
        acc = acc + _dot(yn, w_ref[gi * D_GROUP:(gi + 1) * D_GROUP, :])
    o_ref[...] = acc


def _outproj(x, ys, g, w, layer, tm):
    m, d = x.shape
    yspec = pl.BlockSpec((tm, D_GROUP), lambda i: (i, 0))
    return pl.pallas_call(
        _outproj_kernel,
        out_shape=jax.ShapeDtypeStruct((m, d), F32),
        grid=(m // tm,),
        in_specs=[pl.BlockSpec((tm, d), lambda i: (i, 0)), yspec, yspec, yspec, yspec,
                  pl.BlockSpec((4, D_GROUP), lambda i: (0, 0)),
                  pl.BlockSpec((None, 4 * D_GROUP, d), lambda i: (layer, 0, 0))],
        out_specs=pl.BlockSpec((tm, d), lambda i: (i, 0)),
        compiler_params=_cparams(("arbitrary",)),
        name="outproj",
    )(x, *ys, g, w)


def _mlp_kernel(x_ref, g_ref, wu_ref, wd_ref, o_ref, xn_ref):
    @pl.when(pl.program_id(1) == 0)
    def _():
        x = x_ref[...]
        ms = jnp.mean(x * x, axis=-1, keepdims=True)
        xn_ref[...] = (x * lax.rsqrt(ms + EPS) * g_ref[...]).astype(BF16)
        o_ref[...] = x

    hid = jnp.maximum(_dot(xn_ref[...], wu_ref[...]), 0.0)
    o_ref[...] += _dot((hid * hid).astype(BF16), wd_ref[...])


def _mlp(x, g, wu, wd, layer, tm, tf):
    m, d = x.shape
    f = wu.shape[2]
    return pl.pallas_call(
        _mlp_kernel,
        out_shape=jax.ShapeDtypeStruct((m, d), F32),
        grid=(m // tm, f // tf),
        in_specs=[pl.BlockSpec((tm, d), lambda i, j: (i, 0)),
                  pl.BlockSpec((1, d), lambda i, j: (0, 0)),
                  pl.BlockSpec((None, d, tf), lambda i, j: (layer, 0, j)),
                  pl.BlockSpec((None, tf, d), lambda i, j: (layer, j, 0))],
        out_specs=pl.BlockSpec((tm, d), lambda i, j: (i, 0)),
        scratch_shapes=[pltpu.VMEM((tm, d), BF16)],
        compiler_params=_cparams(("arbitrary", "arbitrary"), 56),
        name="mlp",
    )(x, g, wu, wd)


def _block_diag(w):
    n, d, _ = w.shape
    eye = jnp.eye(n, dtype=w.dtype)
    return (eye[:, None, :, None] * w[:, :, None, :]).reshape(n * d, n * d)


def _input_weights(w_in):
    nf = 6 * D_GROUP
    wt = jnp.swapaxes(w_in, 1, 2)
    pad = jnp.zeros((wt.shape[0], LANES - N_HEADS, wt.shape[2]), wt.dtype)
    return jnp.concatenate([wt[:, :nf], wt[:, nf + N_HEADS:], wt[:, nf:nf + N_HEADS], pad], axis=1).astype(BF16)


def _layer_weights(l, norm1_g, pool_w, pool_scale, conv_w, conv_b, lru_wa, lru_ba, lru_wx, lru_bx,
                   lru_lambda, fox_bf, fox_q_g, fox_k_g, out_g, norm2_g):
    g = D_GROUP
    return dict(
        norm1_g=norm1_g[l][None, :],
        pool_w=pool_w[l].astype(BF16), pool_scale=pool_scale[l][None, :],
        conv_w=jnp.pad(conv_w[l], ((0, HIST_C - CONV_W), (0, 0))), conv_b=conv_b[l][None, :],
        wax=jnp.concatenate([_block_diag(lru_wa[l]), _block_diag(lru_wx[l])], axis=1).astype(BF16),
        ba=lru_ba[l][None, :], bx=lru_bx[l][None, :], lam=lru_lambda[l][None, :],
        bf=jnp.pad(fox_bf[l], (0, LANES - N_HEADS))[None, :],
        gq=jnp.tile(fox_q_g[l], N_HEADS)[None, :], gk=jnp.tile(fox_k_g[l], N_HEADS)[None, :],
        out_g=out_g[l].reshape(4, g), norm2_g=norm2_g[l][None, :],
    )


def _tile(m, pref):
    t = min(m, pref)
    while m % t:
        t //= 2
    return t


def _slot_constants():
    place = np.zeros((D_GROUP, D_AUG), np.float32)
    for c in range(D_GROUP):
        place[c, (c // HEAD_DIM) * LANES + c % HEAD_DIM] = 1.0
    placec = np.zeros((N_SPLIT, LANES, D_AUG), np.float32)
    neg = np.zeros((1, D_AUG), np.float32)
    for i in range(N_SPLIT):
        for h in range(N_HEADS):
            placec[i, h, h * LANES + HEAD_DIM + i] = 1.0
            neg[0, h * LANES + HEAD_DIM + i] = -1.0
    return jnp.asarray(place, BF16), jnp.asarray(placec, BF16), jnp.asarray(neg, F32)


def _expand_queries(q, b, t_new):
    q4 = q.reshape(b, t_new, 1, D_GROUP)
    own = (jnp.arange(D_GROUP)[None, :] // HEAD_DIM) == jnp.arange(N_HEADS)[:, None]
    return jnp.where(own[None, None], q4, jnp.zeros_like(q4)).reshape(b, t_new * N_HEADS, D_GROUP)


def _pages_view(cache):
    d, n = cache.shape[0], cache.shape[1]
    return jnp.transpose(cache, (0, 1, 3, 4, 5, 2)).reshape(d, n, 2 * D_GROUP, PAGE)


def kernel(x_prompt, x_sample, cache_fox_kv, cache_fox_logf, cache_sb_kv, state_pool, state_conv, state_lru,
           page_table, norm1_g, w_in, pool_w, pool_scale, conv_w, conv_b, lru_wa, lru_ba, lru_wx, lru_bx,
           lru_lambda, fox_bf, fox_q_g, fox_k_g, out_g, w_out, norm2_g, w_up, w_down):
    depth = w_in.shape[0]
    bp, seq, d_model = x_prompt.shape
    bs, t_new, _ = x_sample.shape
    past_len = page_table.shape[1] * PAGE
    g = D_GROUP

    seg = jnp.asarray(np.kron(np.eye(N_HEADS), np.ones((HEAD_DIM, HEAD_DIM))), BF16)
    place, placec, neg = _slot_constants()
    tk = _tile(seq, 256)
    tq = _tile(seq, 512)
    tri_q = jnp.asarray(np.triu(np.ones((tk, tk)), 1), BF16)
    tri_p = jnp.asarray(np.tril(np.ones((PAGE, PAGE)), -1), BF16)
    pt_flat = page_table[:, ::-1].reshape(-1).astype(jnp.int32)
    fox_cache = _pages_view(cache_fox_kv)
    sb_cache = _pages_view(cache_sb_kv)
    lft_cache = jnp.swapaxes(cache_fox_logf, 2, 3)
    t_pad = 8

    xp = x_prompt.reshape(bp * seq, d_model)
    xs = x_sample.reshape(bs * t_new, d_model)
    st_p, st_s = [], []
    w_in_b = _input_weights(w_in)
    w_out_b, w_up_b, w_down_b = w_out.astype(BF16), w_up.astype(BF16), w_down.astype(BF16)
    tf = _tile(w_up.shape[2], 512)
    for l in range(depth):
        w = _layer_weights(l, norm1_g, pool_w, pool_scale, conv_w, conv_b, lru_wa, lru_ba, lru_wx,
                           lru_bx, lru_lambda, fox_bf, fox_q_g, fox_k_g, out_g, norm2_g)

        m = bp * seq
        proj, flog = _inproj(xp, w["norm1_g"], w_in_b, l, _tile(m, 1024))
        r3 = lambda a: a.reshape(bp, seq, a.shape[-1])
        fkvt, skvt, lft, ct, fqt, fka, fvt, sqt, ska, svt = _prep_prompt(
            r3(proj), r3(flog), w["gq"], w["gk"], w["bf"], seg, place, placec, neg, _tile(seq, 256))
        y_fox = _fox_prompt_attention(fqt, fka, fvt, ct, tq, tq)

        ms = bs * t_new
        proj_s, flog_s = _inproj(xs, w["norm1_g"], w_in_b, l, ms)
        fkv, skv, lf, c_s, ct_s, qf, qs = _prep_sample(proj_s, flog_s, w["gq"], w["gk"], w["bf"], seg, t_new)
        new_page = lambda a: jnp.pad(jnp.swapaxes(a.reshape(bs, t_new, 2 * g), 1, 2),
                                     ((0, 0), (0, 0), (0, PAGE - t_new)))
        gt = jnp.pad(jnp.swapaxes(ct_s.reshape(N_HEADS, bs, t_new), 0, 1), ((0, 0), (0, 0), (0, PAGE - t_new)))
        grow = c_s[:, :N_HEADS].reshape(bs, t_new * N_HEADS, 1)
        y_sb, y_fox_s, y_sb_s = _sb_prompt_attention(
            sqt, ska, svt, tri_q, tq, tk, pt_flat,
            (_expand_queries(qf, bs, t_new), new_page(fkv), gt, grow, fox_cache, lft_cache),
            (_expand_queries(qs, bs, t_new), new_page(skv), sb_cache), tri_p, l, t_new, PAGES_PER_STEP)
        tt = _tile(seq, 256)
        y_pool, y_lru, pnew, cnew, hnew = _seqmix(
            r3(proj), jnp.zeros((bp, HIST_P, g), F32), jnp.zeros((bp, HIST_C, g), F32),
            jnp.zeros((bp, 1, g), F32), w["pool_w"], w["pool_scale"], w["conv_w"], w["conv_b"], w["wax"],
            w["ba"], w["bx"], w["lam"], tt, tt, 0)
        flat = lambda a: a.reshape(m, g)
        x1p = _outproj(xp, (flat(y_pool), flat(y_lru), flat(y_fox), flat(y_sb)), w["out_g"], w_out_b, l,
                       _tile(m, 512))
        xp = _mlp(x1p, w["norm2_g"], w_up_b, w_down_b, l, _tile(m, 1024), tf)
        kv_state = lambda a: jnp.transpose(a.reshape(bp, 2, N_HEADS, HEAD_DIM, seq), (0, 4, 1, 2, 3))
        st_p.append((kv_state(fkvt), jnp.swapaxes(lft, 1, 2), kv_state(skvt), pnew[:, 1:],
                     cnew[:, HIST_C - CONV_W + 1:], hnew[:, 0]))

        proj3 = jnp.pad(proj_s.reshape(bs, t_new, -1), ((0, 0), (0, t_pad - t_new), (0, 0)))
        y_pool, y_lru, pnew, cnew, hnew = _seqmix(
            proj3, jnp.pad(state_pool[l], ((0, 0), (HIST_P - POOL_MAX + 1, 0), (0, 0))),
            jnp.pad(state_conv[l], ((0, 0), (HIST_C - CONV_W + 1, 0), (0, 0))), state_lru[l][:, None, :],
            w["pool_w"], w["pool_scale"], w["conv_w"], w["conv_b"], w["wax"], w["ba"], w["bx"], w["lam"],
            t_pad, t_new, past_len)
        cut = lambda a: a[:, :t_new].reshape(ms, g)
        x1s = _outproj(xs, (cut(y_pool), cut(y_lru), y_fox_s.reshape(ms, g), y_sb_s.reshape(ms, g)), w["out_g"],
                       w_out_b, l, ms)
        xs = _mlp(x1s, w["norm2_g"], w_up_b, w_down_b, l, ms, tf)
        st_s.append((fkv.reshape(bs, t_new, 2, N_HEADS, HEAD_DIM), lf[:, :N_HEADS].reshape(bs, t_new, N_HEADS),
                     skv.reshape(bs, t_new, 2, N_HEADS, HEAD_DIM), pnew[:, 1:], cnew[:, HIST_C - CONV_W + 1:],
                     hnew[:, 0]))

    stk = lambda sts, j: jnp.stack([s[j] for s in sts], axis=0)
    return ((xp.reshape(bp, seq, d_model), xs.reshape(bs, t_new, d_model))
            + tuple(stk(st_p, j) for j in range(6)) + tuple(stk(st_s, j) for j in range(6)))
```
